```python
import jax, jax.numpy as jnp
from jax import lax
import numpy as np

D_MODEL = 1024
BATCH = 8
SEQ = 2048
DEPTH = 1

GRID_W = 64
CTX_LEN = 256
N_HEADS_NA = 8
HEAD_DIM = 64
D_NA = N_HEADS_NA * HEAD_DIM
NA_KH = 8
NA_KW = 16
D_SG = D_MODEL - D_NA
N_GROUPS_SG = 8
SG_GROUP_DIM = D_SG // N_GROUPS_SG
CHUNK = 128
D_MIX = D_NA + D_SG
D_IN = 3 * D_NA + 2 * D_SG
N_EXPERTS = 16
EC_CAPACITY_FACTOR = 2
D_EXPERT = 1536
N_MOD = 6
EPS = 1e-6
NEG_INF = -1e30

kernel_name = "hybrid_na_gmlp_ec_moe_dit"


def rmsnorm(x, g):
    xf = x.astype(jnp.float32)
    xf = xf * lax.rsqrt(jnp.mean(xf * xf, axis=-1, keepdims=True) + EPS)
    return xf.astype(x.dtype) * g


def layernorm(x, g):
    xf = x.astype(jnp.float32)
    mu = jnp.mean(xf, axis=-1, keepdims=True)
    var = jnp.mean(jnp.square(xf - mu), axis=-1, keepdims=True)
    return ((xf - mu) * lax.rsqrt(var + EPS)).astype(x.dtype) * g


def modulate(h, shift, scale):
    return h * (1 + scale) + shift


def split_in(z):
    q = z[..., :D_NA]
    k = z[..., D_NA:2 * D_NA]
    v = z[..., 2 * D_NA:3 * D_NA]
    u = z[..., 3 * D_NA:3 * D_NA + D_SG]
    zg = z[..., 3 * D_NA + D_SG:]
    return q, k, v, u, zg


def heads(t):
    return t.reshape(t.shape[:-1] + (N_HEADS_NA, HEAD_DIM))


def neighbourhood_attention(q, k, v, k_ctx, v_ctx, rpb):
    b, n, h, dh = q.shape
    rows = n // GRID_W
    kh = min(NA_KH, rows)
    r = np.arange(rows)
    row_start = np.clip(r - kh // 2, 0, rows - kh)
    row_idx = row_start[:, None] + np.arange(kh)[None, :]
    col = np.arange(GRID_W)
    col_start = np.clip(col - NA_KW // 2, 0, GRID_W - NA_KW)
    col_mask = (col[None, :] >= col_start[:, None]) & (col[None, :] < col_start[:, None] + NA_KW)
    row_off = row_idx - r[:, None] + NA_KH - 1
    col_off = np.clip(col[None, :] - col[:, None] + NA_KW - 1, 0, 2 * NA_KW - 2)
    bias = rpb[:, row_off[:, None, :, None], col_off[None, :, None, :]]

    qg = q.reshape(b, rows, GRID_W, h, dh)
    kg = k.reshape(b, rows, GRID_W, h, dh)[:, row_idx]
    vg = v.reshape(b, rows, GRID_W, h, dh)[:, row_idx]
    scale = dh ** -0.5
    s_nb = jnp.einsum('brqhd,brikhd->bhrqik', qg, kg).astype(jnp.float32) * scale + bias.astype(jnp.float32)[None]
    s_nb = jnp.where(col_mask[:, None, :], s_nb, NEG_INF)
    s_ctx = jnp.einsum('brqhd,bjhd->bhrqj', qg, k_ctx).astype(jnp.float32) * scale
    s = jnp.concatenate([s_nb.reshape(b, h, rows, GRID_W, kh * GRID_W), s_ctx], axis=-1)
    p = jax.nn.softmax(s, axis=-1).astype(v.dtype)
    p_nb = p[..., :kh * GRID_W].reshape(b, h, rows, GRID_W, kh, GRID_W)
    p_ctx = p[..., kh * GRID_W:]
    o = jnp.einsum('bhrqik,brikhd->brqhd', p_nb, vg) + jnp.einsum('bhrqj,bjhd->brqhd', p_ctx, v_ctx)
    return o.reshape(b, n, h * dh)


def context_attention(q, k, v):
    b, l, h, dh = q.shape
    s = jnp.einsum('bqhd,bkhd->bhqk', q, k).astype(jnp.float32) * dh ** -0.5
    p = jax.nn.softmax(s, axis=-1).astype(v.dtype)
    return jnp.einsum('bhqk,bkhd->bqhd', p, v).reshape(b, l, h * dh)


def chunk_gmlp(u, zg, w_s, b_s, g_norm):
    b, n, _ = u.shape
    u = jax.nn.gelu(u)
    zg = layernorm(jax.nn.gelu(zg), g_norm)
    zc = zg.reshape(b, n // CHUNK, CHUNK, N_GROUPS_SG, SG_GROUP_DIM)
    sp = jnp.einsum('gpq,bcqgd->bcpgd', w_s, zc) + b_s.T[None, None, :, :, None]
    return u * sp.reshape(b, n, D_SG)


def mix_out(o_a, o_b, out_norm_a, out_norm_b, w_out):
    o = jnp.concatenate([rmsnorm(o_a, out_norm_a), rmsnorm(o_b, out_norm_b)], axis=-1)
    return o @ w_out


def expert_choice_ffn(h, w_router, w_gate, w_up, w_down):
    b, n, d = h.shape
    cap = EC_CAPACITY_FACTOR * n // N_EXPERTS
    aff = jax.nn.softmax((h @ w_router).astype(jnp.float32), axis=-1)
    gates, idx = lax.top_k(jnp.swapaxes(aff, 1, 2), cap)
    xs = jax.vmap(lambda hb, ib: hb[ib])(h, idx)
    a = jnp.einsum('becd,edf->becf', xs, w_gate)
    up = jnp.einsum('becd,edf->becf', xs, w_up)
    y = jnp.einsum('becf,efd->becd', jax.nn.silu(a) * up, w_down) * gates[..., None].astype(h.dtype)
    return jax.vmap(lambda ib, yb: jnp.zeros((n, d), yb.dtype).at[ib.reshape(-1)].add(yb.reshape(-1, d)))(idx, y)


def setup_inputs(seed: int = 0) -> dict:
    key = jax.random.key(seed)
    ks = jax.random.split(key, 22)
    nrm = jax.random.normal
    f32 = jnp.float32
    return {
        "x": nrm(ks[0], (BATCH, SEQ, D_MODEL), f32),
        "c": nrm(ks[1], (BATCH, D_MODEL), f32),
        "ctx": nrm(ks[2], (BATCH, CTX_LEN, D_MODEL), f32),
        "c_ctx": nrm(ks[3], (D_MODEL,), f32),
        "w_mod": nrm(ks[4], (DEPTH, D_MODEL, N_MOD * D_MODEL), f32) * (0.5 * D_MODEL ** -0.5),
        "b_mod": nrm(ks[5], (DEPTH, N_MOD * D_MODEL), f32) * 0.01,
        "norm1": 1.0 + 0.02 * nrm(ks[6], (DEPTH, D_MODEL), f32),
        "w_in": nrm(ks[7], (DEPTH, D_MODEL, D_IN), f32) * D_MODEL ** -0.5,
        "rpb": 0.1 * nrm(ks[8], (DEPTH, N_HEADS_NA, 2 * NA_KH - 1, 2 * NA_KW - 1), f32),
        "w_s": nrm(ks[9], (DEPTH, N_GROUPS_SG, CHUNK, CHUNK), f32) * CHUNK ** -0.5,
        "b_s": 1.0 + 0.02 * nrm(ks[10], (DEPTH, N_GROUPS_SG, CHUNK), f32),
        "gmlp_norm": 1.0 + 0.02 * nrm(ks[11], (DEPTH, D_SG), f32),
        "out_norm_a": 1.0 + 0.02 * nrm(ks[12], (DEPTH, D_NA), f32),
        "out_norm_b": 1.0 + 0.02 * nrm(ks[13], (DEPTH, D_SG), f32),
        "w_out": nrm(ks[14], (DEPTH, D_MIX, D_MODEL), f32) * D_MIX ** -0.5,
        "norm2": 1.0 + 0.02 * nrm(ks[15], (DEPTH, D_MODEL), f32),
        "w_router": nrm(ks[16], (DEPTH, D_MODEL, N_EXPERTS), f32) * D_MODEL ** -0.5,
        "w_gate": nrm(ks[17], (DEPTH, N_EXPERTS, D_MODEL, D_EXPERT), f32) * D_MODEL ** -0.5,
        "w_up": nrm(ks[18], (DEPTH, N_EXPERTS, D_MODEL, D_EXPERT), f32) * D_MODEL ** -0.5,
        "w_down": nrm(ks[19], (DEPTH, N_EXPERTS, D_EXPERT, D_MODEL), f32) * D_EXPERT ** -0.5,
        "norm_final": 1.0 + 0.02 * nrm(ks[20], (D_MODEL,), f32),
    }


def reference(x, c, ctx, c_ctx, w_mod, b_mod, norm1, w_in, rpb, w_s, b_s, gmlp_norm,
              out_norm_a, out_norm_b, w_out, norm2, w_router, w_gate, w_up, w_down, norm_final):
    b = x.shape[0]
    for l in range(DEPTH):
        last = l == DEPTH - 1
        m = (jax.nn.silu(c) @ w_mod[l] + b_mod[l]).reshape(b, 1, N_MOD, D_MODEL)
        sh1, sc1, g1, sh2, sc2, g2 = (m[:, :, i] for i in range(N_MOD))
        mc = (jax.nn.silu(c_ctx) @ w_mod[l] + b_mod[l]).reshape(N_MOD, D_MODEL)
        sh1c, sc1c, g1c, sh2c, sc2c, g2c = (mc[i] for i in range(N_MOD))

        h = modulate(rmsnorm(x, norm1[l]), sh1, sc1)
        hc = modulate(rmsnorm(ctx, norm1[l]), sh1c, sc1c)
        q, k, v, u, zg = split_in(h @ w_in[l])
        if last:
            kv_c = hc @ w_in[l][:, D_NA:3 * D_NA]
            k_c, v_c = kv_c[..., :D_NA], kv_c[..., D_NA:]
        else:
            q_c, k_c, v_c, u_c, zg_c = split_in(hc @ w_in[l])
        o_a = neighbourhood_attention(heads(q), heads(k), heads(v), heads(k_c), heads(v_c), rpb[l])
        o_b = chunk_gmlp(u, zg, w_s[l], b_s[l], gmlp_norm[l])
        x_new = x + g1 * mix_out(o_a, o_b, out_norm_a[l], out_norm_b[l], w_out[l])

        if not last:
            o_ac = context_attention(heads(q_c), heads(k_c), heads(v_c))
            o_bc = chunk_gmlp(u_c, zg_c, w_s[l], b_s[l], gmlp_norm[l])
            ctx = ctx + g1c * mix_out(o_ac, o_bc, out_norm_a[l], out_norm_b[l], w_out[l])
            hc2 = modulate(rmsnorm(ctx, norm2[l]), sh2c, sc2c)
            ctx = ctx + g2c * expert_choice_ffn(hc2, w_router[l], w_gate[l], w_up[l], w_down[l])

        x = x_new
        h2 = modulate(rmsnorm(x, norm2[l]), sh2, sc2)
        x = x + g2 * expert_choice_ffn(h2, w_router[l], w_gate[l], w_up[l], w_down[l])
    return rmsnorm(x, norm_final)
```

```python
import functools

import numpy as np
import jax
import jax.numpy as jnp
from jax import lax
from jax.experimental import pallas as pl
from jax.experimental.pallas import tpu as pltpu

D_MODEL = 1024
GRID_W = 64
CTX_LEN = 256
N_HEADS_NA = 8
HEAD_DIM = 64
D_NA = N_HEADS_NA * HEAD_DIM
NA_KH = 8
NA_KW = 16
D_SG = D_MODEL - D_NA
N_GROUPS_SG = 8
SG_GROUP_DIM = D_SG // N_GROUPS_SG
CHUNK = 128
N_EXPERTS = 16
EC_CAPACITY_FACTOR = 2
D_EXPERT = 1536
N_MOD = 6
EPS = 1e-6
NEG_INF = -1e30

LANES = 128
VMEM_LIMIT = 56 * 1024 * 1024

F32 = jnp.float32
BF16 = jnp.bfloat16
HIGHEST = lax.Precision.HIGHEST

Q_ROWS = 4
W_ROWS = 12
TQ = Q_ROWS * GRID_W
TW = W_ROWS * GRID_W

MANTISSA_STEPS = 36


def _params(*sem):
    return pltpu.CompilerParams(dimension_semantics=sem, vmem_limit_bytes=VMEM_LIMIT)


def _rms_mod(x, g, shift, scale):
    r = lax.rsqrt(jnp.mean(x * x, axis=-1, keepdims=True) + EPS)
    return (x * r) * g * (1.0 + scale) + shift


def _rms(x, g):
    return x * lax.rsqrt(jnp.mean(x * x, axis=-1, keepdims=True) + EPS) * g


def _gelu_tanh(x):
    return 0.5 * x * (1.0 + jnp.tanh(np.sqrt(2.0 / np.pi).astype(np.float32) * (x + 0.044715 * (x * x * x))))


def _silu(x):
    return x * jax.nn.sigmoid(x)


def _mod_kernel(c_ref, w_ref, b_ref, o_ref):
    s = _silu(c_ref[...])
    o_ref[...] = jnp.dot(s, w_ref[...], precision=HIGHEST, preferred_element_type=F32) + b_ref[...]


def _modulation(cc, w_mod, b_mod):
    rows, d = cc.shape
    n = w_mod.shape[1]
    tn = 1024
    return pl.pallas_call(
        _mod_kernel,
        grid=(n // tn,),
        in_specs=[pl.BlockSpec((rows, d), lambda j: (0, 0)),
                  pl.BlockSpec((d, tn), lambda j: (0, j)),
                  pl.BlockSpec((1, tn), lambda j: (0, j))],
        out_specs=pl.BlockSpec((rows, tn), lambda j: (0, j)),
        out_shape=jax.ShapeDtypeStruct((rows, n), F32),
        compiler_params=_params("arbitrary"),
        name="modulation",
    )(cc, w_mod, b_mod)


def _in_kernel(x_ref, mod_ref, n1_ref, w_ref, ws_ref, bs_ref, gn_ref, onb_ref,
               q_ref, k_ref, v_ref, ob_ref, sp_ref):
    x = x_ref[0]
    mod = mod_ref[0]
    hb = _rms_mod(x, n1_ref[...], mod[0:1], mod[1:2]).astype(BF16)
    qkv = jnp.dot(hb, w_ref[:, :3 * D_NA], preferred_element_type=F32)
    q_ref[0] = (qkv[:, :D_NA] * (HEAD_DIM ** -0.5)).astype(BF16)
    k_ref[0] = qkv[:, D_NA:2 * D_NA].astype(BF16)
    v_ref[0] = qkv[:, 2 * D_NA:].astype(BF16)

    uz = jnp.dot(hb, w_ref[:, 3 * D_NA:], preferred_element_type=F32)
    u = _gelu_tanh(uz[:, :D_SG])
    z = _gelu_tanh(uz[:, D_SG:])
    mu = jnp.mean(z, axis=-1, keepdims=True)
    zc = z - mu
    var = jnp.mean(zc * zc, axis=-1, keepdims=True)
    zb = (zc * lax.rsqrt(var + EPS) * gn_ref[...]).astype(BF16)

    tm = x.shape[0]
    nch = tm // CHUNK
    lane = lax.broadcasted_iota(jnp.int32, (CHUNK, LANES), 1)
    first = lane < SG_GROUP_DIM
    for p in range(N_GROUPS_SG // 2):
        zp = jnp.concatenate(
            [zb[c * CHUNK:(c + 1) * CHUNK, p * LANES:(p + 1) * LANES] for c in range(nch)], axis=1)
        r = jnp.dot(ws_ref[p], zp, preferred_element_type=F32)
        bs = bs_ref[p]
        for c in range(nch):
            top = r[:CHUNK, c * LANES:(c + 1) * LANES] + bs[:CHUNK]
            bot = r[CHUNK:, c * LANES:(c + 1) * LANES] + bs[CHUNK:]
            sp_ref[c * CHUNK:(c + 1) * CHUNK, p * LANES:(p + 1) * LANES] = jnp.where(first, top, bot)
    ob = u * sp_ref[...]
    ob_ref[0] = _rms(ob, onb_ref[...]).astype(BF16)


def _in_proj(x, mod, norm1, w_in_b, ws2, bs2, gmlp_norm, out_norm_b, tm=512):
    b, n, d = x.shape
    d_in = w_in_b.shape[1]
    full2 = lambda i, j: (0, 0)
    full3 = lambda i, j: (0, 0, 0)
    tile = lambda i, j: (i, j, 0)
    act = jax.ShapeDtypeStruct((b, n, D_NA), BF16)
    return pl.pallas_call(
        _in_kernel,
        grid=(b, n // tm),
        in_specs=[pl.BlockSpec((1, tm, d), tile),
                  pl.BlockSpec((1, N_MOD, d), lambda i, j: (i, 0, 0)),
                  pl.BlockSpec((1, d), full2),
                  pl.BlockSpec((d, d_in), full2),
                  pl.BlockSpec(ws2.shape, full3),
                  pl.BlockSpec(bs2.shape, full3),
                  pl.BlockSpec((1, D_SG), full2),
                  pl.BlockSpec((1, D_SG), full2)],
        out_specs=[pl.BlockSpec((1, tm, D_NA), tile)] * 4,
        out_shape=[act] * 4,
        scratch_shapes=[pltpu.VMEM((tm, D_SG), F32)],
        compiler_params=_params("parallel", "arbitrary"),
        name="in_proj_gmlp",
    )(x, mod, norm1, w_in_b, ws2, bs2, gmlp_norm, out_norm_b)


def _ctx_kernel(x_ref, mod_ref, n1_ref, w_ref, k_ref, v_ref):
    mod = mod_ref[...]
    hb = _rms_mod(x_ref[...], n1_ref[...], mod[0:1], mod[1:2]).astype(BF16)
    kv = jnp.dot(hb, w_ref[...], preferred_element_type=F32)
    k_ref[...] = kv[:, :D_NA].astype(BF16)
    v_ref[...] = kv[:, D_NA:].astype(BF16)


def _ctx_proj(ctx2, mod_c, norm1, w_kv_b, tm=512):
    rows, d = ctx2.shape
    full = lambda i: (0, 0)
    act = jax.ShapeDtypeStruct((rows, D_NA), BF16)
    return pl.pallas_call(
        _ctx_kernel,
        grid=(rows // tm,),
        in_specs=[pl.BlockSpec((tm, d), lambda i: (i, 0)),
                  pl.BlockSpec((N_MOD, d), full),
                  pl.BlockSpec((1, d), full),
                  pl.BlockSpec(w_kv_b.shape, full)],
        out_specs=[pl.BlockSpec((tm, D_NA), lambda i: (i, 0))] * 2,
        out_shape=[act, act],
        compiler_params=_params("arbitrary"),
        name="ctx_kv_proj",
    )(ctx2, mod_c, norm1, w_kv_b)


def _window_start(t, rows):
    return jnp.clip(Q_ROWS * t - NA_KH // 2, 0, rows - W_ROWS)


def _attn_kernel(q_ref, k_ref, v_ref, kc_ref, vc_ref, bias_ref, ona_ref, o_ref, *, rows):
    t = pl.program_id(1)
    ws = pl.multiple_of(_window_start(t, rows) * GRID_W, GRID_W)
    kw = k_ref[0, pl.ds(ws, TW), :]
    vw = v_ref[0, pl.ds(ws, TW), :]
    q = q_ref[0]
    kc = kc_ref[0]
    vc = vc_ref[0]
    lane = lax.broadcasted_iota(jnp.int32, (TQ, LANES), 1)
    first = lane < HEAD_DIM
    nt = (((1,), (1,)), ((), ()))
    outs = []
    for p in range(N_HEADS_NA // 2):
        sl = slice(p * LANES, (p + 1) * LANES)
        qp, kp, vp, kcp, vcp = q[:, sl], kw[:, sl], vw[:, sl], kc[:, sl], vc[:, sl]
        halves = []
        for j in range(2):
            qm = jnp.where(first if j == 0 else jnp.logical_not(first), qp, jnp.zeros_like(qp))
            s_nb = lax.dot_general(qm, kp, nt, preferred_element_type=F32) + bias_ref[0, 2 * p + j]
            s_cx = lax.dot_general(qm, kcp, nt, preferred_element_type=F32)
            m = jnp.maximum(jnp.max(s_nb, axis=1, keepdims=True), jnp.max(s_cx, axis=1, keepdims=True))
            e_nb = jnp.exp(s_nb - m)
            e_cx = jnp.exp(s_cx - m)
            l = jnp.sum(e_nb, axis=1, keepdims=True) + jnp.sum(e_cx, axis=1, keepdims=True)
            o = (jnp.dot(e_nb.astype(BF16), vp, preferred_element_type=F32)
                 + jnp.dot(e_cx.astype(BF16), vcp, preferred_element_type=F32))
            halves.append(o / l)
        outs.append(jnp.where(first, halves[0], halves[1]))
    o = jnp.concatenate(outs, axis=1)
    o_ref[0] = _rms(o, ona_ref[...]).astype(BF16)


def _bias_tables(rpb, rows):
    n_tiles = rows // Q_ROWS
    n_ro, n_co = 2 * NA_KH - 1, 2 * NA_KW - 1
    qc = np.arange(GRID_W)[:, None]
    kc = np.arange(GRID_W)[None, :]
    cs = np.clip(qc - NA_KW // 2, 0, GRID_W - NA_KW)
    col_ok = (kc >= cs) & (kc < cs + NA_KW)
    spread = ((kc - qc + NA_KW - 1)[None] == np.arange(n_co)[:, None, None]) & col_ok[None]
    blocks = jnp.dot(rpb.reshape(-1, n_co), jnp.asarray(spread.reshape(n_co, -1), F32), precision=HIGHEST)
    blocks = jnp.where(jnp.asarray(col_ok.reshape(-1)), blocks, NEG_INF)
    blocks = blocks.reshape(N_HEADS_NA, n_ro, GRID_W, GRID_W)
    masked = jnp.full((N_HEADS_NA, GRID_W, GRID_W), NEG_INF, F32)
    classes = []
    for t in (0, 1, n_tiles - 1):
        ws = int(np.clip(Q_ROWS * t - NA_KH // 2, 0, rows - W_ROWS))
        tile_rows = []
        for rho in range(Q_ROWS):
            r = Q_ROWS * t + rho
            rs = int(np.clip(r - NA_KH // 2, 0, rows - NA_KH))
            parts = [blocks[:, kr - r + NA_KH - 1] if rs <= kr < rs + NA_KH else masked
                     for kr in range(ws, ws + W_ROWS)]
            tile_rows.append(jnp.concatenate(parts, axis=-1))
        classes.append(jnp.concatenate(tile_rows, axis=1))
    return jnp.stack(classes)


def _attention(q, k, v, kc, vc, bias, out_norm_a):
    b, n, _ = q.shape
    rows = n // GRID_W
    n_tiles = rows // Q_ROWS

    def bias_map(i, t):
        return (jnp.where(t == 0, 0, jnp.where(t == n_tiles - 1, 2, 1)), 0, 0, 0)

    per_b = lambda i, t: (i, 0, 0)
    return pl.pallas_call(
        functools.partial(_attn_kernel, rows=rows),
        grid=(b, n_tiles),
        in_specs=[pl.BlockSpec((1, TQ, D_NA), lambda i, t: (i, t, 0)),
                  pl.BlockSpec((1, n, D_NA), per_b),
                  pl.BlockSpec((1, n, D_NA), per_b),
                  pl.BlockSpec((1, CTX_LEN, D_NA), per_b),
                  pl.BlockSpec((1, CTX_LEN, D_NA), per_b),
                  pl.BlockSpec((1, N_HEADS_NA, TQ, TW), bias_map),
                  pl.BlockSpec((1, D_NA), lambda i, t: (0, 0))],
        out_specs=pl.BlockSpec((1, TQ, D_NA), lambda i, t: (i, t, 0)),
        out_shape=jax.ShapeDtypeStruct((b, n, D_NA), BF16),
        compiler_params=_params("parallel", "arbitrary"),
        name="nbr_attention",
    )(q, k, v, kc, vc, bias, out_norm_a)


def _out_kernel(oa_ref, ob_ref, x_ref, mod_ref, w_ref, n2_ref, wr_ref, xn_ref, h2_ref, aff_ref):
    mix = (jnp.dot(oa_ref[0], w_ref[:D_NA], preferred_element_type=F32)
           + jnp.dot(ob_ref[0], w_ref[D_NA:], preferred_element_type=F32))
    mod = mod_ref[0]
    xn = x_ref[0] + mod[2:3] * mix
    xn_ref[0] = xn
    h2 = _rms_mod(xn, n2_ref[...], mod[3:4], mod[4:5])
    h2_ref[0] = h2.astype(BF16)
    logits = lax.dot_general(wr_ref[...], h2, (((1,), (1,)), ((), ())),
                             precision=HIGHEST, preferred_element_type=F32)
    e = jnp.exp(logits - jnp.max(logits, axis=0, keepdims=True))
    aff_ref[0] = e / jnp.sum(e, axis=0, keepdims=True)


def _out_proj(oa, ob, x, mod, w_out_b, norm2, w_router_t, tm=512):
    b, n, d = x.shape
    tile = lambda i, j: (i, j, 0)
    full = lambda i, j: (0, 0)
    return pl.pallas_call(
        _out_kernel,
        grid=(b, n // tm),
        in_specs=[pl.BlockSpec((1, tm, D_NA), tile),
                  pl.BlockSpec((1, tm, D_SG), tile),
                  pl.BlockSpec((1, tm, d), tile),
                  pl.BlockSpec((1, N_MOD, d), lambda i, j: (i, 0, 0)),
                  pl.BlockSpec(w_out_b.shape, full),
                  pl.BlockSpec((1, d), full),
                  pl.BlockSpec((N_EXPERTS, d), full)],
        out_specs=[pl.BlockSpec((1, tm, d), tile),
                   pl.BlockSpec((1, tm, d), tile),
                   pl.BlockSpec((1, N_EXPERTS, tm), lambda i, j: (i, 0, j))],
        out_shape=[jax.ShapeDtypeStruct((b, n, d), F32),
                   jax.ShapeDtypeStruct((b, n, d), BF16),
                   jax.ShapeDtypeStruct((b, N_EXPERTS, n), F32)],
        compiler_params=_params("parallel", "arbitrary"),
        name="out_proj_router",
    )(oa, ob, x, mod, w_out_b, norm2, w_router_t)


def _prefix_count(mask_f, tri):
    rows, n = mask_f.shape
    parts = []
    carry = jnp.zeros((rows, 1), F32)
    for j in range(n // LANES):
        blk = mask_f[:, j * LANES:(j + 1) * LANES]
        parts.append(jnp.dot(blk.astype(BF16), tri, preferred_element_type=F32) + carry)
        carry = carry + jnp.sum(blk, axis=1, keepdims=True)
    return jnp.concatenate(parts, axis=1)


def _topk_kernel(aff_ref, slot_ref, *, cap):
    a = aff_ref[...]
    rows = a.shape[0]

    def enough(t):
        return jnp.sum(jnp.where(a >= t, 1.0, 0.0), axis=1, keepdims=True) >= cap

    tiny = jnp.full((rows, 1), 2.0 ** -126, F32)
    normal = enough(tiny)
    pw = tiny
    hi = jnp.full((rows, 1), 4.0, F32)
    for bit in range(6, -1, -1):
        cand = pw * (2.0 ** (1 << bit))
        ok = enough(cand)
        pw = jnp.where(ok, cand, pw)
        hi = jnp.where(ok, hi, cand)
    lo = jnp.where(normal, pw, 0.0)
    hi = jnp.where(normal, hi, tiny)
    step = lo
    for _ in range(MANTISSA_STEPS):
        step = step * 0.5
        cand = lo + step
        ok = enough(cand)
        lo = jnp.where(ok, cand, lo)
        hi = jnp.where(ok, hi, cand)
    above = a >= hi
    tie = jnp.logical_and(a >= lo, jnp.logical_not(above))
    n_above = jnp.sum(jnp.where(above, 1.0, 0.0), axis=1, keepdims=True)
    ri = lax.broadcasted_iota(jnp.int32, (LANES, LANES), 0)
    ci = lax.broadcasted_iota(jnp.int32, (LANES, LANES), 1)
    tri = jnp.where(ri <= ci, 1.0, 0.0).astype(BF16)
    tie_rank = _prefix_count(jnp.where(tie, 1.0, 0.0), tri)
    sel = jnp.logical_or(above, jnp.logical_and(tie, tie_rank <= cap - n_above))
    pos = _prefix_count(jnp.where(sel, 1.0, 0.0), tri) - 1.0
    slot_ref[...] = jnp.where(sel, pos, -1.0).astype(jnp.int32)


def _topk_slots(aff2, cap):
    rows, n = aff2.shape
    tr = 32
    return pl.pallas_call(
        functools.partial(_topk_kernel, cap=cap),
        grid=(rows // tr,),
        in_specs=[pl.BlockSpec((tr, n), lambda i: (i, 0))],
        out_specs=pl.BlockSpec((tr, n), lambda i: (i, 0)),
        out_shape=jax.ShapeDtypeStruct((rows, n), jnp.int32),
        compiler_params=_params("parallel"),
        name="expert_topk",
    )(aff2)


def _gather_kernel(h2_ref, slot_ref, aff_ref, xs_ref, gate_ref, *, cap, eg):
    h2 = h2_ref[0]
    n = h2.shape[0]
    ci = lax.broadcasted_iota(jnp.int32, (cap, n), 0)
    for j in range(eg):
        mask = slot_ref[0, 0, j:j + 1, :] == ci
        onehot = jnp.where(mask, 1.0, 0.0).astype(BF16)
        xs_ref[0, j] = jnp.dot(onehot, h2, preferred_element_type=F32).astype(BF16)
        gate_ref[0, j] = jnp.sum(jnp.where(mask, aff_ref[0, 0, j:j + 1, :], 0.0), axis=1, keepdims=True)


def _gather_tokens(h2, slot, aff, cap, eg=4):
    b, n, d = h2.shape
    e = slot.shape[1]
    slot4 = slot.reshape(b, e // eg, eg, n)
    aff4 = aff.reshape(b, e // eg, eg, n)
    meta = pl.BlockSpec((1, 1, eg, n), lambda i, j: (i, j, 0, 0))
    return pl.pallas_call(
        functools.partial(_gather_kernel, cap=cap, eg=eg),
        grid=(b, e // eg),
        in_specs=[pl.BlockSpec((1, n, d), lambda i, j: (i, 0, 0)), meta, meta],
        out_specs=[pl.BlockSpec((1, eg, cap, d), lambda i, j: (i, j, 0, 0)),
                   pl.BlockSpec((1, eg, cap, 1), lambda i, j: (i, j, 0, 0))],
        out_shape=[jax.ShapeDtypeStruct((b, e, cap, d), BF16),
                   jax.ShapeDtypeStruct((b, e, cap, 1), F32)],
        compiler_params=_params("parallel", "arbitrary"),
        name="moe_gather",
    )(h2, slot4, aff4)


def _expert_kernel(xs_ref, wg_ref, wu_ref, wd_ref, gate_ref, y_ref, acc_ref, *, rb):
    f = pl.program_id(1)
    wg = wg_ref[0].astype(BF16)
    wu = wu_ref[0].astype(BF16)
    wd = wd_ref[0].astype(BF16)
    nb, _, cap, d = xs_ref.shape
    for i in range(nb // rb):
        xs = xs_ref[i * rb:(i + 1) * rb, 0].reshape(rb * cap, d)
        a = jnp.dot(xs, wg, preferred_element_type=F32)
        u = jnp.dot(xs, wu, preferred_element_type=F32)
        hm = (_silu(a) * u).astype(BF16)
        part = jnp.dot(hm, wd, preferred_element_type=F32).reshape(rb, cap, d)

        @pl.when(f == 0)
        def _():
            acc_ref[i * rb:(i + 1) * rb] = part

        @pl.when(f != 0)
        def _():
            acc_ref[i * rb:(i + 1) * rb] += part

    @pl.when(f == pl.num_programs(1) - 1)
    def _():
        y_ref[:, 0] = (acc_ref[...] * gate_ref[:, 0]).astype(BF16)


def _experts(xs, w_gate, w_up, w_down, gate, fc=512, rb=4):
    b, e, cap, d = xs.shape
    dff = w_gate.shape[2]
    per_e = lambda i, f: (0, i, 0, 0)
    return pl.pallas_call(
        functools.partial(_expert_kernel, rb=rb),
        grid=(e, dff // fc),
        in_specs=[pl.BlockSpec((b, 1, cap, d), per_e),
                  pl.BlockSpec((1, d, fc), lambda i, f: (i, 0, f)),
                  pl.BlockSpec((1, d, fc), lambda i, f: (i, 0, f)),
                  pl.BlockSpec((1, fc, d), lambda i, f: (i, f, 0)),
                  pl.BlockSpec((b, 1, cap, 1), per_e)],
        out_specs=pl.BlockSpec((b, 1, cap, d), per_e),
        out_shape=jax.ShapeDtypeStruct((b, e, cap, d), BF16),
        scratch_shapes=[pltpu.VMEM((b, cap, d), F32)],
        compiler_params=_params("parallel", "arbitrary"),
        name="moe_experts",
    )(xs, w_gate, w_up, w_down, gate)


def _combine_kernel(slot_t_ref, y_ref, xn_ref, mod_ref, nf_ref, o_ref, *, cap):
    st = slot_t_ref[0]
    tn, e = st.shape
    ci = lax.broadcasted_iota(jnp.int32, (tn, cap), 1)
    scat = jnp.concatenate(
        [jnp.where(st[:, j:j + 1] == ci, 1.0, 0.0).astype(BF16) for j in range(e)], axis=1)
    moe = jnp.dot(scat, y_ref[0], preferred_element_type=F32)
    x = xn_ref[0] + mod_ref[0][5:6] * moe
    o_ref[0] = _rms(x, nf_ref[...])


def _combine(slot_t, y2, x_new, mod, norm_final, cap, tn=512):
    b, n, d = x_new.shape
    e = slot_t.shape[2]
    tile = lambda i, j: (i, j, 0)
    return pl.pallas_call(
        functools.partial(_combine_kernel, cap=cap),
        grid=(b, n // tn),
        in_specs=[pl.BlockSpec((1, tn, e), tile),
                  pl.BlockSpec((1, e * cap, d), lambda i, j: (i, 0, 0)),
                  pl.BlockSpec((1, tn, d), tile),
                  pl.BlockSpec((1, N_MOD, d), lambda i, j: (i, 0, 0)),
                  pl.BlockSpec((1, d), lambda i, j: (0, 0))],
        out_specs=pl.BlockSpec((1, tn, d), tile),
        out_shape=jax.ShapeDtypeStruct((b, n, d), F32),
        compiler_params=_params("parallel", "arbitrary"),
        name="moe_combine_norm",
    )(slot_t, y2, x_new, mod, norm_final)


def kernel(x, c, ctx, c_ctx, w_mod, b_mod, norm1, w_in, rpb, w_s, b_s, gmlp_norm, out_norm_a, out_norm_b,
           w_out, norm2, w_router, w_gate, w_up, w_down, norm_final):
    b, n, d = x.shape
    assert w_mod.shape[0] == 1, "single-layer stack only"
    assert n % (GRID_W * Q_ROWS) == 0 and n // GRID_W >= W_ROWS
    cap = EC_CAPACITY_FACTOR * n // N_EXPERTS

    pad = (-(b + 1)) % 8
    cc = jnp.concatenate([c, c_ctx[None], jnp.zeros((pad, d), F32)], axis=0)
    m = _modulation(cc, w_mod[0], b_mod[0][None])
    mod = m[:b].reshape(b, N_MOD, d)
    mod_c = m[b].reshape(N_MOD, d)

    w_in_b = w_in[0].astype(BF16)
    ws2 = w_s[0].astype(BF16).reshape(N_GROUPS_SG // 2, 2 * CHUNK, CHUNK)
    bs2 = jnp.broadcast_to(b_s[0].reshape(N_GROUPS_SG // 2, 2 * CHUNK, 1), (N_GROUPS_SG // 2, 2 * CHUNK, LANES))
    q, k, v, ob = _in_proj(x, mod, norm1, w_in_b, ws2, bs2, gmlp_norm, out_norm_b)

    kc, vc = _ctx_proj(ctx.reshape(b * CTX_LEN, d), mod_c, norm1, w_in_b[:, D_NA:3 * D_NA])
    kc = kc.reshape(b, CTX_LEN, D_NA)
    vc = vc.reshape(b, CTX_LEN, D_NA)

    bias = _bias_tables(rpb[0], n // GRID_W)
    oa = _attention(q, k, v, kc, vc, bias, out_norm_a)

    x_new, h2, aff = _out_proj(oa, ob, x, mod, w_out[0].astype(BF16), norm2, w_router[0].T)

    slot = _topk_slots(aff.reshape(b * N_EXPERTS, n), cap).reshape(b, N_EXPERTS, n)
    xs, gate = _gather_tokens(h2, slot, aff, cap)
    y = _experts(xs, w_gate[0], w_up[0], w_down[0], gate)
    return _combine(jnp.swapaxes(slot, 1, 2), y.reshape(b, N_EXPERTS * cap, d), x_new, mod, norm_final[None], cap)
```

```python
import functools

import numpy as np
import jax
import jax.numpy as jnp
from jax import lax
from jax.experimental import pallas as pl
from jax.experimental.pallas import tpu as pltpu

D_MODEL = 1024
GRID_W = 64
CTX_LEN = 256
N_HEADS_NA = 8
HEAD_DIM = 64
D_NA = N_HEADS_NA * HEAD_DIM
NA_KH = 8
NA_KW = 16
D_SG = D_MODEL - D_NA
N_GROUPS_SG = 8
SG_GROUP_DIM = D_SG // N_GROUPS_SG
CHUNK = 128
N_EXPERTS = 16
EC_CAPACITY_FACTOR = 2
D_EXPERT = 1536
N_MOD = 6
EPS = 1e-6
NEG_INF = -1e30

LANES = 128
VMEM_LIMIT = 56 * 1024 * 1024

F32 = jnp.float32
BF16 = jnp.bfloat16
HIGHEST = lax.Precision.HIGHEST

Q_ROWS = 4
W_ROWS = 12
TQ = Q_ROWS * GRID_W
TW = W_ROWS * GRID_W

MANTISSA_STEPS = 36


def _params(*sem):
    return pltpu.CompilerParams(dimension_semantics=sem, vmem_limit_bytes=VMEM_LIMIT)


def _rms_mod(x, g, shift, scale):
    r = lax.rsqrt(jnp.mean(x * x, axis=-1, keepdims=True) + EPS)
    return (x * r) * g * (1.0 + scale) + shift


def _rms(x, g):
    return x * lax.rsqrt(jnp.mean(x * x, axis=-1, keepdims=True) + EPS) * g


def _gelu_tanh(x):
    return 0.5 * x * (1.0 + jnp.tanh(np.sqrt(2.0 / np.pi).astype(np.float32) * (x + 0.044715 * (x * x * x))))


def _silu(x):
    return x * jax.nn.sigmoid(x)


def _mod_kernel(c_ref, w_ref, b_ref, o_ref):
    s = _silu(c_ref[...])
    o_ref[...] = jnp.dot(s, w_ref[...], precision=HIGHEST, preferred_element_type=F32) + b_ref[...]


def _modulation(cc, w_mod, b_mod):
    rows, d = cc.shape
    n = w_mod.shape[1]
    tn = 1024
    return pl.pallas_call(
        _mod_kernel,
        grid=(n // tn,),
        in_specs=[pl.BlockSpec((rows, d), lambda j: (0, 0)),
                  pl.BlockSpec((d, tn), lambda j: (0, j)),
                  pl.BlockSpec((1, tn), lambda j: (0, j))],
        out_specs=pl.BlockSpec((rows, tn), lambda j: (0, j)),
        out_shape=jax.ShapeDtypeStruct((rows, n), F32),
        compiler_params=_params("arbitrary"),
        name="modulation",
    )(cc, w_mod, b_mod)


def _in_kernel(x_ref, mod_ref, n1_ref, w_ref, ws_ref, bs_ref, gn_ref, onb_ref,
               q_ref, k_ref, v_ref, ob_ref, sp_ref):
    x = x_ref[0]
    mod = mod_ref[0]
    hb = _rms_mod(x, n1_ref[...], mod[0:1], mod[1:2]).astype(BF16)
    qkv = jnp.dot(hb, w_ref[:, :3 * D_NA], preferred_element_type=F32)
    q_ref[0] = (qkv[:, :D_NA] * (HEAD_DIM ** -0.5)).astype(BF16)
    k_ref[0] = qkv[:, D_NA:2 * D_NA].astype(BF16)
    v_ref[0] = qkv[:, 2 * D_NA:].astype(BF16)

    uz = jnp.dot(hb, w_ref[:, 3 * D_NA:], preferred_element_type=F32)
    u = _gelu_tanh(uz[:, :D_SG])
    z = _gelu_tanh(uz[:, D_SG:])
    mu = jnp.mean(z, axis=-1, keepdims=True)
    zc = z - mu
    var = jnp.mean(zc * zc, axis=-1, keepdims=True)
    zb = (zc * lax.rsqrt(var + EPS) * gn_ref[...]).astype(BF16)

    tm = x.shape[0]
    nch = tm // CHUNK
    lane = lax.broadcasted_iota(jnp.int32, (CHUNK, LANES), 1)
    first = lane < SG_GROUP_DIM
    for p in range(N_GROUPS_SG // 2):
        zp = jnp.concatenate(
            [zb[c * CHUNK:(c + 1) * CHUNK, p * LANES:(p + 1) * LANES] for c in range(nch)], axis=1)
        r = jnp.dot(ws_ref[p], zp, preferred_element_type=F32)
        bs = bs_ref[p]
        for c in range(nch):
            top = r[:CHUNK, c * LANES:(c + 1) * LANES] + bs[:CHUNK]
            bot = r[CHUNK:, c * LANES:(c + 1) * LANES] + bs[CHUNK:]
            sp_ref[c * CHUNK:(c + 1) * CHUNK, p * LANES:(p + 1) * LANES] = jnp.where(first, top, bot)
    ob = u * sp_ref[...]
    ob_ref[0] = _rms(ob, onb_ref[...]).astype(BF16)


def _in_proj(x, mod, norm1, w_in_b, ws2, bs2, gmlp_norm, out_norm_b, tm=512):
    b, n, d = x.shape
    d_in = w_in_b.shape[1]
    full2 = lambda i, j: (0, 0)
    full3 = lambda i, j: (0, 0, 0)
    tile = lambda i, j: (i, j, 0)
    act = jax.ShapeDtypeStruct((b, n, D_NA), BF16)
    return pl.pallas_call(
        _in_kernel,
        grid=(b, n // tm),
        in_specs=[pl.BlockSpec((1, tm, d), tile),
                  pl.BlockSpec((1, N_MOD, d), lambda i, j: (i, 0, 0)),
                  pl.BlockSpec((1, d), full2),
                  pl.BlockSpec((d, d_in), full2),
                  pl.BlockSpec(ws2.shape, full3),
                  pl.BlockSpec(bs2.shape, full3),
                  pl.BlockSpec((1, D_SG), full2),
                  pl.BlockSpec((1, D_SG), full2)],
        out_specs=[pl.BlockSpec((1, tm, D_NA), tile)] * 4,
        out_shape=[act] * 4,
        scratch_shapes=[pltpu.VMEM((tm, D_SG), F32)],
        compiler_params=_params("parallel", "arbitrary"),
        name="in_proj_gmlp",
    )(x, mod, norm1, w_in_b, ws2, bs2, gmlp_norm, out_norm_b)


def _ctx_kernel(x_ref, mod_ref, n1_ref, w_ref, k_ref, v_ref):
    mod = mod_ref[...]
    hb = _rms_mod(x_ref[...], n1_ref[...], mod[0:1], mod[1:2]).astype(BF16)
    kv = jnp.dot(hb, w_ref[...], preferred_element_type=F32)
    k_ref[...] = kv[:, :D_NA].astype(BF16)
    v_ref[...] = kv[:, D_NA:].astype(BF16)


def _ctx_proj(ctx2, mod_c, norm1, w_kv_b, tm=512):
    rows, d = ctx2.shape
    full = lambda i: (0, 0)
    act = jax.ShapeDtypeStruct((rows, D_NA), BF16)
    return pl.pallas_call(
        _ctx_kernel,
        grid=(rows // tm,),
        in_specs=[pl.BlockSpec((tm, d), lambda i: (i, 0)),
                  pl.BlockSpec((N_MOD, d), full),
                  pl.BlockSpec((1, d), full),
                  pl.BlockSpec(w_kv_b.shape, full)],
        out_specs=[pl.BlockSpec((tm, D_NA), lambda i: (i, 0))] * 2,
        out_shape=[act, act],
        compiler_params=_params("arbitrary"),
        name="ctx_kv_proj",
    )(ctx2, mod_c, norm1, w_kv_b)


def _window_start(t, rows):
    return jnp.clip(Q_ROWS * t - NA_KH // 2, 0, rows - W_ROWS)


def _attn_kernel(q_ref, k_ref, v_ref, kc_ref, vc_ref, bias_ref, ona_ref, o_ref, *, rows):
    t = pl.program_id(1)
    ws = pl.multiple_of(_window_start(t, rows) * GRID_W, GRID_W)
    kw = k_ref[0, pl.ds(ws, TW), :]
    vw = v_ref[0, pl.ds(ws, TW), :]
    q = q_ref[0]
    kc = kc_ref[0]
    vc = vc_ref[0]
    lane = lax.broadcasted_iota(jnp.int32, (TQ, LANES), 1)
    first = lane < HEAD_DIM
    nt = (((1,), (1,)), ((), ()))
    outs = []
    for p in range(N_HEADS_NA // 2):
        sl = slice(p * LANES, (p + 1) * LANES)
        qp, kp, vp, kcp, vcp = q[:, sl], kw[:, sl], vw[:, sl], kc[:, sl], vc[:, sl]
        halves = []
        for j in range(2):
            qm = jnp.where(first if j == 0 else jnp.logical_not(first), qp, jnp.zeros_like(qp))
            s_nb = lax.dot_general(qm, kp, nt, preferred_element_type=F32) + bias_ref[0, 2 * p + j]
            s_cx = lax.dot_general(qm, kcp, nt, preferred_element_type=F32)
            m = jnp.maximum(jnp.max(s_nb, axis=1, keepdims=True), jnp.max(s_cx, axis=1, keepdims=True))
            e_nb = jnp.exp(s_nb - m)
            e_cx = jnp.exp(s_cx - m)
            l = jnp.sum(e_nb, axis=1, keepdims=True) + jnp.sum(e_cx, axis=1, keepdims=True)
            o = (jnp.dot(e_nb.astype(BF16), vp, preferred_element_type=F32)
                 + jnp.dot(e_cx.astype(BF16), vcp, preferred_element_type=F32))
            halves.append(o / l)
        outs.append(jnp.where(first, halves[0], halves[1]))
    o = jnp.concatenate(outs, axis=1)
    o_ref[0] = _rms(o, ona_ref[...]).astype(BF16)


def _bias_tables(rpb, rows):
    n_tiles = rows // Q_ROWS
    n_ro, n_co = 2 * NA_KH - 1, 2 * NA_KW - 1
    qc = np.arange(GRID_W)[:, None]
    kc = np.arange(GRID_W)[None, :]
    cs = np.clip(qc - NA_KW // 2, 0, GRID_W - NA_KW)
    col_ok = (kc >= cs) & (kc < cs + NA_KW)
    spread = ((kc - qc + NA_KW - 1)[None] == np.arange(n_co)[:, None, None]) & col_ok[None]
    blocks = jnp.dot(rpb.reshape(-1, n_co), jnp.asarray(spread.reshape(n_co, -1), F32), precision=HIGHEST)
    blocks = jnp.where(jnp.asarray(col_ok.reshape(-1)), blocks, NEG_INF)
    blocks = blocks.reshape(N_HEADS_NA, n_ro, GRID_W, GRID_W)
    masked = jnp.full((N_HEADS_NA, GRID_W, GRID_W), NEG_INF, F32)
    classes = []
    for t in (0, 1, n_tiles - 1):
        ws = int(np.clip(Q_ROWS * t - NA_KH // 2, 0, rows - W_ROWS))
        tile_rows = []
        for rho in range(Q_ROWS):
            r = Q_ROWS * t + rho
            rs = int(np.clip(r - NA_KH // 2, 0, rows - NA_KH))
            parts = [blocks[:, kr - r + NA_KH - 1] if rs <= kr < rs + NA_KH else masked
                     for kr in range(ws, ws + W_ROWS)]
            tile_rows.append(jnp.concatenate(parts, axis=-1))
        classes.append(jnp.concatenate(tile_rows, axis=1))
    return jnp.stack(classes)


def _attention(q, k, v, kc, vc, bias, out_norm_a):
    b, n, _ = q.shape
    rows = n // GRID_W
    n_tiles = rows // Q_ROWS

    def bias_map(i, t):
        return (jnp.where(t == 0, 0, jnp.where(t == n_tiles - 1, 2, 1)), 0, 0, 0)

    per_b = lambda i, t: (i, 0, 0)
    return pl.pallas_call(
        functools.partial(_attn_kernel, rows=rows),
        grid=(b, n_tiles),
        in_specs=[pl.BlockSpec((1, TQ, D_NA), lambda i, t: (i, t, 0)),
                  pl.BlockSpec((1, n, D_NA), per_b),
                  pl.BlockSpec((1, n, D_NA), per_b),
                  pl.BlockSpec((1, CTX_LEN, D_NA), per_b),
                  pl.BlockSpec((1, CTX_LEN, D_NA), per_b),
                  pl.BlockSpec((1, N_HEADS_NA, TQ, TW), bias_map),
                  pl.BlockSpec((1, D_NA), lambda i, t: (0, 0))],
        out_specs=pl.BlockSpec((1, TQ, D_NA), lambda i, t: (i, t, 0)),
        out_shape=jax.ShapeDtypeStruct((b, n, D_NA), BF16),
        compiler_params=_params("parallel", "arbitrary"),
        name="nbr_attention",
    )(q, k, v, kc, vc, bias, out_norm_a)


def _out_kernel(oa_ref, ob_ref, x_ref, mod_ref, w_ref, n2_ref, wr_ref, xn_ref, h2_ref, aff_ref):
    mix = (jnp.dot(oa_ref[0], w_ref[:D_NA], preferred_element_type=F32)
           + jnp.dot(ob_ref[0], w_ref[D_NA:], preferred_element_type=F32))
    mod = mod_ref[0]
    xn = x_ref[0] + mod[2:3] * mix
    xn_ref[0] = xn
    h2 = _rms_mod(xn, n2_ref[...], mod[3:4], mod[4:5])
    h_hi = h2.astype(BF16)
    h2_ref[0] = h_hi
    h_lo = (h2 - h_hi.astype(F32)).astype(BF16)
    nt = (((1,), (1,)), ((), ()))
    l_hi = lax.dot_general(wr_ref[...], h_hi, nt, preferred_element_type=F32)
    l_lo = lax.dot_general(wr_ref[:N_EXPERTS], h_lo, nt, preferred_element_type=F32)
    logits = l_hi[:N_EXPERTS] + l_hi[N_EXPERTS:] + l_lo
    e = jnp.exp(logits - jnp.max(logits, axis=0, keepdims=True))
    aff_ref[0] = e / jnp.sum(e, axis=0, keepdims=True)


def _out_proj(oa, ob, x, mod, w_out_b, norm2, w_router_t, tm=512):
    b, n, d = x.shape
    tile = lambda i, j: (i, j, 0)
    full = lambda i, j: (0, 0)
    return pl.pallas_call(
        _out_kernel,
        grid=(b, n // tm),
        in_specs=[pl.BlockSpec((1, tm, D_NA), tile),
                  pl.BlockSpec((1, tm, D_SG), tile),
                  pl.BlockSpec((1, tm, d), tile),
                  pl.BlockSpec((1, N_MOD, d), lambda i, j: (i, 0, 0)),
                  pl.BlockSpec(w_out_b.shape, full),
                  pl.BlockSpec((1, d), full),
                  pl.BlockSpec((2 * N_EXPERTS, d), full)],
        out_specs=[pl.BlockSpec((1, tm, d), tile),
                   pl.BlockSpec((1, tm, d), tile),
                   pl.BlockSpec((1, N_EXPERTS, tm), lambda i, j: (i, 0, j))],
        out_shape=[jax.ShapeDtypeStruct((b, n, d), F32),
                   jax.ShapeDtypeStruct((b, n, d), BF16),
                   jax.ShapeDtypeStruct((b, N_EXPERTS, n), F32)],
        compiler_params=_params("parallel", "arbitrary"),
        name="out_proj_router",
    )(oa, ob, x, mod, w_out_b, norm2, w_router_t)


def _prefix_count(mask_f, tri):
    rows, n = mask_f.shape
    parts = []
    carry = jnp.zeros((rows, 1), F32)
    for j in range(n // LANES):
        blk = mask_f[:, j * LANES:(j + 1) * LANES]
        parts.append(jnp.dot(blk.astype(BF16), tri, preferred_element_type=F32) + carry)
        carry = carry + jnp.sum(blk, axis=1, keepdims=True)
    return jnp.concatenate(parts, axis=1)


def _topk_kernel(aff_ref, slot_ref, *, cap):
    a = aff_ref[...]
    rows = a.shape[0]

    def enough(t):
        return jnp.sum(jnp.where(a >= t, 1.0, 0.0), axis=1, keepdims=True) >= cap

    tiny = jnp.full((rows, 1), 2.0 ** -126, F32)
    normal = enough(tiny)
    pw = tiny
    hi = jnp.full((rows, 1), 4.0, F32)
    for bit in range(6, -1, -1):
        cand = pw * (2.0 ** (1 << bit))
        ok = enough(cand)
        pw = jnp.where(ok, cand, pw)
        hi = jnp.where(ok, hi, cand)
    lo = jnp.where(normal, pw, 0.0)
    hi = jnp.where(normal, hi, tiny)
    step = lo
    for _ in range(MANTISSA_STEPS):
        step = step * 0.5
        cand = lo + step
        ok = enough(cand)
        lo = jnp.where(ok, cand, lo)
        hi = jnp.where(ok, hi, cand)
    above = a >= hi
    tie = jnp.logical_and(a >= lo, jnp.logical_not(above))
    n_above = jnp.sum(jnp.where(above, 1.0, 0.0), axis=1, keepdims=True)
    ri = lax.broadcasted_iota(jnp.int32, (LANES, LANES), 0)
    ci = lax.broadcasted_iota(jnp.int32, (LANES, LANES), 1)
    tri = jnp.where(ri <= ci, 1.0, 0.0).astype(BF16)
    tie_rank = _prefix_count(jnp.where(tie, 1.0, 0.0), tri)
    sel = jnp.logical_or(above, jnp.logical_and(tie, tie_rank <= cap - n_above))
    pos = _prefix_count(jnp.where(sel, 1.0, 0.0), tri) - 1.0
    slot_ref[...] = jnp.where(sel, pos, -1.0).astype(jnp.int32)


def _topk_slots(aff2, cap):
    rows, n = aff2.shape
    tr = 32
    return pl.pallas_call(
        functools.partial(_topk_kernel, cap=cap),
        grid=(rows // tr,),
        in_specs=[pl.BlockSpec((tr, n), lambda i: (i, 0))],
        out_specs=pl.BlockSpec((tr, n), lambda i: (i, 0)),
        out_shape=jax.ShapeDtypeStruct((rows, n), jnp.int32),
        compiler_params=_params("parallel"),
        name="expert_topk",
    )(aff2)


def _gather_kernel(h2_ref, slot_ref, aff_ref, xs_ref, gate_ref, *, cap, eg):
    h2 = h2_ref[0]
    n = h2.shape[0]
    ci = lax.broadcasted_iota(jnp.int32, (cap, n), 0)
    for j in range(eg):
        mask = slot_ref[0, 0, j:j + 1, :] == ci
        onehot = jnp.where(mask, 1.0, 0.0).astype(BF16)
        xs_ref[0, j] = jnp.dot(onehot, h2, preferred_element_type=F32).astype(BF16)
        gate_ref[0, j] = jnp.sum(jnp.where(mask, aff_ref[0, 0, j:j + 1, :], 0.0), axis=1, keepdims=True)


def _gather_tokens(h2, slot, aff, cap, eg=4):
    b, n, d = h2.shape
    e = slot.shape[1]
    slot4 = slot.reshape(b, e // eg, eg, n)
    aff4 = aff.reshape(b, e // eg, eg, n)
    meta = pl.BlockSpec((1, 1, eg, n), lambda i, j: (i, j, 0, 0))
    return pl.pallas_call(
        functools.partial(_gather_kernel, cap=cap, eg=eg),
        grid=(b, e // eg),
        in_specs=[pl.BlockSpec((1, n, d), lambda i, j: (i, 0, 0)), meta, meta],
        out_specs=[pl.BlockSpec((1, eg, cap, d), lambda i, j: (i, j, 0, 0)),
                   pl.BlockSpec((1, eg, cap, 1), lambda i, j: (i, j, 0, 0))],
        out_shape=[jax.ShapeDtypeStruct((b, e, cap, d), BF16),
                   jax.ShapeDtypeStruct((b, e, cap, 1), F32)],
        compiler_params=_params("parallel", "arbitrary"),
        name="moe_gather",
    )(h2, slot4, aff4)


def _expert_kernel(xs_ref, wg_ref, wu_ref, wd_ref, gate_ref, y_ref, acc_ref, *, rb):
    f = pl.program_id(1)

    @pl.when(f == 0)
    def _():
        acc_ref[...] = jnp.zeros_like(acc_ref)

    wg = wg_ref[0].astype(BF16)
    wu = wu_ref[0].astype(BF16)
    wd = wd_ref[0].astype(BF16)
    nb, _, cap, d = xs_ref.shape
    for i in range(nb // rb):
        xs = xs_ref[i * rb:(i + 1) * rb, 0].reshape(rb * cap, d)
        a = jnp.dot(xs, wg, preferred_element_type=F32)
        u = jnp.dot(xs, wu, preferred_element_type=F32)
        hm = (_silu(a) * u).astype(BF16)
        part = jnp.dot(hm, wd, preferred_element_type=F32).reshape(rb, cap, d)
        acc_ref[i * rb:(i + 1) * rb] += part

    @pl.when(f == pl.num_programs(1) - 1)
    def _():
        y_ref[:, 0] = (acc_ref[...] * gate_ref[:, 0]).astype(BF16)


def _experts(xs, w_gate, w_up, w_down, gate, fc=512, rb=4):
    b, e, cap, d = xs.shape
    dff = w_gate.shape[2]
    per_e = lambda i, f: (0, i, 0, 0)
    return pl.pallas_call(
        functools.partial(_expert_kernel, rb=rb),
        grid=(e, dff // fc),
        in_specs=[pl.BlockSpec((b, 1, cap, d), per_e),
                  pl.BlockSpec((1, d, fc), lambda i, f: (i, 0, f)),
                  pl.BlockSpec((1, d, fc), lambda i, f: (i, 0, f)),
                  pl.BlockSpec((1, fc, d), lambda i, f: (i, f, 0)),
                  pl.BlockSpec((b, 1, cap, 1), per_e)],
        out_specs=pl.BlockSpec((b, 1, cap, d), per_e),
        out_shape=jax.ShapeDtypeStruct((b, e, cap, d), BF16),
        scratch_shapes=[pltpu.VMEM((b, cap, d), F32)],
        compiler_params=_params("parallel", "arbitrary"),
        name="moe_experts",
    )(xs, w_gate, w_up, w_down, gate)


def _combine_kernel(slot_t_ref, y_ref, xn_ref, mod_ref, nf_ref, o_ref, *, cap):
    st = slot_t_ref[0]
    tn, e = st.shape
    ci = lax.broadcasted_iota(jnp.int32, (tn, cap), 1)
    scat = jnp.concatenate(
        [jnp.where(st[:, j:j + 1] == ci, 1.0, 0.0).astype(BF16) for j in range(e)], axis=1)
    moe = jnp.dot(scat, y_ref[0], preferred_element_type=F32)
    x = xn_ref[0] + mod_ref[0][5:6] * moe
    o_ref[0] = _rms(x, nf_ref[...])


def _combine(slot_t, y2, x_new, mod, norm_final, cap, tn=512):
    b, n, d = x_new.shape
    e = slot_t.shape[2]
    tile = lambda i, j: (i, j, 0)
    return pl.pallas_call(
        functools.partial(_combine_kernel, cap=cap),
        grid=(b, n // tn),
        in_specs=[pl.BlockSpec((1, tn, e), tile),
                  pl.BlockSpec((1, e * cap, d), lambda i, j: (i, 0, 0)),
                  pl.BlockSpec((1, tn, d), tile),
                  pl.BlockSpec((1, N_MOD, d), lambda i, j: (i, 0, 0)),
                  pl.BlockSpec((1, d), lambda i, j: (0, 0))],
        out_specs=pl.BlockSpec((1, tn, d), tile),
        out_shape=jax.ShapeDtypeStruct((b, n, d), F32),
        compiler_params=_params("parallel", "arbitrary"),
        name="moe_combine_norm",
    )(slot_t, y2, x_new, mod, norm_final)


def kernel(x, c, ctx, c_ctx, w_mod, b_mod, norm1, w_in, rpb, w_s, b_s, gmlp_norm, out_norm_a, out_norm_b,
           w_out, norm2, w_router, w_gate, w_up, w_down, norm_final):
    b, n, d = x.shape
    assert w_mod.shape[0] == 1, "single-layer stack only"
    assert n % (GRID_W * Q_ROWS) == 0 and n // GRID_W >= W_ROWS
    cap = EC_CAPACITY_FACTOR * n // N_EXPERTS

    pad = (-(b + 1)) % 8
    cc = jnp.concatenate([c, c_ctx[None], jnp.zeros((pad, d), F32)], axis=0)
    m = _modulation(cc, w_mod[0], b_mod[0][None])
    mod = m[:b].reshape(b, N_MOD, d)
    mod_c = m[b].reshape(N_MOD, d)

    w_in_b = w_in[0].astype(BF16)
    ws2 = w_s[0].astype(BF16).reshape(N_GROUPS_SG // 2, 2 * CHUNK, CHUNK)
    bs2 = jnp.broadcast_to(b_s[0].reshape(N_GROUPS_SG // 2, 2 * CHUNK, 1), (N_GROUPS_SG // 2, 2 * CHUNK, LANES))
    q, k, v, ob = _in_proj(x, mod, norm1, w_in_b, ws2, bs2, gmlp_norm, out_norm_b)

    kc, vc = _ctx_proj(ctx.reshape(b * CTX_LEN, d), mod_c, norm1, w_in_b[:, D_NA:3 * D_NA])
    kc = kc.reshape(b, CTX_LEN, D_NA)
    vc = vc.reshape(b, CTX_LEN, D_NA)

    bias = _bias_tables(rpb[0], n // GRID_W)
    oa = _attention(q, k, v, kc, vc, bias, out_norm_a)

    wr_t = w_router[0].T
    wr_hi = wr_t.astype(BF16)
    wr_split = jnp.concatenate([wr_hi, (wr_t - wr_hi.astype(F32)).astype(BF16)], axis=0)
    x_new, h2, aff = _out_proj(oa, ob, x, mod, w_out[0].astype(BF16), norm2, wr_split)

    slot = _topk_slots(aff.reshape(b * N_EXPERTS, n), cap).reshape(b, N_EXPERTS, n)
    xs, gate = _gather_tokens(h2, slot, aff, cap)
    y = _experts(xs, w_gate[0], w_up[0], w_down[0], gate)
    return _combine(jnp.swapaxes(slot, 1, 2), y.reshape(b, N_EXPERTS * cap, d), x_new, mod, norm_final[None], cap)
```

```python
import functools

import numpy as np
import jax
import jax.numpy as jnp
from jax import lax
from jax.experimental import pallas as pl
from jax.experimental.pallas import tpu as pltpu

D_MODEL = 1024
GRID_W = 64
CTX_LEN = 256
N_HEADS_NA = 8
HEAD_DIM = 64
D_NA = N_HEADS_NA * HEAD_DIM
NA_KH = 8
NA_KW = 16
D_SG = D_MODEL - D_NA
N_GROUPS_SG = 8
SG_GROUP_DIM = D_SG // N_GROUPS_SG
CHUNK = 128
N_EXPERTS = 16
EC_CAPACITY_FACTOR = 2
D_EXPERT = 1536
N_MOD = 6
EPS = 1e-6
NEG_INF = -1e30

LANES = 128
VMEM_LIMIT = 56 * 1024 * 1024

F32 = jnp.float32
BF16 = jnp.bfloat16
HIGHEST = lax.Precision.HIGHEST

Q_ROWS = 4
W_ROWS = 12
TQ = Q_ROWS * GRID_W
TW = W_ROWS * GRID_W

MANTISSA_STEPS = 36

TCH = 256
SLOT_W = 64
SLOT_ALIGN = 16
EXPERT_GROUP = 4
GATE_TERMS = 3


def _params(*sem):
    return pltpu.CompilerParams(dimension_semantics=sem, vmem_limit_bytes=VMEM_LIMIT)


def _rms_mod(x, g, shift, scale):
    r = lax.rsqrt(jnp.mean(x * x, axis=-1, keepdims=True) + EPS)
    return (x * r) * g * (1.0 + scale) + shift


def _rms(x, g):
    return x * lax.rsqrt(jnp.mean(x * x, axis=-1, keepdims=True) + EPS) * g


def _gelu_tanh(x):
    return 0.5 * x * (1.0 + jnp.tanh(np.sqrt(2.0 / np.pi).astype(np.float32) * (x + 0.044715 * (x * x * x))))


def _silu(x):
    return x * jax.nn.sigmoid(x)


def _bf16_terms(x, n_terms):
    terms = []
    for _ in range(n_terms):
        t = x.astype(BF16)
        terms.append(t)
        x = x - t.astype(F32)
    return terms


def _mod_kernel(c_ref, w_ref, b_ref, o_ref):
    s = _silu(c_ref[...])
    o_ref[...] = jnp.dot(s, w_ref[...], precision=HIGHEST, preferred_element_type=F32) + b_ref[...]


def _modulation(cc, w_mod, b_mod):
    rows, d = cc.shape
    n = w_mod.shape[1]
    tn = 1024
    return pl.pallas_call(
        _mod_kernel,
        grid=(n // tn,),
        in_specs=[pl.BlockSpec((rows, d), lambda j: (0, 0)),
                  pl.BlockSpec((d, tn), lambda j: (0, j)),
                  pl.BlockSpec((1, tn), lambda j: (0, j))],
        out_specs=pl.BlockSpec((rows, tn), lambda j: (0, j)),
        out_shape=jax.ShapeDtypeStruct((rows, n), F32),
        compiler_params=_params("arbitrary"),
        name="modulation",
    )(cc, w_mod, b_mod)


def _in_kernel(x_ref, mod_ref, n1_ref, w_ref, ws_ref, bs_ref, gn_ref, onb_ref,
               q_ref, k_ref, v_ref, ob_ref, sp_ref):
    x = x_ref[0]
    mod = mod_ref[0]
    hb = _rms_mod(x, n1_ref[...], mod[0:1], mod[1:2]).astype(BF16)
    qkv = jnp.dot(hb, w_ref[:, :3 * D_NA], preferred_element_type=F32)
    q_ref[0] = (qkv[:, :D_NA] * (HEAD_DIM ** -0.5)).astype(BF16)
    k_ref[0] = qkv[:, D_NA:2 * D_NA].astype(BF16)
    v_ref[0] = qkv[:, 2 * D_NA:].astype(BF16)

    uz = jnp.dot(hb, w_ref[:, 3 * D_NA:], preferred_element_type=F32)
    u = _gelu_tanh(uz[:, :D_SG])
    z = _gelu_tanh(uz[:, D_SG:])
    mu = jnp.mean(z, axis=-1, keepdims=True)
    zc = z - mu
    var = jnp.mean(zc * zc, axis=-1, keepdims=True)
    zb = (zc * lax.rsqrt(var + EPS) * gn_ref[...]).astype(BF16)

    tm = x.shape[0]
    nch = tm // CHUNK
    lane = lax.broadcasted_iota(jnp.int32, (CHUNK, LANES), 1)
    first = lane < SG_GROUP_DIM
    for p in range(N_GROUPS_SG // 2):
        zp = jnp.concatenate(
            [zb[c * CHUNK:(c + 1) * CHUNK, p * LANES:(p + 1) * LANES] for c in range(nch)], axis=1)
        r = jnp.dot(ws_ref[p], zp, preferred_element_type=F32)
        bs = bs_ref[p]
        for c in range(nch):
            top = r[:CHUNK, c * LANES:(c + 1) * LANES] + bs[:CHUNK]
            bot = r[CHUNK:, c * LANES:(c + 1) * LANES] + bs[CHUNK:]
            sp_ref[c * CHUNK:(c + 1) * CHUNK, p * LANES:(p + 1) * LANES] = jnp.where(first, top, bot)
    ob = u * sp_ref[...]
    ob_ref[0] = _rms(ob, onb_ref[...]).astype(BF16)


def _in_proj(x, mod, norm1, w_in_b, ws2, bs2, gmlp_norm, out_norm_b, tm=512):
    b, n, d = x.shape
    d_in = w_in_b.shape[1]
    full2 = lambda i, j: (0, 0)
    full3 = lambda i, j: (0, 0, 0)
    tile = lambda i, j: (i, j, 0)
    act = jax.ShapeDtypeStruct((b, n, D_NA), BF16)
    return pl.pallas_call(
        _in_kernel,
        grid=(b, n // tm),
        in_specs=[pl.BlockSpec((1, tm, d), tile),
                  pl.BlockSpec((1, N_MOD, d), lambda i, j: (i, 0, 0)),
                  pl.BlockSpec((1, d), full2),
                  pl.BlockSpec((d, d_in), full2),
                  pl.BlockSpec(ws2.shape, full3),
                  pl.BlockSpec(bs2.shape, full3),
                  pl.BlockSpec((1, D_SG), full2),
                  pl.BlockSpec((1, D_SG), full2)],
        out_specs=[pl.BlockSpec((1, tm, D_NA), tile)] * 4,
        out_shape=[act] * 4,
        scratch_shapes=[pltpu.VMEM((tm, D_SG), F32)],
        compiler_params=_params("parallel", "arbitrary"),
        name="in_proj_gmlp",
    )(x, mod, norm1, w_in_b, ws2, bs2, gmlp_norm, out_norm_b)


def _ctx_kernel(x_ref, mod_ref, n1_ref, w_ref, k_ref, v_ref):
    mod = mod_ref[...]
    hb = _rms_mod(x_ref[...], n1_ref[...], mod[0:1], mod[1:2]).astype(BF16)
    kv = jnp.dot(hb, w_ref[...], preferred_element_type=F32)
    k_ref[...] = kv[:, :D_NA].astype(BF16)
    v_ref[...] = kv[:, D_NA:].astype(BF16)


def _ctx_proj(ctx2, mod_c, norm1, w_kv_b, tm=512):
    rows, d = ctx2.shape
    full = lambda i: (0, 0)
    act = jax.ShapeDtypeStruct((rows, D_NA), BF16)
    return pl.pallas_call(
        _ctx_kernel,
        grid=(rows // tm,),
        in_specs=[pl.BlockSpec((tm, d), lambda i: (i, 0)),
                  pl.BlockSpec((N_MOD, d), full),
                  pl.BlockSpec((1, d), full),
                  pl.BlockSpec(w_kv_b.shape, full)],
        out_specs=[pl.BlockSpec((tm, D_NA), lambda i: (i, 0))] * 2,
        out_shape=[act, act],
        compiler_params=_params("arbitrary"),
        name="ctx_kv_proj",
    )(ctx2, mod_c, norm1, w_kv_b)


def _window_start(t, rows):
    return jnp.clip(Q_ROWS * t - NA_KH // 2, 0, rows - W_ROWS)


def _attn_kernel(q_ref, k_ref, v_ref, kc_ref, vc_ref, bias_ref, ona_ref, o_ref, *, rows):
    t = pl.program_id(1)
    ws = pl.multiple_of(_window_start(t, rows) * GRID_W, GRID_W)
    kw = k_ref[0, pl.ds(ws, TW), :]
    vw = v_ref[0, pl.ds(ws, TW), :]
    q = q_ref[0]
    kc = kc_ref[0]
    vc = vc_ref[0]
    lane = lax.broadcasted_iota(jnp.int32, (TQ, LANES), 1)
    first = lane < HEAD_DIM
    nt = (((1,), (1,)), ((), ()))
    outs = []
    for p in range(N_HEADS_NA // 2):
        sl = slice(p * LANES, (p + 1) * LANES)
        qp, kp, vp, kcp, vcp = q[:, sl], kw[:, sl], vw[:, sl], kc[:, sl], vc[:, sl]
        halves = []
        for j in range(2):
            qm = jnp.where(first if j == 0 else jnp.logical_not(first), qp, jnp.zeros_like(qp))
            s_nb = lax.dot_general(qm, kp, nt, preferred_element_type=F32) + bias_ref[0, 2 * p + j]
            s_cx = lax.dot_general(qm, kcp, nt, preferred_element_type=F32)
            m = jnp.maximum(jnp.max(s_nb, axis=1, keepdims=True), jnp.max(s_cx, axis=1, keepdims=True))
            e_nb = jnp.exp(s_nb - m)
            e_cx = jnp.exp(s_cx - m)
            l = jnp.sum(e_nb, axis=1, keepdims=True) + jnp.sum(e_cx, axis=1, keepdims=True)
            o = (jnp.dot(e_nb.astype(BF16), vp, preferred_element_type=F32)
                 + jnp.dot(e_cx.astype(BF16), vcp, preferred_element_type=F32))
            halves.append(o / l)
        outs.append(jnp.where(first, halves[0], halves[1]))
    o = jnp.concatenate(outs, axis=1)
    o_ref[0] = _rms(o, ona_ref[...]).astype(BF16)


def _bias_tables(rpb, rows):
    n_tiles = rows // Q_ROWS
    n_ro, n_co = 2 * NA_KH - 1, 2 * NA_KW - 1
    qc = np.arange(GRID_W)[:, None]
    kc = np.arange(GRID_W)[None, :]
    cs = np.clip(qc - NA_KW // 2, 0, GRID_W - NA_KW)
    col_ok = (kc >= cs) & (kc < cs + NA_KW)
    spread = ((kc - qc + NA_KW - 1)[None] == np.arange(n_co)[:, None, None]) & col_ok[None]
    blocks = jnp.dot(rpb.reshape(-1, n_co), jnp.asarray(spread.reshape(n_co, -1), F32), precision=HIGHEST)
    blocks = jnp.where(jnp.asarray(col_ok.reshape(-1)), blocks, NEG_INF)
    blocks = blocks.reshape(N_HEADS_NA, n_ro, GRID_W, GRID_W)
    masked = jnp.full((N_HEADS_NA, GRID_W, GRID_W), NEG_INF, F32)
    classes = []
    for t in (0, 1, n_tiles - 1):
        ws = int(np.clip(Q_ROWS * t - NA_KH // 2, 0, rows - W_ROWS))
        tile_rows = []
        for rho in range(Q_ROWS):
            r = Q_ROWS * t + rho
            rs = int(np.clip(r - NA_KH // 2, 0, rows - NA_KH))
            parts = [blocks[:, kr - r + NA_KH - 1] if rs <= kr < rs + NA_KH else masked
                     for kr in range(ws, ws + W_ROWS)]
            tile_rows.append(jnp.concatenate(parts, axis=-1))
        classes.append(jnp.concatenate(tile_rows, axis=1))
    return jnp.stack(classes)


def _attention(q, k, v, kc, vc, bias, out_norm_a):
    b, n, _ = q.shape
    rows = n // GRID_W
    n_tiles = rows // Q_ROWS

    def bias_map(i, t):
        return (jnp.where(t == 0, 0, jnp.where(t == n_tiles - 1, 2, 1)), 0, 0, 0)

    per_b = lambda i, t: (i, 0, 0)
    return pl.pallas_call(
        functools.partial(_attn_kernel, rows=rows),
        grid=(b, n_tiles),
        in_specs=[pl.BlockSpec((1, TQ, D_NA), lambda i, t: (i, t, 0)),
                  pl.BlockSpec((1, n, D_NA), per_b),
                  pl.BlockSpec((1, n, D_NA), per_b),
                  pl.BlockSpec((1, CTX_LEN, D_NA), per_b),
                  pl.BlockSpec((1, CTX_LEN, D_NA), per_b),
                  pl.BlockSpec((1, N_HEADS_NA, TQ, TW), bias_map),
                  pl.BlockSpec((1, D_NA), lambda i, t: (0, 0))],
        out_specs=pl.BlockSpec((1, TQ, D_NA), lambda i, t: (i, t, 0)),
        out_shape=jax.ShapeDtypeStruct((b, n, D_NA), BF16),
        compiler_params=_params("parallel", "arbitrary"),
        name="nbr_attention",
    )(q, k, v, kc, vc, bias, out_norm_a)


def _out_kernel(oa_ref, ob_ref, x_ref, mod_ref, w_ref, n2_ref, wr_ref, xn_ref, h2_ref, aff_ref, terms_ref):
    mix = (jnp.dot(oa_ref[0], w_ref[:D_NA], preferred_element_type=F32)
           + jnp.dot(ob_ref[0], w_ref[D_NA:], preferred_element_type=F32))
    mod = mod_ref[0]
    xn = x_ref[0] + mod[2:3] * mix
    xn_ref[0] = xn
    h2 = _rms_mod(xn, n2_ref[...], mod[3:4], mod[4:5])
    h_hi, h_lo = _bf16_terms(h2, 2)
    h2_ref[0] = h_hi
    w_hi, w_lo = _bf16_terms(wr_ref[...], 2)
    nt = (((1,), (1,)), ((), ()))
    l_hi = lax.dot_general(jnp.concatenate([w_hi, w_lo], axis=0), h_hi, nt, preferred_element_type=F32)
    l_lo = lax.dot_general(w_hi, h_lo, nt, preferred_element_type=F32)
    logits = l_hi[:N_EXPERTS] + l_hi[N_EXPERTS:] + l_lo
    e = jnp.exp(logits - jnp.max(logits, axis=0, keepdims=True))
    aff = e / jnp.sum(e, axis=0, keepdims=True)
    aff_ref[0] = aff
    terms_ref[0] = jnp.concatenate(_bf16_terms(aff, 3), axis=0)


def _out_proj(oa, ob, x, mod, w_out_b, norm2, w_router_t, tm=512):
    b, n, d = x.shape
    tile = lambda i, j: (i, j, 0)
    full = lambda i, j: (0, 0)
    return pl.pallas_call(
        _out_kernel,
        grid=(b, n // tm),
        in_specs=[pl.BlockSpec((1, tm, D_NA), tile),
                  pl.BlockSpec((1, tm, D_SG), tile),
                  pl.BlockSpec((1, tm, d), tile),
                  pl.BlockSpec((1, N_MOD, d), lambda i, j: (i, 0, 0)),
                  pl.BlockSpec(w_out_b.shape, full),
                  pl.BlockSpec((1, d), full),
                  pl.BlockSpec((N_EXPERTS, d), full)],
        out_specs=[pl.BlockSpec((1, tm, d), tile),
                   pl.BlockSpec((1, tm, d), tile),
                   pl.BlockSpec((1, N_EXPERTS, tm), lambda i, j: (i, 0, j)),
                   pl.BlockSpec((1, GATE_TERMS * N_EXPERTS, tm), lambda i, j: (i, 0, j))],
        out_shape=[jax.ShapeDtypeStruct((b, n, d), F32),
                   jax.ShapeDtypeStruct((b, n, d), BF16),
                   jax.ShapeDtypeStruct((b, N_EXPERTS, n), F32),
                   jax.ShapeDtypeStruct((b, GATE_TERMS * N_EXPERTS, n), BF16)],
        compiler_params=_params("parallel", "arbitrary"),
        name="out_proj_router",
    )(oa, ob, x, mod, w_out_b, norm2, w_router_t)


def _prefix_count(mask_f, tri):
    rows, n = mask_f.shape
    parts = []
    carry = jnp.zeros((rows, 1), F32)
    for j in range(n // LANES):
        blk = mask_f[:, j * LANES:(j + 1) * LANES]
        parts.append(jnp.dot(blk.astype(BF16), tri, preferred_element_type=F32) + carry)
        carry = carry + jnp.sum(blk, axis=1, keepdims=True)
    return jnp.concatenate(parts, axis=1)


def _topk_kernel(aff_ref, slot_ref, cnt_ref, *, cap):
    a = aff_ref[...]
    rows = a.shape[0]

    def enough(t):
        return jnp.sum(jnp.where(a >= t, 1.0, 0.0), axis=1, keepdims=True) >= cap

    tiny = jnp.full((rows, 1), 2.0 ** -126, F32)
    normal = enough(tiny)
    pw = tiny
    hi = jnp.full((rows, 1), 4.0, F32)
    for bit in range(6, -1, -1):
        cand = pw * (2.0 ** (1 << bit))
        ok = enough(cand)
        pw = jnp.where(ok, cand, pw)
        hi = jnp.where(ok, hi, cand)
    lo = jnp.where(normal, pw, 0.0)
    hi = jnp.where(normal, hi, tiny)
    step = lo
    for _ in range(MANTISSA_STEPS):
        step = step * 0.5
        cand = lo + step
        ok = enough(cand)
        lo = jnp.where(ok, cand, lo)
        hi = jnp.where(ok, hi, cand)
    above = a >= hi
    tie = jnp.logical_and(a >= lo, jnp.logical_not(above))
    n_above = jnp.sum(jnp.where(above, 1.0, 0.0), axis=1, keepdims=True)
    ri = lax.broadcasted_iota(jnp.int32, (LANES, LANES), 0)
    ci = lax.broadcasted_iota(jnp.int32, (LANES, LANES), 1)
    tri = jnp.where(ri <= ci, 1.0, 0.0).astype(BF16)
    tie_rank = _prefix_count(jnp.where(tie, 1.0, 0.0), tri)
    sel = jnp.logical_or(above, jnp.logical_and(tie, tie_rank <= cap - n_above))
    sel_f = jnp.where(sel, 1.0, 0.0)
    pos = _prefix_count(sel_f, tri) - 1.0
    slot_ref[...] = jnp.where(sel, pos, -1.0).astype(jnp.int32)
    lane = lax.broadcasted_iota(jnp.int32, (rows, LANES), 1)
    cnt = jnp.zeros((rows, LANES), F32)
    for j in range(a.shape[1] // TCH):
        cnt = jnp.where(lane == j, jnp.sum(sel_f[:, j * TCH:(j + 1) * TCH], axis=1, keepdims=True), cnt)
    cnt_ref[...] = cnt.astype(jnp.int32)


def _topk_slots(aff2, cap):
    rows, n = aff2.shape
    tr = 32
    return pl.pallas_call(
        functools.partial(_topk_kernel, cap=cap),
        grid=(rows // tr,),
        in_specs=[pl.BlockSpec((tr, n), lambda i: (i, 0))],
        out_specs=[pl.BlockSpec((tr, n), lambda i: (i, 0)), pl.BlockSpec((tr, LANES), lambda i: (i, 0))],
        out_shape=[jax.ShapeDtypeStruct((rows, n), jnp.int32), jax.ShapeDtypeStruct((rows, LANES), jnp.int32)],
        compiler_params=_params("parallel"),
        name="expert_topk",
    )(aff2)


def _slot_window(eb_ref, bi, e, j, n_e, n_chunks, cap):
    base = (bi * n_e + e) * (n_chunks + 1) + j
    s0 = eb_ref[base]
    s1 = eb_ref[base + 1]
    start = jnp.minimum(s0 & -SLOT_ALIGN, cap - SLOT_W)
    return s0, s1, pl.multiple_of(start, SLOT_ALIGN)


def _gate_lanes(res, e, n_e):
    lane = lax.broadcasted_iota(jnp.int32, res.shape, 1)
    head = jnp.logical_or(lane == e, lane == n_e + e)
    t12 = jnp.sum(jnp.where(head, res, 0.0), axis=1, keepdims=True)
    t3 = jnp.sum(jnp.where(lane == 2 * n_e + e, res, 0.0), axis=1, keepdims=True)
    return t12 + t3


def _gather_kernel(eb_ref, ovf_ref, h2_ref, slot_ref, ap_ref, xs_ref, gate_ref, *, cap, jc, n_chunks):
    bi = pl.program_id(0)
    jo = pl.program_id(1)
    n_e = slot_ref.shape[1]
    wi = lax.broadcasted_iota(jnp.int32, (SLOT_W, TCH), 0)
    wcol = lax.broadcasted_iota(jnp.int32, (SLOT_W, 1), 0)
    nt = (((1,), (1,)), ((), ()))

    @pl.when(jo == 0)
    def _():
        xs_ref[...] = jnp.zeros_like(xs_ref)
        gate_ref[...] = jnp.zeros_like(gate_ref)

    for jj in range(jc):
        j = jo * jc + jj
        tok = slice(jj * TCH, (jj + 1) * TCH)
        h2c = h2_ref[0, tok, :]
        apc = ap_ref[0, :, tok]
        for g in range(n_e // EXPERT_GROUP):
            blocks, meta = [], []
            for q in range(EXPERT_GROUP):
                e = EXPERT_GROUP * g + q
                s0, s1, start = _slot_window(eb_ref, bi, e, j, n_e, n_chunks, cap)
                blocks.append(jnp.where(slot_ref[0, e:e + 1, tok] - start == wi, 1.0, 0.0).astype(BF16))
                meta.append((e, s0, s1, start))
            onehot = jnp.concatenate(blocks, axis=0)
            rows = jnp.dot(onehot, h2c, preferred_element_type=F32).astype(BF16)
            aff3 = lax.dot_general(onehot, apc, nt, preferred_element_type=F32)
            for q, (e, s0, s1, start) in enumerate(meta):
                own = jnp.logical_and(wcol + start >= s0, wcol + start < s1)
                win = pl.ds(start, SLOT_W)
                blk = slice(q * SLOT_W, (q + 1) * SLOT_W)
                xs_ref[0, e, win, :] = jnp.where(own, rows[blk], xs_ref[0, e, win, :])
                gate_ref[0, e, win, :] = jnp.where(own, _gate_lanes(aff3[blk], e, n_e), gate_ref[0, e, win, :])

        @pl.when(ovf_ref[bi * n_chunks + j] != 0)
        def _():
            ci = lax.broadcasted_iota(jnp.int32, (cap, TCH), 0)
            ccol = lax.broadcasted_iota(jnp.int32, (cap, 1), 0)
            for e in range(n_e):
                s0, s1, _ = _slot_window(eb_ref, bi, e, j, n_e, n_chunks, cap)
                onehot = jnp.where(slot_ref[0, e:e + 1, tok] == ci, 1.0, 0.0).astype(BF16)
                rows = jnp.dot(onehot, h2c, preferred_element_type=F32).astype(BF16)
                aff3 = lax.dot_general(onehot, apc, nt, preferred_element_type=F32)
                own = jnp.logical_and(ccol >= s0, ccol < s1)
                xs_ref[0, e] = jnp.where(own, rows, xs_ref[0, e])
                gate_ref[0, e] = jnp.where(own, _gate_lanes(aff3, e, n_e), gate_ref[0, e])


def _gather_tokens(ebound, ovf, h2, slot, aff_terms, cap, jc=2):
    b, n, d = h2.shape
    e = slot.shape[1]
    n_chunks = n // TCH
    tok = lambda i, j, *_: (i, j, 0)
    per_b = lambda i, j, *_: (i, 0, 0, 0)
    return pl.pallas_call(
        functools.partial(_gather_kernel, cap=cap, jc=jc, n_chunks=n_chunks),
        grid_spec=pltpu.PrefetchScalarGridSpec(
            num_scalar_prefetch=2,
            grid=(b, n_chunks // jc),
            in_specs=[pl.BlockSpec((1, jc * TCH, d), tok),
                      pl.BlockSpec((1, e, jc * TCH), lambda i, j, *_: (i, 0, j)),
                      pl.BlockSpec((1, GATE_TERMS * e, jc * TCH), lambda i, j, *_: (i, 0, j))],
            out_specs=[pl.BlockSpec((1, e, cap, d), per_b),
                       pl.BlockSpec((1, e, cap, 1), per_b)]),
        out_shape=[jax.ShapeDtypeStruct((b, e, cap, d), BF16),
                   jax.ShapeDtypeStruct((b, e, cap, 1), F32)],
        compiler_params=_params("parallel", "arbitrary"),
        name="moe_gather",
    )(ebound, ovf, h2, slot, aff_terms)


def _expert_kernel(xs_ref, wg_ref, wu_ref, wd_ref, gate_ref, y_ref, acc_ref, *, rb):
    f = pl.program_id(1)

    @pl.when(f == 0)
    def _():
        acc_ref[...] = jnp.zeros_like(acc_ref)

    wg = wg_ref[0].astype(BF16)
    wu = wu_ref[0].astype(BF16)
    wd = wd_ref[0].astype(BF16)
    nb, _, cap, d = xs_ref.shape
    for i in range(nb // rb):
        xs = xs_ref[i * rb:(i + 1) * rb, 0].reshape(rb * cap, d)
        a = jnp.dot(xs, wg, preferred_element_type=F32)
        u = jnp.dot(xs, wu, preferred_element_type=F32)
        hm = (_silu(a) * u).astype(BF16)
        part = jnp.dot(hm, wd, preferred_element_type=F32).reshape(rb, cap, d)
        acc_ref[i * rb:(i + 1) * rb] += part

    @pl.when(f == pl.num_programs(1) - 1)
    def _():
        y_ref[:, 0] = (acc_ref[...] * gate_ref[:, 0]).astype(BF16)


def _experts(xs, w_gate, w_up, w_down, gate, fc=512, rb=4):
    b, e, cap, d = xs.shape
    dff = w_gate.shape[2]
    per_e = lambda i, f: (0, i, 0, 0)
    return pl.pallas_call(
        functools.partial(_expert_kernel, rb=rb),
        grid=(e, dff // fc),
        in_specs=[pl.BlockSpec((b, 1, cap, d), per_e),
                  pl.BlockSpec((1, d, fc), lambda i, f: (i, 0, f)),
                  pl.BlockSpec((1, d, fc), lambda i, f: (i, 0, f)),
                  pl.BlockSpec((1, fc, d), lambda i, f: (i, f, 0)),
                  pl.BlockSpec((b, 1, cap, 1), per_e)],
        out_specs=pl.BlockSpec((b, 1, cap, d), per_e),
        out_shape=jax.ShapeDtypeStruct((b, e, cap, d), BF16),
        scratch_shapes=[pltpu.VMEM((b, cap, d), F32)],
        compiler_params=_params("parallel", "arbitrary"),
        name="moe_experts",
    )(xs, w_gate, w_up, w_down, gate)


def _combine_kernel(eb_ref, ovf_ref, slot_t_ref, y_ref, xn_ref, mod_ref, nf_ref, o_ref, moe_ref, *,
                    cap, jc, n_chunks):
    bi = pl.program_id(0)
    jo = pl.program_id(1)
    n_e = y_ref.shape[1]
    ci = lax.broadcasted_iota(jnp.int32, (TCH, EXPERT_GROUP * SLOT_W), 1)
    for jj in range(jc):
        j = jo * jc + jj
        tok = slice(jj * TCH, (jj + 1) * TCH)
        st = slot_t_ref[0, tok, :]
        s_blocks, y_blocks = [], []
        for g in range(n_e // EXPERT_GROUP):
            col = None
            for q in reversed(range(EXPERT_GROUP)):
                e = EXPERT_GROUP * g + q
                _, _, start = _slot_window(eb_ref, bi, e, j, n_e, n_chunks, cap)
                rel = st[:, e:e + 1] - start
                tgt = jnp.where(jnp.logical_and(rel >= 0, rel < SLOT_W), rel + q * SLOT_W, -1)
                col = tgt if col is None else jnp.where(ci < (q + 1) * SLOT_W, tgt, col)
                y_blocks.insert(g * EXPERT_GROUP, y_ref[0, e, pl.ds(start, SLOT_W), :])
            s_blocks.append(jnp.where(col == ci, 1.0, 0.0).astype(BF16))
        scat = jnp.concatenate(s_blocks, axis=1)
        ywin = jnp.concatenate(y_blocks, axis=0)
        moe_ref[tok, :] = jnp.dot(scat, ywin, preferred_element_type=F32)

        @pl.when(ovf_ref[bi * n_chunks + j] != 0)
        def _():
            cf = lax.broadcasted_iota(jnp.int32, (TCH, cap), 1)
            dense = jnp.concatenate(
                [jnp.where(st[:, e:e + 1] == cf, 1.0, 0.0).astype(BF16) for e in range(n_e)], axis=1)
            moe_ref[tok, :] = jnp.dot(dense, y_ref[0].reshape(n_e * cap, y_ref.shape[3]),
                                      preferred_element_type=F32)

    x = xn_ref[0] + mod_ref[0][5:6] * moe_ref[...]
    o_ref[0] = _rms(x, nf_ref[...])


def _combine(ebound, ovf, slot_t, y, x_new, mod, norm_final, cap, jc=2):
    b, n, d = x_new.shape
    e = slot_t.shape[2]
    n_chunks = n // TCH
    tile = lambda i, j, *_: (i, j, 0)
    return pl.pallas_call(
        functools.partial(_combine_kernel, cap=cap, jc=jc, n_chunks=n_chunks),
        grid_spec=pltpu.PrefetchScalarGridSpec(
            num_scalar_prefetch=2,
            grid=(b, n_chunks // jc),
            in_specs=[pl.BlockSpec((1, jc * TCH, e), tile),
                      pl.BlockSpec((1, e, cap, d), lambda i, j, *_: (i, 0, 0, 0)),
                      pl.BlockSpec((1, jc * TCH, d), tile),
                      pl.BlockSpec((1, N_MOD, d), lambda i, j, *_: (i, 0, 0)),
                      pl.BlockSpec((1, d), lambda i, j, *_: (0, 0))],
            out_specs=pl.BlockSpec((1, jc * TCH, d), tile),
            scratch_shapes=[pltpu.VMEM((jc * TCH, d), F32)]),
        out_shape=jax.ShapeDtypeStruct((b, n, d), F32),
        compiler_params=_params("parallel", "arbitrary"),
        name="moe_combine_norm",
    )(ebound, ovf, slot_t, y, x_new, mod, norm_final)


def kernel(x, c, ctx, c_ctx, w_mod, b_mod, norm1, w_in, rpb, w_s, b_s, gmlp_norm, out_norm_a, out_norm_b,
           w_out, norm2, w_router, w_gate, w_up, w_down, norm_final):
    b, n, d = x.shape
    assert w_mod.shape[0] == 1, "single-layer stack only"
    assert n % (GRID_W * Q_ROWS) == 0 and n // GRID_W >= W_ROWS
    assert n % (2 * TCH) == 0 and N_EXPERTS % EXPERT_GROUP == 0
    cap = EC_CAPACITY_FACTOR * n // N_EXPERTS

    pad = (-(b + 1)) % 8
    cc = jnp.concatenate([c, c_ctx[None], jnp.zeros((pad, d), F32)], axis=0)
    m = _modulation(cc, w_mod[0], b_mod[0][None])
    mod = m[:b].reshape(b, N_MOD, d)
    mod_c = m[b].reshape(N_MOD, d)

    w_in_b = w_in[0].astype(BF16)
    ws2 = w_s[0].astype(BF16).reshape(N_GROUPS_SG // 2, 2 * CHUNK, CHUNK)
    bs2 = jnp.broadcast_to(b_s[0].reshape(N_GROUPS_SG // 2, 2 * CHUNK, 1), (N_GROUPS_SG // 2, 2 * CHUNK, LANES))
    q, k, v, ob = _in_proj(x, mod, norm1, w_in_b, ws2, bs2, gmlp_norm, out_norm_b)

    kc, vc = _ctx_proj(ctx.reshape(b * CTX_LEN, d), mod_c, norm1, w_in_b[:, D_NA:3 * D_NA])
    kc = kc.reshape(b, CTX_LEN, D_NA)
    vc = vc.reshape(b, CTX_LEN, D_NA)

    bias = _bias_tables(rpb[0], n // GRID_W)
    oa = _attention(q, k, v, kc, vc, bias, out_norm_a)

    x_new, h2, aff, aff_terms = _out_proj(oa, ob, x, mod, w_out[0].astype(BF16), norm2, w_router[0].T)

    slot, cnt = _topk_slots(aff.reshape(b * N_EXPERTS, n), cap)
    slot = slot.reshape(b, N_EXPERTS, n)
    n_chunks = n // TCH
    cnt = cnt[:, :n_chunks].reshape(b, N_EXPERTS, n_chunks)
    ebound = jnp.concatenate([jnp.zeros((b, N_EXPERTS, 1), jnp.int32), jnp.cumsum(cnt, axis=-1)], axis=-1)
    start = jnp.minimum(ebound[..., :-1] & -SLOT_ALIGN, cap - SLOT_W)
    ovf = jnp.any(ebound[..., 1:] > start + SLOT_W, axis=1).astype(jnp.int32)
    ebound, ovf = ebound.reshape(-1), ovf.reshape(-1)

    xs, gate = _gather_tokens(ebound, ovf, h2, slot, aff_terms, cap)
    y = _experts(xs, w_gate[0], w_up[0], w_down[0], gate)
    return _combine(ebound, ovf, jnp.swapaxes(slot, 1, 2), y, x_new, mod, norm_final[None], cap)
```

```python
import functools

import numpy as np
import jax
import jax.numpy as jnp
from jax import lax
from jax.experimental import pallas as pl
from jax.experimental.pallas import tpu as pltpu

D_MODEL = 1024
GRID_W = 64
CTX_LEN = 256
N_HEADS_NA = 8
HEAD_DIM = 64
D_NA = N_HEADS_NA * HEAD_DIM
NA_KH = 8
NA_KW = 16
D_SG = D_MODEL - D_NA
N_GROUPS_SG = 8
SG_GROUP_DIM = D_SG // N_GROUPS_SG
CHUNK = 128
N_EXPERTS = 16
EC_CAPACITY_FACTOR = 2
D_EXPERT = 1536
N_MOD = 6
EPS = 1e-6
NEG_INF = -1e30
LOG2E = float(np.log2(np.e))

LANES = 128
VMEM_LIMIT = 56 * 1024 * 1024

F32 = jnp.float32
BF16 = jnp.bfloat16
HIGHEST = lax.Precision.HIGHEST

Q_ROWS = 4
W_ROWS = 12
TQ = Q_ROWS * GRID_W
TW = W_ROWS * GRID_W

MANTISSA_STEPS = 36

TCH = 256
SLOT_W = 64
SLOT_ALIGN = 16
EXPERT_GROUP = 4
GATE_TERMS = 3


def _params(*sem):
    return pltpu.CompilerParams(dimension_semantics=sem, vmem_limit_bytes=VMEM_LIMIT)


def _rms_mod(x, g, shift, scale):
    r = lax.rsqrt(jnp.mean(x * x, axis=-1, keepdims=True) + EPS)
    return (x * r) * g * (1.0 + scale) + shift


def _rms(x, g):
    return x * lax.rsqrt(jnp.mean(x * x, axis=-1, keepdims=True) + EPS) * g


def _gelu_tanh(x):
    return 0.5 * x * (1.0 + jnp.tanh(np.sqrt(2.0 / np.pi).astype(np.float32) * (x + 0.044715 * (x * x * x))))


def _silu(x):
    return x * jax.nn.sigmoid(x)


def _bf16_terms(x, n_terms):
    terms = []
    for _ in range(n_terms):
        t = x.astype(BF16)
        terms.append(t)
        x = x - t.astype(F32)
    return terms


def _mod_kernel(c_ref, w_ref, b_ref, o_ref):
    s = _silu(c_ref[...])
    o_ref[...] = jnp.dot(s, w_ref[...], precision=HIGHEST, preferred_element_type=F32) + b_ref[...]


def _modulation(cc, w_mod, b_mod):
    rows, d = cc.shape
    n = w_mod.shape[1]
    tn = 1024
    return pl.pallas_call(
        _mod_kernel,
        grid=(n // tn,),
        in_specs=[pl.BlockSpec((rows, d), lambda j: (0, 0)),
                  pl.BlockSpec((d, tn), lambda j: (0, j)),
                  pl.BlockSpec((1, tn), lambda j: (0, j))],
        out_specs=pl.BlockSpec((rows, tn), lambda j: (0, j)),
        out_shape=jax.ShapeDtypeStruct((rows, n), F32),
        compiler_params=_params("arbitrary"),
        name="modulation",
    )(cc, w_mod, b_mod)


def _in_kernel(x_ref, mod_ref, n1_ref, w_ref, ws_ref, bs_ref, gn_ref, onb_ref,
               q_ref, k_ref, v_ref, ob_ref, sp_ref):
    x = x_ref[0]
    mod = mod_ref[0]
    hb = _rms_mod(x, n1_ref[...], mod[0:1], mod[1:2]).astype(BF16)
    qkv = jnp.dot(hb, w_ref[:, :3 * D_NA], preferred_element_type=F32)
    q_ref[0] = (qkv[:, :D_NA] * (HEAD_DIM ** -0.5 * LOG2E)).astype(BF16)
    k_ref[0] = qkv[:, D_NA:2 * D_NA].astype(BF16)
    v_ref[0] = qkv[:, 2 * D_NA:].astype(BF16)

    uz = jnp.dot(hb, w_ref[:, 3 * D_NA:], preferred_element_type=F32)
    u = _gelu_tanh(uz[:, :D_SG])
    z = _gelu_tanh(uz[:, D_SG:])
    mu = jnp.mean(z, axis=-1, keepdims=True)
    zc = z - mu
    var = jnp.mean(zc * zc, axis=-1, keepdims=True)
    zb = (zc * lax.rsqrt(var + EPS) * gn_ref[...]).astype(BF16)

    tm = x.shape[0]
    nch = tm // CHUNK
    lane = lax.broadcasted_iota(jnp.int32, (CHUNK, LANES), 1)
    first = lane < SG_GROUP_DIM
    for p in range(N_GROUPS_SG // 2):
        zp = jnp.concatenate(
            [zb[c * CHUNK:(c + 1) * CHUNK, p * LANES:(p + 1) * LANES] for c in range(nch)], axis=1)
        r = jnp.dot(ws_ref[p], zp, preferred_element_type=F32)
        bs = bs_ref[p]
        for c in range(nch):
            top = r[:CHUNK, c * LANES:(c + 1) * LANES] + bs[:CHUNK]
            bot = r[CHUNK:, c * LANES:(c + 1) * LANES] + bs[CHUNK:]
            sp_ref[c * CHUNK:(c + 1) * CHUNK, p * LANES:(p + 1) * LANES] = jnp.where(first, top, bot)
    ob = u * sp_ref[...]
    ob_ref[0] = _rms(ob, onb_ref[...]).astype(BF16)


def _in_proj(x, mod, norm1, w_in_b, ws2, bs2, gmlp_norm, out_norm_b, tm=512):
    b, n, d = x.shape
    d_in = w_in_b.shape[1]
    full2 = lambda i, j: (0, 0)
    full3 = lambda i, j: (0, 0, 0)
    tile = lambda i, j: (i, j, 0)
    act = jax.ShapeDtypeStruct((b, n, D_NA), BF16)
    return pl.pallas_call(
        _in_kernel,
        grid=(b, n // tm),
        in_specs=[pl.BlockSpec((1, tm, d), tile),
                  pl.BlockSpec((1, N_MOD, d), lambda i, j: (i, 0, 0)),
                  pl.BlockSpec((1, d), full2),
                  pl.BlockSpec((d, d_in), full2),
                  pl.BlockSpec(ws2.shape, full3),
                  pl.BlockSpec(bs2.shape, full3),
                  pl.BlockSpec((1, D_SG), full2),
                  pl.BlockSpec((1, D_SG), full2)],
        out_specs=[pl.BlockSpec((1, tm, D_NA), tile)] * 4,
        out_shape=[act] * 4,
        scratch_shapes=[pltpu.VMEM((tm, D_SG), F32)],
        compiler_params=_params("parallel", "arbitrary"),
        name="in_proj_gmlp",
    )(x, mod, norm1, w_in_b, ws2, bs2, gmlp_norm, out_norm_b)


def _ctx_kernel(x_ref, mod_ref, n1_ref, wk_ref, wv_ref, k_ref, v_ref):
    mod = mod_ref[0]
    hb = _rms_mod(x_ref[...], n1_ref[...], mod[0:1], mod[1:2]).astype(BF16)
    k_ref[...] = jnp.dot(hb, wk_ref[...], preferred_element_type=F32).astype(BF16)
    v_ref[...] = jnp.dot(hb, wv_ref[...], preferred_element_type=F32).astype(BF16)


def _ctx_proj(ctx2, mod, mod_row, norm1, w_in_b, tm=512):
    rows, d = ctx2.shape
    act = jax.ShapeDtypeStruct((rows, D_NA), BF16)
    return pl.pallas_call(
        _ctx_kernel,
        grid=(rows // tm,),
        in_specs=[pl.BlockSpec((tm, d), lambda i: (i, 0)),
                  pl.BlockSpec((1, N_MOD, d), lambda i: (mod_row, 0, 0)),
                  pl.BlockSpec((1, d), lambda i: (0, 0)),
                  pl.BlockSpec((d, D_NA), lambda i: (0, 1)),
                  pl.BlockSpec((d, D_NA), lambda i: (0, 2))],
        out_specs=[pl.BlockSpec((tm, D_NA), lambda i: (i, 0))] * 2,
        out_shape=[act, act],
        compiler_params=_params("arbitrary"),
        name="ctx_kv_proj",
    )(ctx2, mod, norm1, w_in_b, w_in_b)


def _window_start(t, rows):
    return jnp.clip(Q_ROWS * t - NA_KH // 2, 0, rows - W_ROWS)


def _attn_kernel(q_ref, k_ref, v_ref, kc_ref, vc_ref, bias_ref, ona_ref, o_ref, *, rows):
    t = pl.program_id(1)
    ws = pl.multiple_of(_window_start(t, rows) * GRID_W, GRID_W)
    kw = k_ref[0, pl.ds(ws, TW), :]
    vw = v_ref[0, pl.ds(ws, TW), :]
    q = q_ref[0]
    kc = kc_ref[0]
    vc = vc_ref[0]
    lane = lax.broadcasted_iota(jnp.int32, (TQ, LANES), 1)
    first = lane < HEAD_DIM
    nt = (((1,), (1,)), ((), ()))
    outs = []
    for p in range(N_HEADS_NA // 2):
        sl = slice(p * LANES, (p + 1) * LANES)
        qp, kp, vp, kcp, vcp = q[:, sl], kw[:, sl], vw[:, sl], kc[:, sl], vc[:, sl]
        halves = []
        for j in range(2):
            qm = jnp.where(first if j == 0 else jnp.logical_not(first), qp, jnp.zeros_like(qp))
            s_nb = lax.dot_general(qm, kp, nt, preferred_element_type=F32) + bias_ref[0, 2 * p + j]
            s_cx = lax.dot_general(qm, kcp, nt, preferred_element_type=F32)
            m = jnp.maximum(jnp.max(s_nb, axis=1, keepdims=True), jnp.max(s_cx, axis=1, keepdims=True))
            e_nb = jnp.exp2(s_nb - m)
            e_cx = jnp.exp2(s_cx - m)
            l = jnp.sum(e_nb, axis=1, keepdims=True) + jnp.sum(e_cx, axis=1, keepdims=True)
            o = (jnp.dot(e_nb.astype(BF16), vp, preferred_element_type=F32)
                 + jnp.dot(e_cx.astype(BF16), vcp, preferred_element_type=F32))
            halves.append(o / l)
        outs.append(jnp.where(first, halves[0], halves[1]))
    o = jnp.concatenate(outs, axis=1)
    o_ref[0] = _rms(o, ona_ref[...]).astype(BF16)


def _bias_tables(rpb, rows):
    n_tiles = rows // Q_ROWS
    n_ro, n_co = 2 * NA_KH - 1, 2 * NA_KW - 1
    qc = np.arange(GRID_W)[:, None]
    kc = np.arange(GRID_W)[None, :]
    cs = np.clip(qc - NA_KW // 2, 0, GRID_W - NA_KW)
    col_ok = (kc >= cs) & (kc < cs + NA_KW)
    spread = ((kc - qc + NA_KW - 1)[None] == np.arange(n_co)[:, None, None]) & col_ok[None]
    blocks = jnp.dot(rpb.reshape(-1, n_co), jnp.asarray(spread.reshape(n_co, -1), F32), precision=HIGHEST)
    blocks = jnp.where(jnp.asarray(col_ok.reshape(-1)), blocks * LOG2E, NEG_INF)
    blocks = blocks.reshape(N_HEADS_NA, n_ro, GRID_W, GRID_W)
    masked = jnp.full((N_HEADS_NA, GRID_W, GRID_W), NEG_INF, F32)
    classes = []
    for t in (0, 1, n_tiles - 1):
        ws = int(np.clip(Q_ROWS * t - NA_KH // 2, 0, rows - W_ROWS))
        tile_rows = []
        for rho in range(Q_ROWS):
            r = Q_ROWS * t + rho
            rs = int(np.clip(r - NA_KH // 2, 0, rows - NA_KH))
            parts = [blocks[:, kr - r + NA_KH - 1] if rs <= kr < rs + NA_KH else masked
                     for kr in range(ws, ws + W_ROWS)]
            tile_rows.append(jnp.concatenate(parts, axis=-1))
        classes.append(jnp.concatenate(tile_rows, axis=1))
    return jnp.stack(classes)


def _attention(q, k, v, kc, vc, bias, out_norm_a):
    b, n, _ = q.shape
    rows = n // GRID_W
    n_tiles = rows // Q_ROWS

    def bias_map(i, t):
        return (jnp.where(t == 0, 0, jnp.where(t == n_tiles - 1, 2, 1)), 0, 0, 0)

    per_b = lambda i, t: (i, 0, 0)
    return pl.pallas_call(
        functools.partial(_attn_kernel, rows=rows),
        grid=(b, n_tiles),
        in_specs=[pl.BlockSpec((1, TQ, D_NA), lambda i, t: (i, t, 0)),
                  pl.BlockSpec((1, n, D_NA), per_b),
                  pl.BlockSpec((1, n, D_NA), per_b),
                  pl.BlockSpec((1, CTX_LEN, D_NA), per_b),
                  pl.BlockSpec((1, CTX_LEN, D_NA), per_b),
                  pl.BlockSpec((1, N_HEADS_NA, TQ, TW), bias_map),
                  pl.BlockSpec((1, D_NA), lambda i, t: (0, 0))],
        out_specs=pl.BlockSpec((1, TQ, D_NA), lambda i, t: (i, t, 0)),
        out_shape=jax.ShapeDtypeStruct((b, n, D_NA), BF16),
        compiler_params=_params("parallel", "arbitrary"),
        name="nbr_attention",
    )(q, k, v, kc, vc, bias, out_norm_a)


def _out_kernel(oa_ref, ob_ref, x_ref, mod_ref, w_ref, n2_ref, wr_ref, xn_ref, h2_ref, aff_ref, terms_ref):
    mix = (jnp.dot(oa_ref[0], w_ref[:D_NA], preferred_element_type=F32)
           + jnp.dot(ob_ref[0], w_ref[D_NA:], preferred_element_type=F32))
    mod = mod_ref[0]
    xn = x_ref[0] + mod[2:3] * mix
    xn_ref[0] = xn
    h2 = _rms_mod(xn, n2_ref[...], mod[3:4], mod[4:5])
    h_hi, h_lo = _bf16_terms(h2, 2)
    h2_ref[0] = h_hi
    w_hi, w_lo = _bf16_terms(wr_ref[...], 2)
    nt = (((1,), (1,)), ((), ()))
    l_hi = lax.dot_general(jnp.concatenate([w_hi, w_lo], axis=0), h_hi, nt, preferred_element_type=F32)
    l_lo = lax.dot_general(w_hi, h_lo, nt, preferred_element_type=F32)
    logits = l_hi[:N_EXPERTS] + l_hi[N_EXPERTS:] + l_lo
    e = jnp.exp(logits - jnp.max(logits, axis=0, keepdims=True))
    aff = e / jnp.sum(e, axis=0, keepdims=True)
    aff_ref[0] = aff
    terms_ref[0] = jnp.concatenate(_bf16_terms(aff, 3), axis=0)


def _out_proj(oa, ob, x, mod, w_out_b, norm2, w_router_t, tm=512):
    b, n, d = x.shape
    tile = lambda i, j: (i, j, 0)
    full = lambda i, j: (0, 0)
    return pl.pallas_call(
        _out_kernel,
        grid=(b, n // tm),
        in_specs=[pl.BlockSpec((1, tm, D_NA), tile),
                  pl.BlockSpec((1, tm, D_SG), tile),
                  pl.BlockSpec((1, tm, d), tile),
                  pl.BlockSpec((1, N_MOD, d), lambda i, j: (i, 0, 0)),
                  pl.BlockSpec(w_out_b.shape, full),
                  pl.BlockSpec((1, d), full),
                  pl.BlockSpec((N_EXPERTS, d), full)],
        out_specs=[pl.BlockSpec((1, tm, d), tile),
                   pl.BlockSpec((1, tm, d), tile),
                   pl.BlockSpec((1, N_EXPERTS, tm), lambda i, j: (i, 0, j)),
                   pl.BlockSpec((1, GATE_TERMS * N_EXPERTS, tm), lambda i, j: (i, 0, j))],
        out_shape=[jax.ShapeDtypeStruct((b, n, d), F32),
                   jax.ShapeDtypeStruct((b, n, d), BF16),
                   jax.ShapeDtypeStruct((b, N_EXPERTS, n), F32),
                   jax.ShapeDtypeStruct((b, GATE_TERMS * N_EXPERTS, n), BF16)],
        compiler_params=_params("parallel", "arbitrary"),
        name="out_proj_router",
    )(oa, ob, x, mod, w_out_b, norm2, w_router_t)


def _prefix_count(mask_f, tri):
    rows, n = mask_f.shape
    parts = []
    carry = jnp.zeros((rows, 1), F32)
    for j in range(n // LANES):
        blk = mask_f[:, j * LANES:(j + 1) * LANES]
        parts.append(jnp.dot(blk.astype(BF16), tri, preferred_element_type=F32) + carry)
        carry = carry + jnp.sum(blk, axis=1, keepdims=True)
    return jnp.concatenate(parts, axis=1)


def _topk_kernel(aff_ref, slot_ref, eb_ref, *, cap):
    a = aff_ref[...]
    rows = a.shape[0]

    def enough(t):
        return jnp.sum(jnp.where(a >= t, 1.0, 0.0), axis=1, keepdims=True) >= cap

    tiny = jnp.full((rows, 1), 2.0 ** -126, F32)
    normal = enough(tiny)
    pw = tiny
    hi = jnp.full((rows, 1), 4.0, F32)
    for bit in range(6, -1, -1):
        cand = pw * (2.0 ** (1 << bit))
        ok = enough(cand)
        pw = jnp.where(ok, cand, pw)
        hi = jnp.where(ok, hi, cand)
    lo = jnp.where(normal, pw, 0.0)
    hi = jnp.where(normal, hi, tiny)
    step = lo
    for _ in range(MANTISSA_STEPS):
        step = step * 0.5
        cand = lo + step
        ok = enough(cand)
        lo = jnp.where(ok, cand, lo)
        hi = jnp.where(ok, hi, cand)
    above = a >= hi
    tie = jnp.logical_and(a >= lo, jnp.logical_not(above))
    n_above = jnp.sum(jnp.where(above, 1.0, 0.0), axis=1, keepdims=True)
    ri = lax.broadcasted_iota(jnp.int32, (LANES, LANES), 0)
    ci = lax.broadcasted_iota(jnp.int32, (LANES, LANES), 1)
    tri = jnp.where(ri <= ci, 1.0, 0.0).astype(BF16)
    tie_rank = _prefix_count(jnp.where(tie, 1.0, 0.0), tri)
    sel = jnp.logical_or(above, jnp.logical_and(tie, tie_rank <= cap - n_above))
    sel_f = jnp.where(sel, 1.0, 0.0)
    pos = _prefix_count(sel_f, tri) - 1.0
    slot_ref[...] = jnp.where(sel, pos, -1.0).astype(jnp.int32)
    lane = lax.broadcasted_iota(jnp.int32, (rows, LANES), 1)
    cnt = jnp.zeros((rows, LANES), F32)
    for j in range(a.shape[1] // TCH):
        cnt = jnp.where(lane == j, jnp.sum(sel_f[:, j * TCH:(j + 1) * TCH], axis=1, keepdims=True), cnt)
    before = jnp.where(ri < ci, 1.0, 0.0).astype(BF16)
    eb_ref[...] = jnp.dot(cnt.astype(BF16), before, preferred_element_type=F32).astype(jnp.int32)


def _topk_slots(aff2, cap):
    rows, n = aff2.shape
    tr = 32
    return pl.pallas_call(
        functools.partial(_topk_kernel, cap=cap),
        grid=(rows // tr,),
        in_specs=[pl.BlockSpec((tr, n), lambda i: (i, 0))],
        out_specs=[pl.BlockSpec((tr, n), lambda i: (i, 0)), pl.BlockSpec((tr, LANES), lambda i: (i, 0))],
        out_shape=[jax.ShapeDtypeStruct((rows, n), jnp.int32), jax.ShapeDtypeStruct((rows, LANES), jnp.int32)],
        compiler_params=_params("parallel"),
        name="expert_topk",
    )(aff2)


def _slot_window(eb_ref, bi, e, j, n_e, n_chunks, cap):
    base = (bi * n_e + e) * (n_chunks + 1) + j
    s0 = eb_ref[base]
    s1 = eb_ref[base + 1]
    start = jnp.minimum(s0 & -SLOT_ALIGN, cap - SLOT_W)
    return s0, s1, pl.multiple_of(start, SLOT_ALIGN)


def _window_overflow(eb_ref, bi, j, n_e, n_chunks, cap):
    over = None
    for e in range(n_e):
        _, s1, start = _slot_window(eb_ref, bi, e, j, n_e, n_chunks, cap)
        o = s1 > start + SLOT_W
        over = o if over is None else jnp.logical_or(over, o)
    return over


def _gate_lanes(res, e, n_e):
    lane = lax.broadcasted_iota(jnp.int32, res.shape, 1)
    head = jnp.logical_or(lane == e, lane == n_e + e)
    t12 = jnp.sum(jnp.where(head, res, 0.0), axis=1, keepdims=True)
    t3 = jnp.sum(jnp.where(lane == 2 * n_e + e, res, 0.0), axis=1, keepdims=True)
    return t12 + t3


def _gather_kernel(eb_ref, h2_ref, slot_ref, ap_ref, xs_ref, gate_ref, *, cap, jc, n_chunks):
    bi = pl.program_id(0)
    jo = pl.program_id(1)
    n_e = slot_ref.shape[1]
    wi = lax.broadcasted_iota(jnp.int32, (SLOT_W, TCH), 0)
    wcol = lax.broadcasted_iota(jnp.int32, (SLOT_W, 1), 0)
    nt = (((1,), (1,)), ((), ()))

    @pl.when(jo == 0)
    def _():
        xs_ref[...] = jnp.zeros_like(xs_ref)
        gate_ref[...] = jnp.zeros_like(gate_ref)

    for jj in range(jc):
        j = jo * jc + jj
        tok = slice(jj * TCH, (jj + 1) * TCH)
        h2c = h2_ref[0, tok, :]
        apc = ap_ref[0, :, tok]
        for g in range(n_e // EXPERT_GROUP):
            blocks, meta = [], []
            for q in range(EXPERT_GROUP):
                e = EXPERT_GROUP * g + q
                s0, s1, start = _slot_window(eb_ref, bi, e, j, n_e, n_chunks, cap)
                blocks.append(jnp.where(slot_ref[0, e:e + 1, tok] - start == wi, 1.0, 0.0).astype(BF16))
                meta.append((e, s0, s1, start))
            onehot = jnp.concatenate(blocks, axis=0)
            rows = jnp.dot(onehot, h2c, preferred_element_type=F32).astype(BF16)
            aff3 = lax.dot_general(onehot, apc, nt, preferred_element_type=F32)
            for q, (e, s0, s1, start) in enumerate(meta):
                own = jnp.logical_and(wcol + start >= s0, wcol + start < s1)
                win = pl.ds(start, SLOT_W)
                blk = slice(q * SLOT_W, (q + 1) * SLOT_W)
                xs_ref[0, e, win, :] = jnp.where(own, rows[blk], xs_ref[0, e, win, :])
                gate_ref[0, e, win, :] = jnp.where(own, _gate_lanes(aff3[blk], e, n_e), gate_ref[0, e, win, :])

        @pl.when(_window_overflow(eb_ref, bi, j, n_e, n_chunks, cap))
        def _():
            ci = lax.broadcasted_iota(jnp.int32, (cap, TCH), 0)
            ccol = lax.broadcasted_iota(jnp.int32, (cap, 1), 0)
            for e in range(n_e):
                s0, s1, _ = _slot_window(eb_ref, bi, e, j, n_e, n_chunks, cap)
                onehot = jnp.where(slot_ref[0, e:e + 1, tok] == ci, 1.0, 0.0).astype(BF16)
                rows = jnp.dot(onehot, h2c, preferred_element_type=F32).astype(BF16)
                aff3 = lax.dot_general(onehot, apc, nt, preferred_element_type=F32)
                own = jnp.logical_and(ccol >= s0, ccol < s1)
                xs_ref[0, e] = jnp.where(own, rows, xs_ref[0, e])
                gate_ref[0, e] = jnp.where(own, _gate_lanes(aff3, e, n_e), gate_ref[0, e])


def _gather_tokens(ebound, h2, slot, aff_terms, cap, jc=2):
    b, n, d = h2.shape
    e = slot.shape[1]
    n_chunks = n // TCH
    tok = lambda i, j, *_: (i, j, 0)
    per_b = lambda i, j, *_: (i, 0, 0, 0)
    return pl.pallas_call(
        functools.partial(_gather_kernel, cap=cap, jc=jc, n_chunks=n_chunks),
        grid_spec=pltpu.PrefetchScalarGridSpec(
            num_scalar_prefetch=1,
            grid=(b, n_chunks // jc),
            in_specs=[pl.BlockSpec((1, jc * TCH, d), tok),
                      pl.BlockSpec((1, e, jc * TCH), lambda i, j, *_: (i, 0, j)),
                      pl.BlockSpec((1, GATE_TERMS * e, jc * TCH), lambda i, j, *_: (i, 0, j))],
            out_specs=[pl.BlockSpec((1, e, cap, d), per_b),
                       pl.BlockSpec((1, e, cap, 1), per_b)]),
        out_shape=[jax.ShapeDtypeStruct((b, e, cap, d), BF16),
                   jax.ShapeDtypeStruct((b, e, cap, 1), F32)],
        compiler_params=_params("parallel", "arbitrary"),
        name="moe_gather",
    )(ebound, h2, slot, aff_terms)


def _expert_kernel(xs_ref, wg_ref, wu_ref, wd_ref, gate_ref, y_ref, acc_ref, *, rb):
    f = pl.program_id(1)

    @pl.when(f == 0)
    def _():
        acc_ref[...] = jnp.zeros_like(acc_ref)

    wg = wg_ref[0].astype(BF16)
    wu = wu_ref[0].astype(BF16)
    wd = wd_ref[0].astype(BF16)
    nb, _, cap, d = xs_ref.shape
    for i in range(nb // rb):
        xs = xs_ref[i * rb:(i + 1) * rb, 0].reshape(rb * cap, d)
        a = jnp.dot(xs, wg, preferred_element_type=F32)
        u = jnp.dot(xs, wu, preferred_element_type=F32)
        hm = (_silu(a) * u).astype(BF16)
        part = jnp.dot(hm, wd, preferred_element_type=F32).reshape(rb, cap, d)
        acc_ref[i * rb:(i + 1) * rb] += part

    @pl.when(f == pl.num_programs(1) - 1)
    def _():
        y_ref[:, 0] = (acc_ref[...] * gate_ref[:, 0]).astype(BF16)


def _experts(xs, w_gate, w_up, w_down, gate, fc=768, rb=4):
    b, e, cap, d = xs.shape
    dff = w_gate.shape[2]
    per_e = lambda i, f: (0, i, 0, 0)
    return pl.pallas_call(
        functools.partial(_expert_kernel, rb=rb),
        grid=(e, dff // fc),
        in_specs=[pl.BlockSpec((b, 1, cap, d), per_e),
                  pl.BlockSpec((1, d, fc), lambda i, f: (i, 0, f)),
                  pl.BlockSpec((1, d, fc), lambda i, f: (i, 0, f)),
                  pl.BlockSpec((1, fc, d), lambda i, f: (i, f, 0)),
                  pl.BlockSpec((b, 1, cap, 1), per_e)],
        out_specs=pl.BlockSpec((b, 1, cap, d), per_e),
        out_shape=jax.ShapeDtypeStruct((b, e, cap, d), BF16),
        scratch_shapes=[pltpu.VMEM((b, cap, d), F32)],
        compiler_params=_params("parallel", "arbitrary"),
        name="moe_experts",
    )(xs, w_gate, w_up, w_down, gate)


def _combine_kernel(eb_ref, slot_t_ref, y_ref, xn_ref, mod_ref, nf_ref, o_ref, moe_ref, *,
                    cap, jc, n_chunks):
    bi = pl.program_id(0)
    jo = pl.program_id(1)
    n_e = y_ref.shape[1]
    ci = lax.broadcasted_iota(jnp.int32, (TCH, EXPERT_GROUP * SLOT_W), 1)
    for jj in range(jc):
        j = jo * jc + jj
        tok = slice(jj * TCH, (jj + 1) * TCH)
        st = slot_t_ref[0, tok, :]
        s_blocks, y_blocks = [], []
        for g in range(n_e // EXPERT_GROUP):
            col = None
            for q in reversed(range(EXPERT_GROUP)):
                e = EXPERT_GROUP * g + q
                _, _, start = _slot_window(eb_ref, bi, e, j, n_e, n_chunks, cap)
                rel = st[:, e:e + 1] - start
                tgt = jnp.where(jnp.logical_and(rel >= 0, rel < SLOT_W), rel + q * SLOT_W, -1)
                col = tgt if col is None else jnp.where(ci < (q + 1) * SLOT_W, tgt, col)
                y_blocks.insert(g * EXPERT_GROUP, y_ref[0, e, pl.ds(start, SLOT_W), :])
            s_blocks.append(jnp.where(col == ci, 1.0, 0.0).astype(BF16))
        scat = jnp.concatenate(s_blocks, axis=1)
        ywin = jnp.concatenate(y_blocks, axis=0)
        moe_ref[tok, :] = jnp.dot(scat, ywin, preferred_element_type=F32)

        @pl.when(_window_overflow(eb_ref, bi, j, n_e, n_chunks, cap))
        def _():
            cf = lax.broadcasted_iota(jnp.int32, (TCH, cap), 1)
            dense = jnp.concatenate(
                [jnp.where(st[:, e:e + 1] == cf, 1.0, 0.0).astype(BF16) for e in range(n_e)], axis=1)
            moe_ref[tok, :] = jnp.dot(dense, y_ref[0].reshape(n_e * cap, y_ref.shape[3]),
                                      preferred_element_type=F32)

    x = xn_ref[0] + mod_ref[0][5:6] * moe_ref[...]
    o_ref[0] = _rms(x, nf_ref[...])


def _combine(ebound, slot_t, y, x_new, mod, norm_final, cap, jc=2):
    b, n, d = x_new.shape
    e = slot_t.shape[2]
    n_chunks = n // TCH
    tile = lambda i, j, *_: (i, j, 0)
    return pl.pallas_call(
        functools.partial(_combine_kernel, cap=cap, jc=jc, n_chunks=n_chunks),
        grid_spec=pltpu.PrefetchScalarGridSpec(
            num_scalar_prefetch=1,
            grid=(b, n_chunks // jc),
            in_specs=[pl.BlockSpec((1, jc * TCH, e), tile),
                      pl.BlockSpec((1, e, cap, d), lambda i, j, *_: (i, 0, 0, 0)),
                      pl.BlockSpec((1, jc * TCH, d), tile),
                      pl.BlockSpec((1, N_MOD, d), lambda i, j, *_: (i, 0, 0)),
                      pl.BlockSpec((1, d), lambda i, j, *_: (0, 0))],
            out_specs=pl.BlockSpec((1, jc * TCH, d), tile),
            scratch_shapes=[pltpu.VMEM((jc * TCH, d), F32)]),
        out_shape=jax.ShapeDtypeStruct((b, n, d), F32),
        compiler_params=_params("parallel", "arbitrary"),
        name="moe_combine_norm",
    )(ebound, slot_t, y, x_new, mod, norm_final)


def kernel(x, c, ctx, c_ctx, w_mod, b_mod, norm1, w_in, rpb, w_s, b_s, gmlp_norm, out_norm_a, out_norm_b,
           w_out, norm2, w_router, w_gate, w_up, w_down, norm_final):
    b, n, d = x.shape
    assert w_mod.shape[0] == 1, "single-layer stack only"
    assert n % (GRID_W * Q_ROWS) == 0 and n // GRID_W >= W_ROWS
    assert n % (2 * TCH) == 0 and N_EXPERTS % EXPERT_GROUP == 0
    cap = EC_CAPACITY_FACTOR * n // N_EXPERTS

    pad = (-(b + 1)) % 8
    cc = jnp.concatenate([c, c_ctx[None], jnp.zeros((pad, d), F32)], axis=0)
    m = _modulation(cc, w_mod[0], b_mod[0][None])
    mod = m.reshape(-1, N_MOD, d)

    w_in_b = w_in[0].astype(BF16)
    ws2 = w_s[0].astype(BF16).reshape(N_GROUPS_SG // 2, 2 * CHUNK, CHUNK)
    bs2 = jnp.broadcast_to(b_s[0].reshape(N_GROUPS_SG // 2, 2 * CHUNK, 1), (N_GROUPS_SG // 2, 2 * CHUNK, LANES))
    q, k, v, ob = _in_proj(x, mod, norm1, w_in_b, ws2, bs2, gmlp_norm, out_norm_b)

    kc, vc = _ctx_proj(ctx.reshape(b * CTX_LEN, d), mod, b, norm1, w_in_b)
    kc = kc.reshape(b, CTX_LEN, D_NA)
    vc = vc.reshape(b, CTX_LEN, D_NA)

    bias = _bias_tables(rpb[0], n // GRID_W)
    oa = _attention(q, k, v, kc, vc, bias, out_norm_a)

    x_new, h2, aff, aff_terms = _out_proj(oa, ob, x, mod, w_out[0].astype(BF16), norm2, w_router[0].T)

    slot, ebound = _topk_slots(aff.reshape(b * N_EXPERTS, n), cap)
    slot = slot.reshape(b, N_EXPERTS, n)
    ebound = ebound[:, :n // TCH + 1].reshape(-1)

    xs, gate = _gather_tokens(ebound, h2, slot, aff_terms, cap)
    y = _experts(xs, w_gate[0], w_up[0], w_down[0], gate)
    return _combine(ebound, jnp.swapaxes(slot, 1, 2), y, x_new, mod, norm_final[None], cap)
```

```python
import functools

import numpy as np
import jax
import jax.numpy as jnp
from jax import lax
from jax.experimental import pallas as pl
from jax.experimental.pallas import tpu as pltpu

D_MODEL = 1024
GRID_W = 64
CTX_LEN = 256
N_HEADS_NA = 8
HEAD_DIM = 64
D_NA = N_HEADS_NA * HEAD_DIM
NA_KH = 8
NA_KW = 16
D_SG = D_MODEL - D_NA
N_GROUPS_SG = 8
SG_GROUP_DIM = D_SG // N_GROUPS_SG
CHUNK = 128
N_EXPERTS = 16
EC_CAPACITY_FACTOR = 2
D_EXPERT = 1536
N_MOD = 6
EPS = 1e-6
NEG_INF = -1e30
LOG2E = float(np.log2(np.e))

LANES = 128
VMEM_LIMIT = 56 * 1024 * 1024

F32 = jnp.float32
BF16 = jnp.bfloat16
HIGHEST = lax.Precision.HIGHEST

Q_ROWS = 4
W_ROWS = 12
TQ = Q_ROWS * GRID_W
TW = W_ROWS * GRID_W

MANTISSA_STEPS = 36

TCH = 256
SLOT_W = 64
SLOT_ALIGN = 16
EXPERT_GROUP = 4
GATE_TERMS = 3


def _params(*sem):
    return pltpu.CompilerParams(dimension_semantics=sem, vmem_limit_bytes=VMEM_LIMIT)


def _rms_mod(x, g, shift, scale):
    r = lax.rsqrt(jnp.mean(x * x, axis=-1, keepdims=True) + EPS)
    return (x * r) * g * (1.0 + scale) + shift


def _rms(x, g):
    return x * lax.rsqrt(jnp.mean(x * x, axis=-1, keepdims=True) + EPS) * g


def _gelu_tanh(x):
    return 0.5 * x * (1.0 + jnp.tanh(np.sqrt(2.0 / np.pi).astype(np.float32) * (x + 0.044715 * (x * x * x))))


def _silu(x):
    return x * jax.nn.sigmoid(x)


def _bf16_terms(x, n_terms):
    terms = []
    for _ in range(n_terms):
        t = x.astype(BF16)
        terms.append(t)
        x = x - t.astype(F32)
    return terms


def _mod_kernel(c_ref, w_ref, b_ref, o_ref):
    s = _silu(c_ref[...])
    o_ref[...] = jnp.dot(s, w_ref[...], precision=HIGHEST, preferred_element_type=F32) + b_ref[...]


def _modulation(cc, w_mod, b_mod):
    rows, d = cc.shape
    n = w_mod.shape[1]
    tn = 1024
    return pl.pallas_call(
        _mod_kernel,
        grid=(n // tn,),
        in_specs=[pl.BlockSpec((rows, d), lambda j: (0, 0)),
                  pl.BlockSpec((d, tn), lambda j: (0, j)),
                  pl.BlockSpec((1, tn), lambda j: (0, j))],
        out_specs=pl.BlockSpec((rows, tn), lambda j: (0, j)),
        out_shape=jax.ShapeDtypeStruct((rows, n), F32),
        compiler_params=_params("arbitrary"),
        name="modulation",
    )(cc, w_mod, b_mod)


def _in_kernel(x_ref, mod_ref, n1_ref, w_ref, ws_ref, bs_ref, gn_ref, onb_ref,
               q_ref, k_ref, v_ref, ob_ref, sp_ref, *, sub):
    mod = mod_ref[0]
    tiles = [slice(r, r + sub) for r in range(0, x_ref.shape[1], sub)]
    uzs = []
    for rs in tiles:
        hb = _rms_mod(x_ref[0, rs], n1_ref[...], mod[0:1], mod[1:2]).astype(BF16)
        qkv = jnp.dot(hb, w_ref[:, :3 * D_NA], preferred_element_type=F32)
        q_ref[0, rs] = (qkv[:, :D_NA] * (HEAD_DIM ** -0.5 * LOG2E)).astype(BF16)
        k_ref[0, rs] = qkv[:, D_NA:2 * D_NA].astype(BF16)
        v_ref[0, rs] = qkv[:, 2 * D_NA:].astype(BF16)
        uzs.append(jnp.dot(hb, w_ref[:, 3 * D_NA:], preferred_element_type=F32))

    nch = sub // CHUNK
    lane = lax.broadcasted_iota(jnp.int32, (CHUNK, LANES), 1)
    first = lane < SG_GROUP_DIM
    for rs, uz in zip(tiles, uzs):
        u = _gelu_tanh(uz[:, :D_SG])
        z = _gelu_tanh(uz[:, D_SG:])
        mu = jnp.mean(z, axis=-1, keepdims=True)
        zc = z - mu
        var = jnp.mean(zc * zc, axis=-1, keepdims=True)
        zb = (zc * lax.rsqrt(var + EPS) * gn_ref[...]).astype(BF16)
        for p in range(N_GROUPS_SG // 2):
            zp = jnp.concatenate(
                [zb[c * CHUNK:(c + 1) * CHUNK, p * LANES:(p + 1) * LANES] for c in range(nch)], axis=1)
            r = jnp.dot(ws_ref[p], zp, preferred_element_type=F32)
            bs = bs_ref[p]
            for c in range(nch):
                top = r[:CHUNK, c * LANES:(c + 1) * LANES] + bs[:CHUNK]
                bot = r[CHUNK:, c * LANES:(c + 1) * LANES] + bs[CHUNK:]
                sp_ref[rs.start + c * CHUNK:rs.start + (c + 1) * CHUNK, p * LANES:(p + 1) * LANES] = (
                    jnp.where(first, top, bot))
        ob = u * sp_ref[rs, :]
        ob_ref[0, rs] = _rms(ob, onb_ref[...]).astype(BF16)


def _in_proj(x, mod, norm1, w_in_b, ws2, bs2, gmlp_norm, out_norm_b, tm=1024, sub=512):
    b, n, d = x.shape
    d_in = w_in_b.shape[1]
    full2 = lambda i, j: (0, 0)
    full3 = lambda i, j: (0, 0, 0)
    tile = lambda i, j: (i, j, 0)
    act = jax.ShapeDtypeStruct((b, n, D_NA), BF16)
    return pl.pallas_call(
        functools.partial(_in_kernel, sub=sub),
        grid=(b, n // tm),
        in_specs=[pl.BlockSpec((1, tm, d), tile),
                  pl.BlockSpec((1, N_MOD, d), lambda i, j: (i, 0, 0)),
                  pl.BlockSpec((1, d), full2),
                  pl.BlockSpec((d, d_in), full2),
                  pl.BlockSpec(ws2.shape, full3),
                  pl.BlockSpec(bs2.shape, full3),
                  pl.BlockSpec((1, D_SG), full2),
                  pl.BlockSpec((1, D_SG), full2)],
        out_specs=[pl.BlockSpec((1, tm, D_NA), tile)] * 4,
        out_shape=[act] * 4,
        scratch_shapes=[pltpu.VMEM((tm, D_SG), F32)],
        compiler_params=_params("parallel", "arbitrary"),
        name="in_proj_gmlp",
    )(x, mod, norm1, w_in_b, ws2, bs2, gmlp_norm, out_norm_b)


def _ctx_kernel(x_ref, mod_ref, n1_ref, wk_ref, wv_ref, k_ref, v_ref):
    mod = mod_ref[0]
    hb = _rms_mod(x_ref[...], n1_ref[...], mod[0:1], mod[1:2]).astype(BF16)
    k_ref[...] = jnp.dot(hb, wk_ref[...], preferred_element_type=F32).astype(BF16)
    v_ref[...] = jnp.dot(hb, wv_ref[...], preferred_element_type=F32).astype(BF16)


def _ctx_proj(ctx2, mod, mod_row, norm1, w_in_b, tm=512):
    rows, d = ctx2.shape
    act = jax.ShapeDtypeStruct((rows, D_NA), BF16)
    return pl.pallas_call(
        _ctx_kernel,
        grid=(rows // tm,),
        in_specs=[pl.BlockSpec((tm, d), lambda i: (i, 0)),
                  pl.BlockSpec((1, N_MOD, d), lambda i: (mod_row, 0, 0)),
                  pl.BlockSpec((1, d), lambda i: (0, 0)),
                  pl.BlockSpec((d, D_NA), lambda i: (0, 1)),
                  pl.BlockSpec((d, D_NA), lambda i: (0, 2))],
        out_specs=[pl.BlockSpec((tm, D_NA), lambda i: (i, 0))] * 2,
        out_shape=[act, act],
        compiler_params=_params("arbitrary"),
        name="ctx_kv_proj",
    )(ctx2, mod, norm1, w_in_b, w_in_b)


def _window_start(t, rows):
    return jnp.clip(Q_ROWS * t - NA_KH // 2, 0, rows - W_ROWS)


def _attn_kernel(tbl_ref, q_ref, k_ref, v_ref, kc_ref, vc_ref, pb_ref, ona_ref, o_ref, *, rows):
    t = pl.program_id(1)
    n_tiles = rows // Q_ROWS
    ws = pl.multiple_of(_window_start(t, rows) * GRID_W, GRID_W)
    kw = k_ref[0, pl.ds(ws, TW), :]
    vw = v_ref[0, pl.ds(ws, TW), :]
    q = q_ref[0]
    kc = kc_ref[0]
    vc = vc_ref[0]
    lane = lax.broadcasted_iota(jnp.int32, (TQ, LANES), 1)
    first = lane < HEAD_DIM
    nt = (((1,), (1,)), ((), ()))
    cls = jnp.where(t == 0, 0, jnp.where(t == n_tiles - 1, 2, 1))
    pair = [[tbl_ref[(cls * Q_ROWS + r) * (W_ROWS // 2) + j] for j in range(W_ROWS // 2)] for r in range(Q_ROWS)]
    outs = []
    for p in range(N_HEADS_NA // 2):
        sl = slice(p * LANES, (p + 1) * LANES)
        qp, kp, vp, kcp, vcp = q[:, sl], kw[:, sl], vw[:, sl], kc[:, sl], vc[:, sl]
        halves = []
        for j in range(2):
            h = 2 * p + j
            qm = jnp.where(first if j == 0 else jnp.logical_not(first), qp, jnp.zeros_like(qp))
            bias = jnp.concatenate(
                [jnp.concatenate([pb_ref[h, pair[r][c]] for c in range(W_ROWS // 2)], axis=1)
                 for r in range(Q_ROWS)], axis=0)
            s_nb = lax.dot_general(qm, kp, nt, preferred_element_type=F32) + bias
            s_cx = lax.dot_general(qm, kcp, nt, preferred_element_type=F32)
            m = jnp.maximum(jnp.max(s_nb, axis=1, keepdims=True), jnp.max(s_cx, axis=1, keepdims=True))
            e_nb = jnp.exp2(s_nb - m)
            e_cx = jnp.exp2(s_cx - m)
            l = jnp.sum(e_nb, axis=1, keepdims=True) + jnp.sum(e_cx, axis=1, keepdims=True)
            o = (jnp.dot(e_nb.astype(BF16), vp, preferred_element_type=F32)
                 + jnp.dot(e_cx.astype(BF16), vcp, preferred_element_type=F32))
            halves.append(o / l)
        outs.append(jnp.where(first, halves[0], halves[1]))
    o = jnp.concatenate(outs, axis=1)
    o_ref[0] = _rms(o, ona_ref[...]).astype(BF16)


def _bias_blocks(rpb, rows):
    n_tiles = rows // Q_ROWS
    n_ro, n_co = 2 * NA_KH - 1, 2 * NA_KW - 1
    qc = np.arange(GRID_W)[:, None]
    kc = np.arange(GRID_W)[None, :]
    cs = np.clip(qc - NA_KW // 2, 0, GRID_W - NA_KW)
    col_ok = (kc >= cs) & (kc < cs + NA_KW)
    spread = ((kc - qc + NA_KW - 1)[None] == np.arange(n_co)[:, None, None]) & col_ok[None]
    blocks = jnp.dot(rpb.reshape(-1, n_co), jnp.asarray(spread.reshape(n_co, -1), F32), precision=HIGHEST)
    blocks = jnp.where(jnp.asarray(col_ok.reshape(-1)), blocks * LOG2E, NEG_INF)
    blocks = blocks.reshape(N_HEADS_NA, n_ro, GRID_W, GRID_W)
    masked = jnp.full((N_HEADS_NA, 1, GRID_W, GRID_W), NEG_INF, F32)
    ext = jnp.concatenate([blocks, masked], axis=1)
    pairs, table = [], []
    for t in (0, 1, n_tiles - 1):
        ws = int(np.clip(Q_ROWS * t - NA_KH // 2, 0, rows - W_ROWS))
        for rho in range(Q_ROWS):
            r = Q_ROWS * t + rho
            rs = int(np.clip(r - NA_KH // 2, 0, rows - NA_KH))
            off = [kr - r + NA_KH - 1 if rs <= kr < rs + NA_KH else n_ro for kr in range(ws, ws + W_ROWS)]
            for c in range(W_ROWS // 2):
                pr = (off[2 * c], off[2 * c + 1])
                if pr not in pairs:
                    pairs.append(pr)
                table.append(pairs.index(pr))
    pb = jnp.stack([jnp.concatenate([ext[:, a], ext[:, b]], axis=-1) for a, b in pairs], axis=1)
    return pb, jnp.asarray(np.asarray(table, np.int32))


def _attention(q, k, v, kc, vc, pb, table, out_norm_a):
    b, n, _ = q.shape
    rows = n // GRID_W
    n_tiles = rows // Q_ROWS
    per_b = lambda i, t, *_: (i, 0, 0)
    return pl.pallas_call(
        functools.partial(_attn_kernel, rows=rows),
        grid_spec=pltpu.PrefetchScalarGridSpec(
            num_scalar_prefetch=1,
            grid=(b, n_tiles),
            in_specs=[pl.BlockSpec((1, TQ, D_NA), lambda i, t, *_: (i, t, 0)),
                      pl.BlockSpec((1, n, D_NA), per_b),
                      pl.BlockSpec((1, n, D_NA), per_b),
                      pl.BlockSpec((1, CTX_LEN, D_NA), per_b),
                      pl.BlockSpec((1, CTX_LEN, D_NA), per_b),
                      pl.BlockSpec(pb.shape, lambda i, t, *_: (0, 0, 0, 0)),
                      pl.BlockSpec((1, D_NA), lambda i, t, *_: (0, 0))],
            out_specs=pl.BlockSpec((1, TQ, D_NA), lambda i, t, *_: (i, t, 0))),
        out_shape=jax.ShapeDtypeStruct((b, n, D_NA), BF16),
        compiler_params=_params("parallel", "arbitrary"),
        name="nbr_attention",
    )(table, q, k, v, kc, vc, pb, out_norm_a)


def _out_kernel(oa_ref, ob_ref, x_ref, mod_ref, w_ref, n2_ref, wr_ref, xn_ref, h2_ref, aff_ref, terms_ref, *,
                sub):
    mod = mod_ref[0]
    nt = (((1,), (1,)), ((), ()))
    w_hi, w_lo = _bf16_terms(wr_ref[...], 2)
    w2 = jnp.concatenate([w_hi, w_lo], axis=0)
    tiles = [slice(r, r + sub) for r in range(0, x_ref.shape[1], sub)]
    mixes = [jnp.dot(oa_ref[0, rs], w_ref[:D_NA], preferred_element_type=F32)
             + jnp.dot(ob_ref[0, rs], w_ref[D_NA:], preferred_element_type=F32) for rs in tiles]
    for rs, mix in zip(tiles, mixes):
        xn = x_ref[0, rs] + mod[2:3] * mix
        xn_ref[0, rs] = xn
        h2 = _rms_mod(xn, n2_ref[...], mod[3:4], mod[4:5])
        h_hi, h_lo = _bf16_terms(h2, 2)
        h2_ref[0, rs] = h_hi
        l_hi = lax.dot_general(w2, h_hi, nt, preferred_element_type=F32)
        l_lo = lax.dot_general(w_hi, h_lo, nt, preferred_element_type=F32)
        logits = l_hi[:N_EXPERTS] + l_hi[N_EXPERTS:] + l_lo
        e = jnp.exp(logits - jnp.max(logits, axis=0, keepdims=True))
        aff = e / jnp.sum(e, axis=0, keepdims=True)
        aff_ref[0, :, rs] = aff
        terms_ref[0, :, rs] = jnp.concatenate(_bf16_terms(aff, 3), axis=0)


def _out_proj(oa, ob, x, mod, w_out_b, norm2, w_router_t, tm=1024, sub=512):
    b, n, d = x.shape
    tile = lambda i, j: (i, j, 0)
    full = lambda i, j: (0, 0)
    return pl.pallas_call(
        functools.partial(_out_kernel, sub=sub),
        grid=(b, n // tm),
        in_specs=[pl.BlockSpec((1, tm, D_NA), tile),
                  pl.BlockSpec((1, tm, D_SG), tile),
                  pl.BlockSpec((1, tm, d), tile),
                  pl.BlockSpec((1, N_MOD, d), lambda i, j: (i, 0, 0)),
                  pl.BlockSpec(w_out_b.shape, full),
                  pl.BlockSpec((1, d), full),
                  pl.BlockSpec((N_EXPERTS, d), full)],
        out_specs=[pl.BlockSpec((1, tm, d), tile),
                   pl.BlockSpec((1, tm, d), tile),
                   pl.BlockSpec((1, N_EXPERTS, tm), lambda i, j: (i, 0, j)),
                   pl.BlockSpec((1, GATE_TERMS * N_EXPERTS, tm), lambda i, j: (i, 0, j))],
        out_shape=[jax.ShapeDtypeStruct((b, n, d), F32),
                   jax.ShapeDtypeStruct((b, n, d), BF16),
                   jax.ShapeDtypeStruct((b, N_EXPERTS, n), F32),
                   jax.ShapeDtypeStruct((b, GATE_TERMS * N_EXPERTS, n), BF16)],
        compiler_params=_params("parallel", "arbitrary"),
        name="out_proj_router",
    )(oa, ob, x, mod, w_out_b, norm2, w_router_t)


def _prefix_count(mask_f, tri):
    rows, n = mask_f.shape
    parts = []
    carry = jnp.zeros((rows, 1), F32)
    for j in range(n // LANES):
        blk = mask_f[:, j * LANES:(j + 1) * LANES]
        parts.append(jnp.dot(blk.astype(BF16), tri, preferred_element_type=F32) + carry)
        carry = carry + jnp.sum(blk, axis=1, keepdims=True)
    return jnp.concatenate(parts, axis=1)


def _topk_kernel(aff_ref, slot_ref, eb_ref, *, cap):
    a = aff_ref[...]
    rows = a.shape[0]

    def enough(t):
        return jnp.sum(jnp.where(a >= t, 1.0, 0.0), axis=1, keepdims=True) >= cap

    tiny = jnp.full((rows, 1), 2.0 ** -126, F32)
    normal = enough(tiny)
    pw = tiny
    hi = jnp.full((rows, 1), 4.0, F32)
    for bit in range(6, -1, -1):
        cand = pw * (2.0 ** (1 << bit))
        ok = enough(cand)
        pw = jnp.where(ok, cand, pw)
        hi = jnp.where(ok, hi, cand)
    lo = jnp.where(normal, pw, 0.0)
    hi = jnp.where(normal, hi, tiny)
    step = lo
    for _ in range(MANTISSA_STEPS):
        step = step * 0.5
        cand = lo + step
        ok = enough(cand)
        lo = jnp.where(ok, cand, lo)
        hi = jnp.where(ok, hi, cand)
    above = a >= hi
    tie = jnp.logical_and(a >= lo, jnp.logical_not(above))
    n_above = jnp.sum(jnp.where(above, 1.0, 0.0), axis=1, keepdims=True)
    ri = lax.broadcasted_iota(jnp.int32, (LANES, LANES), 0)
    ci = lax.broadcasted_iota(jnp.int32, (LANES, LANES), 1)
    tri = jnp.where(ri <= ci, 1.0, 0.0).astype(BF16)
    tie_rank = _prefix_count(jnp.where(tie, 1.0, 0.0), tri)
    sel = jnp.logical_or(above, jnp.logical_and(tie, tie_rank <= cap - n_above))
    sel_f = jnp.where(sel, 1.0, 0.0)
    pos = _prefix_count(sel_f, tri) - 1.0
    slot_ref[...] = jnp.where(sel, pos, -1.0).astype(jnp.int32)
    lane = lax.broadcasted_iota(jnp.int32, (rows, LANES), 1)
    cnt = jnp.zeros((rows, LANES), F32)
    for j in range(a.shape[1] // TCH):
        cnt = jnp.where(lane == j, jnp.sum(sel_f[:, j * TCH:(j + 1) * TCH], axis=1, keepdims=True), cnt)
    before = jnp.where(ri < ci, 1.0, 0.0).astype(BF16)
    eb_ref[...] = jnp.dot(cnt.astype(BF16), before, preferred_element_type=F32).astype(jnp.int32)


def _topk_slots(aff2, cap):
    rows, n = aff2.shape
    tr = 32
    return pl.pallas_call(
        functools.partial(_topk_kernel, cap=cap),
        grid=(rows // tr,),
        in_specs=[pl.BlockSpec((tr, n), lambda i: (i, 0))],
        out_specs=[pl.BlockSpec((tr, n), lambda i: (i, 0)), pl.BlockSpec((tr, LANES), lambda i: (i, 0))],
        out_shape=[jax.ShapeDtypeStruct((rows, n), jnp.int32), jax.ShapeDtypeStruct((rows, LANES), jnp.int32)],
        compiler_params=_params("parallel"),
        name="expert_topk",
    )(aff2)


def _slot_window(eb_ref, bi, e, j, n_e, n_chunks, cap):
    base = (bi * n_e + e) * (n_chunks + 1) + j
    s0 = eb_ref[base]
    s1 = eb_ref[base + 1]
    start = jnp.minimum(s0 & -SLOT_ALIGN, cap - SLOT_W)
    return s0, s1, pl.multiple_of(start, SLOT_ALIGN)


def _window_overflow(eb_ref, bi, j, n_e, n_chunks, cap):
    over = None
    for e in range(n_e):
        _, s1, start = _slot_window(eb_ref, bi, e, j, n_e, n_chunks, cap)
        o = s1 > start + SLOT_W
        over = o if over is None else jnp.logical_or(over, o)
    return over


def _gate_lanes(res, e, n_e):
    lane = lax.broadcasted_iota(jnp.int32, res.shape, 1)
    head = jnp.logical_or(lane == e, lane == n_e + e)
    t12 = jnp.sum(jnp.where(head, res, 0.0), axis=1, keepdims=True)
    t3 = jnp.sum(jnp.where(lane == 2 * n_e + e, res, 0.0), axis=1, keepdims=True)
    return t12 + t3


def _gather_kernel(eb_ref, h2_ref, slot_ref, ap_ref, xs_ref, gate_ref, *, cap, jc, n_chunks):
    bi = pl.program_id(0)
    jo = pl.program_id(1)
    n_e = slot_ref.shape[1]
    wi = lax.broadcasted_iota(jnp.int32, (SLOT_W, TCH), 0)
    wcol = lax.broadcasted_iota(jnp.int32, (SLOT_W, 1), 0)
    nt = (((1,), (1,)), ((), ()))

    @pl.when(jo == 0)
    def _():
        xs_ref[...] = jnp.zeros_like(xs_ref)
        gate_ref[...] = jnp.zeros_like(gate_ref)

    for jj in range(jc):
        j = jo * jc + jj
        tok = slice(jj * TCH, (jj + 1) * TCH)
        h2c = h2_ref[0, tok, :]
        apc = ap_ref[0, :, tok]
        for g in range(n_e // EXPERT_GROUP):
            blocks, meta = [], []
            for q in range(EXPERT_GROUP):
                e = EXPERT_GROUP * g + q
                s0, s1, start = _slot_window(eb_ref, bi, e, j, n_e, n_chunks, cap)
                blocks.append(jnp.where(slot_ref[0, e:e + 1, tok] - start == wi, 1.0, 0.0).astype(BF16))
                meta.append((e, s0, s1, start))
            onehot = jnp.concatenate(blocks, axis=0)
            rows = jnp.dot(onehot, h2c, preferred_element_type=F32).astype(BF16)
            aff3 = lax.dot_general(onehot, apc, nt, preferred_element_type=F32)
            for q, (e, s0, s1, start) in enumerate(meta):
                own = jnp.logical_and(wcol + start >= s0, wcol + start < s1)
                win = pl.ds(start, SLOT_W)
                blk = slice(q * SLOT_W, (q + 1) * SLOT_W)
                xs_ref[0, e, win, :] = jnp.where(own, rows[blk], xs_ref[0, e, win, :])
                gate_ref[0, e, win, :] = jnp.where(own, _gate_lanes(aff3[blk], e, n_e), gate_ref[0, e, win, :])

        @pl.when(_window_overflow(eb_ref, bi, j, n_e, n_chunks, cap))
        def _():
            ci = lax.broadcasted_iota(jnp.int32, (cap, TCH), 0)
            ccol = lax.broadcasted_iota(jnp.int32, (cap, 1), 0)
            for e in range(n_e):
                s0, s1, _ = _slot_window(eb_ref, bi, e, j, n_e, n_chunks, cap)
                onehot = jnp.where(slot_ref[0, e:e + 1, tok] == ci, 1.0, 0.0).astype(BF16)
                rows = jnp.dot(onehot, h2c, preferred_element_type=F32).astype(BF16)
                aff3 = lax.dot_general(onehot, apc, nt, preferred_element_type=F32)
                own = jnp.logical_and(ccol >= s0, ccol < s1)
                xs_ref[0, e] = jnp.where(own, rows, xs_ref[0, e])
                gate_ref[0, e] = jnp.where(own, _gate_lanes(aff3, e, n_e), gate_ref[0, e])


def _gather_tokens(ebound, h2, slot, aff_terms, cap, jc=2):
    b, n, d = h2.shape
    e = slot.shape[1]
    n_chunks = n // TCH
    tok = lambda i, j, *_: (i, j, 0)
    per_b = lambda i, j, *_: (i, 0, 0, 0)
    return pl.pallas_call(
        functools.partial(_gather_kernel, cap=cap, jc=jc, n_chunks=n_chunks),
        grid_spec=pltpu.PrefetchScalarGridSpec(
            num_scalar_prefetch=1,
            grid=(b, n_chunks // jc),
            in_specs=[pl.BlockSpec((1, jc * TCH, d), tok),
                      pl.BlockSpec((1, e, jc * TCH), lambda i, j, *_: (i, 0, j)),
                      pl.BlockSpec((1, GATE_TERMS * e, jc * TCH), lambda i, j, *_: (i, 0, j))],
            out_specs=[pl.BlockSpec((1, e, cap, d), per_b),
                       pl.BlockSpec((1, e, cap, 1), per_b)]),
        out_shape=[jax.ShapeDtypeStruct((b, e, cap, d), BF16),
                   jax.ShapeDtypeStruct((b, e, cap, 1), F32)],
        compiler_params=_params("parallel", "arbitrary"),
        name="moe_gather",
    )(ebound, h2, slot, aff_terms)


def _expert_kernel(xs_ref, wg_ref, wu_ref, wd_ref, gate_ref, y_ref, acc_ref, *, rb):
    f = pl.program_id(1)

    @pl.when(f == 0)
    def _():
        acc_ref[...] = jnp.zeros_like(acc_ref)

    wg = wg_ref[0].astype(BF16)
    wu = wu_ref[0].astype(BF16)
    wd = wd_ref[0].astype(BF16)
    nb, _, cap, d = xs_ref.shape
    for i in range(nb // rb):
        xs = xs_ref[i * rb:(i + 1) * rb, 0].reshape(rb * cap, d)
        a = jnp.dot(xs, wg, preferred_element_type=F32)
        u = jnp.dot(xs, wu, preferred_element_type=F32)
        hm = (_silu(a) * u).astype(BF16)
        part = jnp.dot(hm, wd, preferred_element_type=F32).reshape(rb, cap, d)
        acc_ref[i * rb:(i + 1) * rb] += part

    @pl.when(f == pl.num_programs(1) - 1)
    def _():
        y_ref[:, 0] = (acc_ref[...] * gate_ref[:, 0]).astype(BF16)


def _experts(xs, w_gate, w_up, w_down, gate, fc=768, rb=4):
    b, e, cap, d = xs.shape
    dff = w_gate.shape[2]
    per_e = lambda i, f: (0, i, 0, 0)
    return pl.pallas_call(
        functools.partial(_expert_kernel, rb=rb),
        grid=(e, dff // fc),
        in_specs=[pl.BlockSpec((b, 1, cap, d), per_e),
                  pl.BlockSpec((1, d, fc), lambda i, f: (i, 0, f)),
                  pl.BlockSpec((1, d, fc), lambda i, f: (i, 0, f)),
                  pl.BlockSpec((1, fc, d), lambda i, f: (i, f, 0)),
                  pl.BlockSpec((b, 1, cap, 1), per_e)],
        out_specs=pl.BlockSpec((b, 1, cap, d), per_e),
        out_shape=jax.ShapeDtypeStruct((b, e, cap, d), BF16),
        scratch_shapes=[pltpu.VMEM((b, cap, d), F32)],
        compiler_params=_params("parallel", "arbitrary"),
        name="moe_experts",
    )(xs, w_gate, w_up, w_down, gate)


def _combine_kernel(eb_ref, slot_t_ref, y_ref, xn_ref, mod_ref, nf_ref, o_ref, moe_ref, *,
                    cap, jc, n_chunks):
    bi = pl.program_id(0)
    jo = pl.program_id(1)
    n_e = y_ref.shape[1]
    ci = lax.broadcasted_iota(jnp.int32, (TCH, EXPERT_GROUP * SLOT_W), 1)
    for jj in range(jc):
        j = jo * jc + jj
        tok = slice(jj * TCH, (jj + 1) * TCH)
        st = slot_t_ref[0, tok, :]
        s_blocks, y_blocks = [], []
        for g in range(n_e // EXPERT_GROUP):
            col = None
            for q in reversed(range(EXPERT_GROUP)):
                e = EXPERT_GROUP * g + q
                _, _, start = _slot_window(eb_ref, bi, e, j, n_e, n_chunks, cap)
                rel = st[:, e:e + 1] - start
                tgt = jnp.where(jnp.logical_and(rel >= 0, rel < SLOT_W), rel + q * SLOT_W, -1)
                col = tgt if col is None else jnp.where(ci < (q + 1) * SLOT_W, tgt, col)
                y_blocks.insert(g * EXPERT_GROUP, y_ref[0, e, pl.ds(start, SLOT_W), :])
            s_blocks.append(jnp.where(col == ci, 1.0, 0.0).astype(BF16))
        scat = jnp.concatenate(s_blocks, axis=1)
        ywin = jnp.concatenate(y_blocks, axis=0)
        moe_ref[tok, :] = jnp.dot(scat, ywin, preferred_element_type=F32)

        @pl.when(_window_overflow(eb_ref, bi, j, n_e, n_chunks, cap))
        def _():
            cf = lax.broadcasted_iota(jnp.int32, (TCH, cap), 1)
            dense = jnp.concatenate(
                [jnp.where(st[:, e:e + 1] == cf, 1.0, 0.0).astype(BF16) for e in range(n_e)], axis=1)
            moe_ref[tok, :] = jnp.dot(dense, y_ref[0].reshape(n_e * cap, y_ref.shape[3]),
                                      preferred_element_type=F32)

    x = xn_ref[0] + mod_ref[0][5:6] * moe_ref[...]
    o_ref[0] = _rms(x, nf_ref[...])


def _combine(ebound, slot_t, y, x_new, mod, norm_final, cap, jc=2):
    b, n, d = x_new.shape
    e = slot_t.shape[2]
    n_chunks = n // TCH
    tile = lambda i, j, *_: (i, j, 0)
    return pl.pallas_call(
        functools.partial(_combine_kernel, cap=cap, jc=jc, n_chunks=n_chunks),
        grid_spec=pltpu.PrefetchScalarGridSpec(
            num_scalar_prefetch=1,
            grid=(b, n_chunks // jc),
            in_specs=[pl.BlockSpec((1, jc * TCH, e), tile),
                      pl.BlockSpec((1, e, cap, d), lambda i, j, *_: (i, 0, 0, 0)),
                      pl.BlockSpec((1, jc * TCH, d), tile),
                      pl.BlockSpec((1, N_MOD, d), lambda i, j, *_: (i, 0, 0)),
                      pl.BlockSpec((1, d), lambda i, j, *_: (0, 0))],
            out_specs=pl.BlockSpec((1, jc * TCH, d), tile),
            scratch_shapes=[pltpu.VMEM((jc * TCH, d), F32)]),
        out_shape=jax.ShapeDtypeStruct((b, n, d), F32),
        compiler_params=_params("parallel", "arbitrary"),
        name="moe_combine_norm",
    )(ebound, slot_t, y, x_new, mod, norm_final)


def kernel(x, c, ctx, c_ctx, w_mod, b_mod, norm1, w_in, rpb, w_s, b_s, gmlp_norm, out_norm_a, out_norm_b,
           w_out, norm2, w_router, w_gate, w_up, w_down, norm_final):
    b, n, d = x.shape
    assert w_mod.shape[0] == 1, "single-layer stack only"
    assert n % (GRID_W * Q_ROWS) == 0 and n // GRID_W >= W_ROWS
    assert n % (2 * TCH) == 0 and N_EXPERTS % EXPERT_GROUP == 0
    cap = EC_CAPACITY_FACTOR * n // N_EXPERTS

    pad = (-(b + 1)) % 8
    cc = jnp.concatenate([c, c_ctx[None], jnp.zeros((pad, d), F32)], axis=0)
    m = _modulation(cc, w_mod[0], b_mod[0][None])
    mod = m.reshape(-1, N_MOD, d)

    w_in_b = w_in[0].astype(BF16)
    ws2 = w_s[0].astype(BF16).reshape(N_GROUPS_SG // 2, 2 * CHUNK, CHUNK)
    bs2 = jnp.broadcast_to(b_s[0].reshape(N_GROUPS_SG // 2, 2 * CHUNK, 1), (N_GROUPS_SG // 2, 2 * CHUNK, LANES))
    q, k, v, ob = _in_proj(x, mod, norm1, w_in_b, ws2, bs2, gmlp_norm, out_norm_b)

    kc, vc = _ctx_proj(ctx.reshape(b * CTX_LEN, d), mod, b, norm1, w_in_b)
    kc = kc.reshape(b, CTX_LEN, D_NA)
    vc = vc.reshape(b, CTX_LEN, D_NA)

    pb, table = _bias_blocks(rpb[0], n // GRID_W)
    oa = _attention(q, k, v, kc, vc, pb, table, out_norm_a)

    x_new, h2, aff, aff_terms = _out_proj(oa, ob, x, mod, w_out[0].astype(BF16), norm2, w_router[0].T)

    slot, ebound = _topk_slots(aff.reshape(b * N_EXPERTS, n), cap)
    slot = slot.reshape(b, N_EXPERTS, n)
    ebound = ebound[:, :n // TCH + 1].reshape(-1)

    xs, gate = _gather_tokens(ebound, h2, slot, aff_terms, cap)
    y = _experts(xs, w_gate[0], w_up[0], w_down[0], gate)
    return _combine(ebound, jnp.swapaxes(slot, 1, 2), y, x_new, mod, norm_final[None], cap)
```

```python
import functools

import numpy as np
import jax
import jax.numpy as jnp
from jax import lax
from jax.experimental import pallas as pl
from jax.experimental.pallas import tpu as pltpu

D_MODEL = 1024
GRID_W = 64
CTX_LEN = 256
N_HEADS_NA = 8
HEAD_DIM = 64
D_NA = N_HEADS_NA * HEAD_DIM
NA_KH = 8
NA_KW = 16
D_SG = D_MODEL - D_NA
N_GROUPS_SG = 8
SG_GROUP_DIM = D_SG // N_GROUPS_SG
CHUNK = 128
N_EXPERTS = 16
EC_CAPACITY_FACTOR = 2
D_EXPERT = 1536
N_MOD = 6
EPS = 1e-6
NEG_INF = -1e30
LOG2E = float(np.log2(np.e))

LANES = 128
VMEM_LIMIT = 56 * 1024 * 1024

F32 = jnp.float32
BF16 = jnp.bfloat16
HIGHEST = lax.Precision.HIGHEST

Q_ROWS = 4
W_ROWS = 12
TQ = Q_ROWS * GRID_W
TW = W_ROWS * GRID_W

MANTISSA_STEPS = 36

TCH = 256
SLOT_W = 64
SLOT_ALIGN = 16
EXPERT_GROUP = 4
GATE_TERMS = 3


def _params(*sem):
    return pltpu.CompilerParams(dimension_semantics=sem, vmem_limit_bytes=VMEM_LIMIT)


def _rms_mod(x, g, shift, scale):
    r = lax.rsqrt(jnp.mean(x * x, axis=-1, keepdims=True) + EPS)
    return (x * r) * g * (1.0 + scale) + shift


def _rms(x, g):
    return x * lax.rsqrt(jnp.mean(x * x, axis=-1, keepdims=True) + EPS) * g


def _gelu_tanh(x):
    return 0.5 * x * (1.0 + jnp.tanh(np.sqrt(2.0 / np.pi).astype(np.float32) * (x + 0.044715 * (x * x * x))))


def _silu(x):
    return x * jax.nn.sigmoid(x)


def _bf16_terms(x, n_terms):
    terms = []
    for _ in range(n_terms):
        t = x.astype(BF16)
        terms.append(t)
        x = x - t.astype(F32)
    return terms


def _mod_kernel(c_ref, w_ref, b_ref, o_ref):
    s = _silu(c_ref[...])
    o_ref[...] = jnp.dot(s, w_ref[...], precision=HIGHEST, preferred_element_type=F32) + b_ref[...]


def _modulation(cc, w_mod, b_mod):
    rows, d = cc.shape
    n = w_mod.shape[1]
    tn = 1024
    return pl.pallas_call(
        _mod_kernel,
        grid=(n // tn,),
        in_specs=[pl.BlockSpec((rows, d), lambda j: (0, 0)),
                  pl.BlockSpec((d, tn), lambda j: (0, j)),
                  pl.BlockSpec((1, tn), lambda j: (0, j))],
        out_specs=pl.BlockSpec((rows, tn), lambda j: (0, j)),
        out_shape=jax.ShapeDtypeStruct((rows, n), F32),
        compiler_params=_params("arbitrary"),
        name="modulation",
    )(cc, w_mod, b_mod)


def _in_kernel(x_ref, mod_ref, n1_ref, w_ref, ws_ref, bs_ref, gn_ref, onb_ref,
               q_ref, k_ref, v_ref, ob_ref, sp_ref, *, sub):
    mod = mod_ref[0]
    tiles = [slice(r, r + sub) for r in range(0, x_ref.shape[1], sub)]
    uzs = []
    for rs in tiles:
        hb = _rms_mod(x_ref[0, rs], n1_ref[...], mod[0:1], mod[1:2]).astype(BF16)
        qkv = jnp.dot(hb, w_ref[:, :3 * D_NA], preferred_element_type=F32)
        q_ref[0, rs] = (qkv[:, :D_NA] * (HEAD_DIM ** -0.5 * LOG2E)).astype(BF16)
        k_ref[0, rs] = qkv[:, D_NA:2 * D_NA].astype(BF16)
        v_ref[0, rs] = qkv[:, 2 * D_NA:].astype(BF16)
        uzs.append(jnp.dot(hb, w_ref[:, 3 * D_NA:], preferred_element_type=F32))

    nch = sub // CHUNK
    lane = lax.broadcasted_iota(jnp.int32, (CHUNK, LANES), 1)
    first = lane < SG_GROUP_DIM
    for rs, uz in zip(tiles, uzs):
        u = _gelu_tanh(uz[:, :D_SG])
        z = _gelu_tanh(uz[:, D_SG:])
        mu = jnp.mean(z, axis=-1, keepdims=True)
        zc = z - mu
        var = jnp.mean(zc * zc, axis=-1, keepdims=True)
        zb = (zc * lax.rsqrt(var + EPS) * gn_ref[...]).astype(BF16)
        for p in range(N_GROUPS_SG // 2):
            zp = jnp.concatenate(
                [zb[c * CHUNK:(c + 1) * CHUNK, p * LANES:(p + 1) * LANES] for c in range(nch)], axis=1)
            r = jnp.dot(ws_ref[p], zp, preferred_element_type=F32)
            bs = bs_ref[p]
            for c in range(nch):
                top = r[:CHUNK, c * LANES:(c + 1) * LANES] + bs[:CHUNK]
                bot = r[CHUNK:, c * LANES:(c + 1) * LANES] + bs[CHUNK:]
                sp_ref[rs.start + c * CHUNK:rs.start + (c + 1) * CHUNK, p * LANES:(p + 1) * LANES] = (
                    jnp.where(first, top, bot))
        ob = u * sp_ref[rs, :]
        ob_ref[0, rs] = _rms(ob, onb_ref[...]).astype(BF16)


def _in_proj(x, mod, norm1, w_in_b, ws2, bs2, gmlp_norm, out_norm_b, tm=1024, sub=512):
    b, n, d = x.shape
    d_in = w_in_b.shape[1]
    full2 = lambda i, j: (0, 0)
    full3 = lambda i, j: (0, 0, 0)
    tile = lambda i, j: (i, j, 0)
    act = jax.ShapeDtypeStruct((b, n, D_NA), BF16)
    return pl.pallas_call(
        functools.partial(_in_kernel, sub=sub),
        grid=(b, n // tm),
        in_specs=[pl.BlockSpec((1, tm, d), tile),
                  pl.BlockSpec((1, N_MOD, d), lambda i, j: (i, 0, 0)),
                  pl.BlockSpec((1, d), full2),
                  pl.BlockSpec((d, d_in), full2),
                  pl.BlockSpec(ws2.shape, full3),
                  pl.BlockSpec(bs2.shape, full3),
                  pl.BlockSpec((1, D_SG), full2),
                  pl.BlockSpec((1, D_SG), full2)],
        out_specs=[pl.BlockSpec((1, tm, D_NA), tile)] * 4,
        out_shape=[act] * 4,
        scratch_shapes=[pltpu.VMEM((tm, D_SG), F32)],
        compiler_params=_params("parallel", "arbitrary"),
        name="in_proj_gmlp",
    )(x, mod, norm1, w_in_b, ws2, bs2, gmlp_norm, out_norm_b)


def _ctx_kernel(x_ref, mod_ref, n1_ref, wk_ref, wv_ref, k_ref, v_ref):
    mod = mod_ref[0]
    hb = _rms_mod(x_ref[...], n1_ref[...], mod[0:1], mod[1:2]).astype(BF16)
    k_ref[...] = jnp.dot(hb, wk_ref[...], preferred_element_type=F32).astype(BF16)
    v_ref[...] = jnp.dot(hb, wv_ref[...], preferred_element_type=F32).astype(BF16)


def _ctx_proj(ctx2, mod, mod_row, norm1, w_in_b, tm=512):
    rows, d = ctx2.shape
    act = jax.ShapeDtypeStruct((rows, D_NA), BF16)
    return pl.pallas_call(
        _ctx_kernel,
        grid=(rows // tm,),
        in_specs=[pl.BlockSpec((tm, d), lambda i: (i, 0)),
                  pl.BlockSpec((1, N_MOD, d), lambda i: (mod_row, 0, 0)),
                  pl.BlockSpec((1, d), lambda i: (0, 0)),
                  pl.BlockSpec((d, D_NA), lambda i: (0, 1)),
                  pl.BlockSpec((d, D_NA), lambda i: (0, 2))],
        out_specs=[pl.BlockSpec((tm, D_NA), lambda i: (i, 0))] * 2,
        out_shape=[act, act],
        compiler_params=_params("arbitrary"),
        name="ctx_kv_proj",
    )(ctx2, mod, norm1, w_in_b, w_in_b)


def _window_start(t, rows):
    return jnp.clip(Q_ROWS * t - NA_KH // 2, 0, rows - W_ROWS)


def _attn_kernel(tbl_ref, q_ref, k_ref, v_ref, kc_ref, vc_ref, pb_ref, ona_ref, o_ref, *, rows):
    t = pl.program_id(1)
    n_tiles = rows // Q_ROWS
    ws = pl.multiple_of(_window_start(t, rows) * GRID_W, GRID_W)
    kw = k_ref[0, pl.ds(ws, TW), :]
    vw = v_ref[0, pl.ds(ws, TW), :]
    q = q_ref[0]
    kc = kc_ref[0]
    vc = vc_ref[0]
    lane = lax.broadcasted_iota(jnp.int32, (TQ, LANES), 1)
    first = lane < HEAD_DIM
    nt = (((1,), (1,)), ((), ()))
    cls = jnp.where(t == 0, 0, jnp.where(t == n_tiles - 1, 2, 1))
    pair = [[tbl_ref[(cls * Q_ROWS + r) * (W_ROWS // 2) + j] for j in range(W_ROWS // 2)] for r in range(Q_ROWS)]

    def operands(h):
        sl = slice((h // 2) * LANES, (h // 2 + 1) * LANES)
        return q[:, sl], kw[:, sl], vw[:, sl], kc[:, sl], vc[:, sl]

    def scores(h):
        qp, kp, _, kcp, _ = operands(h)
        qm = jnp.where(first if h % 2 == 0 else jnp.logical_not(first), qp, jnp.zeros_like(qp))
        bias = jnp.concatenate(
            [jnp.concatenate([pb_ref[h, pair[r][c]] for c in range(W_ROWS // 2)], axis=1)
             for r in range(Q_ROWS)], axis=0)
        s_nb = lax.dot_general(qm, kp, nt, preferred_element_type=F32) + bias
        s_cx = lax.dot_general(qm, kcp, nt, preferred_element_type=F32)
        return s_nb, s_cx

    def softmax(s_nb, s_cx):
        m = jnp.maximum(jnp.max(s_nb, axis=1, keepdims=True), jnp.max(s_cx, axis=1, keepdims=True))
        e_nb = jnp.exp2(s_nb - m)
        e_cx = jnp.exp2(s_cx - m)
        l = jnp.sum(e_nb, axis=1, keepdims=True) + jnp.sum(e_cx, axis=1, keepdims=True)
        return e_nb.astype(BF16), e_cx.astype(BF16), l

    def values(h, e_nb, e_cx, l):
        _, _, vp, _, vcp = operands(h)
        o = jnp.dot(e_nb, vp, preferred_element_type=F32) + jnp.dot(e_cx, vcp, preferred_element_type=F32)
        return o / l

    s, p, o = {}, {}, {}
    for step in range(N_HEADS_NA + 2):
        if step < N_HEADS_NA:
            s[step] = scores(step)
        if 0 <= step - 1 < N_HEADS_NA:
            p[step - 1] = softmax(*s.pop(step - 1))
        if 0 <= step - 2 < N_HEADS_NA:
            o[step - 2] = values(step - 2, *p.pop(step - 2))
    out = jnp.concatenate([jnp.where(first, o[h], o[h + 1]) for h in range(0, N_HEADS_NA, 2)], axis=1)
    o_ref[0] = _rms(out, ona_ref[...]).astype(BF16)


def _bias_blocks(rpb, rows):
    n_tiles = rows // Q_ROWS
    n_ro, n_co = 2 * NA_KH - 1, 2 * NA_KW - 1
    qc = np.arange(GRID_W)[:, None]
    kc = np.arange(GRID_W)[None, :]
    cs = np.clip(qc - NA_KW // 2, 0, GRID_W - NA_KW)
    col_ok = (kc >= cs) & (kc < cs + NA_KW)
    spread = ((kc - qc + NA_KW - 1)[None] == np.arange(n_co)[:, None, None]) & col_ok[None]
    blocks = jnp.dot(rpb.reshape(-1, n_co), jnp.asarray(spread.reshape(n_co, -1), F32), precision=HIGHEST)
    blocks = jnp.where(jnp.asarray(col_ok.reshape(-1)), blocks * LOG2E, NEG_INF)
    blocks = blocks.reshape(N_HEADS_NA, n_ro, GRID_W, GRID_W)
    masked = jnp.full((N_HEADS_NA, 1, GRID_W, GRID_W), NEG_INF, F32)
    ext = jnp.concatenate([blocks, masked], axis=1)
    pairs, table = [], []
    for t in (0, 1, n_tiles - 1):
        ws = int(np.clip(Q_ROWS * t - NA_KH // 2, 0, rows - W_ROWS))
        for rho in range(Q_ROWS):
            r = Q_ROWS * t + rho
            rs = int(np.clip(r - NA_KH // 2, 0, rows - NA_KH))
            off = [kr - r + NA_KH - 1 if rs <= kr < rs + NA_KH else n_ro for kr in range(ws, ws + W_ROWS)]
            for c in range(W_ROWS // 2):
                pr = (off[2 * c], off[2 * c + 1])
                if pr not in pairs:
                    pairs.append(pr)
                table.append(pairs.index(pr))
    pb = jnp.stack([jnp.concatenate([ext[:, a], ext[:, b]], axis=-1) for a, b in pairs], axis=1)
    return pb, jnp.asarray(np.asarray(table, np.int32))


def _attention(q, k, v, kc, vc, pb, table, out_norm_a):
    b, n, _ = q.shape
    rows = n // GRID_W
    n_tiles = rows // Q_ROWS
    per_b = lambda i, t, *_: (i, 0, 0)
    return pl.pallas_call(
        functools.partial(_attn_kernel, rows=rows),
        grid_spec=pltpu.PrefetchScalarGridSpec(
            num_scalar_prefetch=1,
            grid=(b, n_tiles),
            in_specs=[pl.BlockSpec((1, TQ, D_NA), lambda i, t, *_: (i, t, 0)),
                      pl.BlockSpec((1, n, D_NA), per_b),
                      pl.BlockSpec((1, n, D_NA), per_b),
                      pl.BlockSpec((1, CTX_LEN, D_NA), per_b),
                      pl.BlockSpec((1, CTX_LEN, D_NA), per_b),
                      pl.BlockSpec(pb.shape, lambda i, t, *_: (0, 0, 0, 0)),
                      pl.BlockSpec((1, D_NA), lambda i, t, *_: (0, 0))],
            out_specs=pl.BlockSpec((1, TQ, D_NA), lambda i, t, *_: (i, t, 0))),
        out_shape=jax.ShapeDtypeStruct((b, n, D_NA), BF16),
        compiler_params=_params("parallel", "arbitrary"),
        name="nbr_attention",
    )(table, q, k, v, kc, vc, pb, out_norm_a)


def _out_kernel(oa_ref, ob_ref, x_ref, mod_ref, w_ref, n2_ref, wr_ref, xn_ref, h2_ref, aff_ref, terms_ref, *,
                sub):
    mod = mod_ref[0]
    nt = (((1,), (1,)), ((), ()))
    w_hi, w_lo = _bf16_terms(wr_ref[...], 2)
    w2 = jnp.concatenate([w_hi, w_lo], axis=0)
    tiles = [slice(r, r + sub) for r in range(0, x_ref.shape[1], sub)]
    mixes = [jnp.dot(oa_ref[0, rs], w_ref[:D_NA], preferred_element_type=F32)
             + jnp.dot(ob_ref[0, rs], w_ref[D_NA:], preferred_element_type=F32) for rs in tiles]
    for rs, mix in zip(tiles, mixes):
        xn = x_ref[0, rs] + mod[2:3] * mix
        xn_ref[0, rs] = xn
        h2 = _rms_mod(xn, n2_ref[...], mod[3:4], mod[4:5])
        h_hi, h_lo = _bf16_terms(h2, 2)
        h2_ref[0, rs] = h_hi
        l_hi = lax.dot_general(w2, h_hi, nt, preferred_element_type=F32)
        l_lo = lax.dot_general(w_hi, h_lo, nt, preferred_element_type=F32)
        logits = l_hi[:N_EXPERTS] + l_hi[N_EXPERTS:] + l_lo
        e = jnp.exp(logits - jnp.max(logits, axis=0, keepdims=True))
        aff = e / jnp.sum(e, axis=0, keepdims=True)
        aff_ref[0, :, rs] = aff
        terms_ref[0, :, rs] = jnp.concatenate(_bf16_terms(aff, 3), axis=0)


def _out_proj(oa, ob, x, mod, w_out_b, norm2, w_router_t, tm=1024, sub=512):
    b, n, d = x.shape
    tile = lambda i, j: (i, j, 0)
    full = lambda i, j: (0, 0)
    return pl.pallas_call(
        functools.partial(_out_kernel, sub=sub),
        grid=(b, n // tm),
        in_specs=[pl.BlockSpec((1, tm, D_NA), tile),
                  pl.BlockSpec((1, tm, D_SG), tile),
                  pl.BlockSpec((1, tm, d), tile),
                  pl.BlockSpec((1, N_MOD, d), lambda i, j: (i, 0, 0)),
                  pl.BlockSpec(w_out_b.shape, full),
                  pl.BlockSpec((1, d), full),
                  pl.BlockSpec((N_EXPERTS, d), full)],
        out_specs=[pl.BlockSpec((1, tm, d), tile),
                   pl.BlockSpec((1, tm, d), tile),
                   pl.BlockSpec((1, N_EXPERTS, tm), lambda i, j: (i, 0, j)),
                   pl.BlockSpec((1, GATE_TERMS * N_EXPERTS, tm), lambda i, j: (i, 0, j))],
        out_shape=[jax.ShapeDtypeStruct((b, n, d), F32),
                   jax.ShapeDtypeStruct((b, n, d), BF16),
                   jax.ShapeDtypeStruct((b, N_EXPERTS, n), F32),
                   jax.ShapeDtypeStruct((b, GATE_TERMS * N_EXPERTS, n), BF16)],
        compiler_params=_params("parallel", "arbitrary"),
        name="out_proj_router",
    )(oa, ob, x, mod, w_out_b, norm2, w_router_t)


def _prefix_count(mask_f, tri):
    rows, n = mask_f.shape
    parts = []
    carry = jnp.zeros((rows, 1), F32)
    for j in range(n // LANES):
        blk = mask_f[:, j * LANES:(j + 1) * LANES]
        parts.append(jnp.dot(blk.astype(BF16), tri, preferred_element_type=F32) + carry)
        carry = carry + jnp.sum(blk, axis=1, keepdims=True)
    return jnp.concatenate(parts, axis=1)


def _topk_kernel(aff_ref, slot_ref, eb_ref, *, cap):
    a = aff_ref[...]
    rows = a.shape[0]

    def enough(t):
        return jnp.sum(jnp.where(a >= t, 1.0, 0.0), axis=1, keepdims=True) >= cap

    tiny = jnp.full((rows, 1), 2.0 ** -126, F32)
    normal = enough(tiny)
    pw = tiny
    hi = jnp.full((rows, 1), 4.0, F32)
    for bit in range(6, -1, -1):
        cand = pw * (2.0 ** (1 << bit))
        ok = enough(cand)
        pw = jnp.where(ok, cand, pw)
        hi = jnp.where(ok, hi, cand)
    lo = jnp.where(normal, pw, 0.0)
    hi = jnp.where(normal, hi, tiny)
    step = lo
    for _ in range(MANTISSA_STEPS):
        step = step * 0.5
        cand = lo + step
        ok = enough(cand)
        lo = jnp.where(ok, cand, lo)
        hi = jnp.where(ok, hi, cand)
    above = a >= hi
    tie = jnp.logical_and(a >= lo, jnp.logical_not(above))
    n_above = jnp.sum(jnp.where(above, 1.0, 0.0), axis=1, keepdims=True)
    ri = lax.broadcasted_iota(jnp.int32, (LANES, LANES), 0)
    ci = lax.broadcasted_iota(jnp.int32, (LANES, LANES), 1)
    tri = jnp.where(ri <= ci, 1.0, 0.0).astype(BF16)
    tie_rank = _prefix_count(jnp.where(tie, 1.0, 0.0), tri)
    sel = jnp.logical_or(above, jnp.logical_and(tie, tie_rank <= cap - n_above))
    sel_f = jnp.where(sel, 1.0, 0.0)
    pos = _prefix_count(sel_f, tri) - 1.0
    slot_ref[...] = jnp.where(sel, pos, -1.0).astype(jnp.int32)
    lane = lax.broadcasted_iota(jnp.int32, (rows, LANES), 1)
    cnt = jnp.zeros((rows, LANES), F32)
    for j in range(a.shape[1] // TCH):
        cnt = jnp.where(lane == j, jnp.sum(sel_f[:, j * TCH:(j + 1) * TCH], axis=1, keepdims=True), cnt)
    before = jnp.where(ri < ci, 1.0, 0.0).astype(BF16)
    eb_ref[...] = jnp.dot(cnt.astype(BF16), before, preferred_element_type=F32).astype(jnp.int32)


def _topk_slots(aff2, cap):
    rows, n = aff2.shape
    tr = rows
    return pl.pallas_call(
        functools.partial(_topk_kernel, cap=cap),
        grid=(rows // tr,),
        in_specs=[pl.BlockSpec((tr, n), lambda i: (i, 0))],
        out_specs=[pl.BlockSpec((tr, n), lambda i: (i, 0)), pl.BlockSpec((tr, LANES), lambda i: (i, 0))],
        out_shape=[jax.ShapeDtypeStruct((rows, n), jnp.int32), jax.ShapeDtypeStruct((rows, LANES), jnp.int32)],
        compiler_params=_params("parallel"),
        name="expert_topk",
    )(aff2)


def _slot_window(eb_ref, bi, e, j, n_e, n_chunks, cap):
    base = (bi * n_e + e) * (n_chunks + 1) + j
    s0 = eb_ref[base]
    s1 = eb_ref[base + 1]
    start = jnp.minimum(s0 & -SLOT_ALIGN, cap - SLOT_W)
    return s0, s1, pl.multiple_of(start, SLOT_ALIGN)


def _window_overflow(eb_ref, bi, j, n_e, n_chunks, cap):
    over = None
    for e in range(n_e):
        _, s1, start = _slot_window(eb_ref, bi, e, j, n_e, n_chunks, cap)
        o = s1 > start + SLOT_W
        over = o if over is None else jnp.logical_or(over, o)
    return over


def _gate_lanes(res, e, n_e):
    lane = lax.broadcasted_iota(jnp.int32, res.shape, 1)
    head = jnp.logical_or(lane == e, lane == n_e + e)
    t12 = jnp.sum(jnp.where(head, res, 0.0), axis=1, keepdims=True)
    t3 = jnp.sum(jnp.where(lane == 2 * n_e + e, res, 0.0), axis=1, keepdims=True)
    return t12 + t3


def _gather_kernel(eb_ref, h2_ref, slot_ref, ap_ref, xs_ref, gate_ref, *, cap, jc, n_chunks):
    bi = pl.program_id(0)
    jo = pl.program_id(1)
    n_e = slot_ref.shape[1]
    wi = lax.broadcasted_iota(jnp.int32, (SLOT_W, TCH), 0)
    wcol = lax.broadcasted_iota(jnp.int32, (SLOT_W, 1), 0)
    nt = (((1,), (1,)), ((), ()))

    @pl.when(jo == 0)
    def _():
        xs_ref[...] = jnp.zeros_like(xs_ref)
        gate_ref[...] = jnp.zeros_like(gate_ref)

    for jj in range(jc):
        j = jo * jc + jj
        tok = slice(jj * TCH, (jj + 1) * TCH)
        h2c = h2_ref[0, tok, :]
        apc = ap_ref[0, :, tok]
        for g in range(n_e // EXPERT_GROUP):
            blocks, meta = [], []
            for q in range(EXPERT_GROUP):
                e = EXPERT_GROUP * g + q
                s0, s1, start = _slot_window(eb_ref, bi, e, j, n_e, n_chunks, cap)
                blocks.append(jnp.where(slot_ref[0, e:e + 1, tok] - start == wi, 1.0, 0.0).astype(BF16))
                meta.append((e, s0, s1, start))
            onehot = jnp.concatenate(blocks, axis=0)
            rows = jnp.dot(onehot, h2c, preferred_element_type=F32).astype(BF16)
            aff3 = lax.dot_general(onehot, apc, nt, preferred_element_type=F32)
            for q, (e, s0, s1, start) in enumerate(meta):
                own = jnp.logical_and(wcol + start >= s0, wcol + start < s1)
                win = pl.ds(start, SLOT_W)
                blk = slice(q * SLOT_W, (q + 1) * SLOT_W)
                xs_ref[0, e, win, :] = jnp.where(own, rows[blk], xs_ref[0, e, win, :])
                gate_ref[0, e, win, :] = jnp.where(own, _gate_lanes(aff3[blk], e, n_e), gate_ref[0, e, win, :])

    for jj in range(jc):
        j = jo * jc + jj
        tok = slice(jj * TCH, (jj + 1) * TCH)

        @pl.when(_window_overflow(eb_ref, bi, j, n_e, n_chunks, cap))
        def _():
            h2c = h2_ref[0, tok, :]
            apc = ap_ref[0, :, tok]
            ci = lax.broadcasted_iota(jnp.int32, (cap, TCH), 0)
            ccol = lax.broadcasted_iota(jnp.int32, (cap, 1), 0)
            for e in range(n_e):
                s0, s1, _ = _slot_window(eb_ref, bi, e, j, n_e, n_chunks, cap)
                onehot = jnp.where(slot_ref[0, e:e + 1, tok] == ci, 1.0, 0.0).astype(BF16)
                rows = jnp.dot(onehot, h2c, preferred_element_type=F32).astype(BF16)
                aff3 = lax.dot_general(onehot, apc, nt, preferred_element_type=F32)
                own = jnp.logical_and(ccol >= s0, ccol < s1)
                xs_ref[0, e] = jnp.where(own, rows, xs_ref[0, e])
                gate_ref[0, e] = jnp.where(own, _gate_lanes(aff3, e, n_e), gate_ref[0, e])


def _gather_tokens(ebound, h2, slot, aff_terms, cap, jc=2):
    b, n, d = h2.shape
    e = slot.shape[1]
    n_chunks = n // TCH
    tok = lambda i, j, *_: (i, j, 0)
    per_b = lambda i, j, *_: (i, 0, 0, 0)
    return pl.pallas_call(
        functools.partial(_gather_kernel, cap=cap, jc=jc, n_chunks=n_chunks),
        grid_spec=pltpu.PrefetchScalarGridSpec(
            num_scalar_prefetch=1,
            grid=(b, n_chunks // jc),
            in_specs=[pl.BlockSpec((1, jc * TCH, d), tok),
                      pl.BlockSpec((1, e, jc * TCH), lambda i, j, *_: (i, 0, j)),
                      pl.BlockSpec((1, GATE_TERMS * e, jc * TCH), lambda i, j, *_: (i, 0, j))],
            out_specs=[pl.BlockSpec((1, e, cap, d), per_b),
                       pl.BlockSpec((1, e, cap, 1), per_b)]),
        out_shape=[jax.ShapeDtypeStruct((b, e, cap, d), BF16),
                   jax.ShapeDtypeStruct((b, e, cap, 1), F32)],
        compiler_params=_params("parallel", "arbitrary"),
        name="moe_gather",
    )(ebound, h2, slot, aff_terms)


def _expert_kernel(xs_ref, wg_ref, wu_ref, wd_ref, gate_ref, y_ref, acc_ref, *, rb):
    f = pl.program_id(1)

    @pl.when(f == 0)
    def _():
        acc_ref[...] = jnp.zeros_like(acc_ref)

    wg = wg_ref[0].astype(BF16)
    wu = wu_ref[0].astype(BF16)
    wd = wd_ref[0].astype(BF16)
    nb, _, cap, d = xs_ref.shape
    for i in range(nb // rb):
        xs = xs_ref[i * rb:(i + 1) * rb, 0].reshape(rb * cap, d)
        a = jnp.dot(xs, wg, preferred_element_type=F32)
        u = jnp.dot(xs, wu, preferred_element_type=F32)
        hm = (_silu(a) * u).astype(BF16)
        part = jnp.dot(hm, wd, preferred_element_type=F32).reshape(rb, cap, d)
        acc_ref[i * rb:(i + 1) * rb] += part

    @pl.when(f == pl.num_programs(1) - 1)
    def _():
        y_ref[:, 0] = (acc_ref[...] * gate_ref[:, 0]).astype(BF16)


def _experts(xs, w_gate, w_up, w_down, gate, fc=768, rb=4):
    b, e, cap, d = xs.shape
    dff = w_gate.shape[2]
    per_e = lambda i, f: (0, i, 0, 0)
    return pl.pallas_call(
        functools.partial(_expert_kernel, rb=rb),
        grid=(e, dff // fc),
        in_specs=[pl.BlockSpec((b, 1, cap, d), per_e),
                  pl.BlockSpec((1, d, fc), lambda i, f: (i, 0, f)),
                  pl.BlockSpec((1, d, fc), lambda i, f: (i, 0, f)),
                  pl.BlockSpec((1, fc, d), lambda i, f: (i, f, 0)),
                  pl.BlockSpec((b, 1, cap, 1), per_e)],
        out_specs=pl.BlockSpec((b, 1, cap, d), per_e),
        out_shape=jax.ShapeDtypeStruct((b, e, cap, d), BF16),
        scratch_shapes=[pltpu.VMEM((b, cap, d), F32)],
        compiler_params=_params("parallel", "arbitrary"),
        name="moe_experts",
    )(xs, w_gate, w_up, w_down, gate)


def _combine_kernel(eb_ref, slot_t_ref, y_ref, xn_ref, mod_ref, nf_ref, o_ref, *, cap, jc, n_chunks):
    bi = pl.program_id(0)
    jo = pl.program_id(1)
    n_e = y_ref.shape[1]
    ci = lax.broadcasted_iota(jnp.int32, (TCH, EXPERT_GROUP * SLOT_W), 1)
    toks = [slice(jj * TCH, (jj + 1) * TCH) for jj in range(jc)]

    def finish(tok, moe):
        x = xn_ref[0, tok] + mod_ref[0][5:6] * moe
        o_ref[0, tok] = _rms(x, nf_ref[...])

    moes = []
    for jj, tok in enumerate(toks):
        j = jo * jc + jj
        st = slot_t_ref[0, tok, :]
        s_blocks, y_blocks = [], []
        for g in range(n_e // EXPERT_GROUP):
            col = None
            for q in reversed(range(EXPERT_GROUP)):
                e = EXPERT_GROUP * g + q
                _, _, start = _slot_window(eb_ref, bi, e, j, n_e, n_chunks, cap)
                rel = st[:, e:e + 1] - start
                tgt = jnp.where(jnp.logical_and(rel >= 0, rel < SLOT_W), rel + q * SLOT_W, -1)
                col = tgt if col is None else jnp.where(ci < (q + 1) * SLOT_W, tgt, col)
                y_blocks.insert(g * EXPERT_GROUP, y_ref[0, e, pl.ds(start, SLOT_W), :])
            s_blocks.append(jnp.where(col == ci, 1.0, 0.0).astype(BF16))
        scat = jnp.concatenate(s_blocks, axis=1)
        ywin = jnp.concatenate(y_blocks, axis=0)
        moes.append(jnp.dot(scat, ywin, preferred_element_type=F32))
    for tok, moe in zip(toks, moes):
        finish(tok, moe)

    for jj, tok in enumerate(toks):
        @pl.when(_window_overflow(eb_ref, bi, jo * jc + jj, n_e, n_chunks, cap))
        def _():
            st = slot_t_ref[0, tok, :]
            cf = lax.broadcasted_iota(jnp.int32, (TCH, cap), 1)
            dense = jnp.concatenate(
                [jnp.where(st[:, e:e + 1] == cf, 1.0, 0.0).astype(BF16) for e in range(n_e)], axis=1)
            finish(tok, jnp.dot(dense, y_ref[0].reshape(n_e * cap, y_ref.shape[3]),
                                preferred_element_type=F32))


def _combine(ebound, slot_t, y, x_new, mod, norm_final, cap, jc=2):
    b, n, d = x_new.shape
    e = slot_t.shape[2]
    n_chunks = n // TCH
    tile = lambda i, j, *_: (i, j, 0)
    return pl.pallas_call(
        functools.partial(_combine_kernel, cap=cap, jc=jc, n_chunks=n_chunks),
        grid_spec=pltpu.PrefetchScalarGridSpec(
            num_scalar_prefetch=1,
            grid=(b, n_chunks // jc),
            in_specs=[pl.BlockSpec((1, jc * TCH, e), tile),
                      pl.BlockSpec((1, e, cap, d), lambda i, j, *_: (i, 0, 0, 0)),
                      pl.BlockSpec((1, jc * TCH, d), tile),
                      pl.BlockSpec((1, N_MOD, d), lambda i, j, *_: (i, 0, 0)),
                      pl.BlockSpec((1, d), lambda i, j, *_: (0, 0))],
            out_specs=pl.BlockSpec((1, jc * TCH, d), tile)),
        out_shape=jax.ShapeDtypeStruct((b, n, d), F32),
        compiler_params=_params("parallel", "arbitrary"),
        name="moe_combine_norm",
    )(ebound, slot_t, y, x_new, mod, norm_final)


def kernel(x, c, ctx, c_ctx, w_mod, b_mod, norm1, w_in, rpb, w_s, b_s, gmlp_norm, out_norm_a, out_norm_b,
           w_out, norm2, w_router, w_gate, w_up, w_down, norm_final):
    b, n, d = x.shape
    assert w_mod.shape[0] == 1, "single-layer stack only"
    assert n % (GRID_W * Q_ROWS) == 0 and n // GRID_W >= W_ROWS
    assert n % (2 * TCH) == 0 and N_EXPERTS % EXPERT_GROUP == 0
    cap = EC_CAPACITY_FACTOR * n // N_EXPERTS

    pad = (-(b + 1)) % 8
    cc = jnp.concatenate([c, c_ctx[None], jnp.zeros((pad, d), F32)], axis=0)
    m = _modulation(cc, w_mod[0], b_mod[0][None])
    mod = m.reshape(-1, N_MOD, d)

    w_in_b = w_in[0].astype(BF16)
    ws2 = w_s[0].astype(BF16).reshape(N_GROUPS_SG // 2, 2 * CHUNK, CHUNK)
    bs2 = jnp.broadcast_to(b_s[0].reshape(N_GROUPS_SG // 2, 2 * CHUNK, 1), (N_GROUPS_SG // 2, 2 * CHUNK, LANES))
    q, k, v, ob = _in_proj(x, mod, norm1, w_in_b, ws2, bs2, gmlp_norm, out_norm_b)

    kc, vc = _ctx_proj(ctx.reshape(b * CTX_LEN, d), mod, b, norm1, w_in_b)
    kc = kc.reshape(b, CTX_LEN, D_NA)
    vc = vc.reshape(b, CTX_LEN, D_NA)

    pb, table = _bias_blocks(rpb[0], n // GRID_W)
    oa = _attention(q, k, v, kc, vc, pb, table, out_norm_a)

    x_new, h2, aff, aff_terms = _out_proj(oa, ob, x, mod, w_out[0].astype(BF16), norm2, w_router[0].T)

    slot, ebound = _topk_slots(aff.reshape(b * N_EXPERTS, n), cap)
    slot = slot.reshape(b, N_EXPERTS, n)
    ebound = ebound[:, :n // TCH + 1].reshape(-1)

    xs, gate = _gather_tokens(ebound, h2, slot, aff_terms, cap)
    y = _experts(xs, w_gate[0], w_up[0], w_down[0], gate)
    return _combine(ebound, jnp.swapaxes(slot, 1, 2), y, x_new, mod, norm_final[None], cap)
```

```python
import functools

import numpy as np
import jax
import jax.numpy as jnp
from jax import lax
from jax.experimental import pallas as pl
from jax.experimental.pallas import tpu as pltpu

D_MODEL = 1024
GRID_W = 64
CTX_LEN = 256
N_HEADS_NA = 8
HEAD_DIM = 64
D_NA = N_HEADS_NA * HEAD_DIM
NA_KH = 8
NA_KW = 16
D_SG = D_MODEL - D_NA
N_GROUPS_SG = 8
SG_GROUP_DIM = D_SG // N_GROUPS_SG
CHUNK = 128
N_EXPERTS = 16
EC_CAPACITY_FACTOR = 2
D_EXPERT = 1536
N_MOD = 6
EPS = 1e-6
NEG_INF = -1e30
LOG2E = float(np.log2(np.e))

LANES = 128
VMEM_LIMIT = 56 * 1024 * 1024

F32 = jnp.float32
BF16 = jnp.bfloat16
HIGHEST = lax.Precision.HIGHEST

Q_ROWS = 4
W_ROWS = 12
TQ = Q_ROWS * GRID_W
TW = W_ROWS * GRID_W

MANTISSA_STEPS = 36

TCH = 256
SLOT_W = 64
SLOT_ALIGN = 16
EXPERT_GROUP = 4
GATE_TERMS = 3


def _params(*sem):
    return pltpu.CompilerParams(dimension_semantics=sem, vmem_limit_bytes=VMEM_LIMIT)


def _rms_mod(x, g, shift, scale):
    r = lax.rsqrt(jnp.mean(x * x, axis=-1, keepdims=True) + EPS)
    return (x * r) * g * (1.0 + scale) + shift


def _rms(x, g):
    return x * lax.rsqrt(jnp.mean(x * x, axis=-1, keepdims=True) + EPS) * g


def _gelu_tanh(x):
    return 0.5 * x * (1.0 + jnp.tanh(np.sqrt(2.0 / np.pi).astype(np.float32) * (x + 0.044715 * (x * x * x))))


def _silu(x):
    return x * jax.nn.sigmoid(x)


def _bf16_terms(x, n_terms):
    terms = []
    for _ in range(n_terms):
        t = x.astype(BF16)
        terms.append(t)
        x = x - t.astype(F32)
    return terms


def _mod_kernel(c_ref, w_ref, b_ref, o_ref):
    rows = c_ref.shape[0]
    s = jnp.concatenate(_bf16_terms(_silu(c_ref[...]), 3), axis=0)
    w_hi, w_lo = _bf16_terms(w_ref[...], 2)
    hi = jnp.dot(s, w_hi, preferred_element_type=F32)
    lo = jnp.dot(s[:2 * rows], w_lo, preferred_element_type=F32)
    small = (hi[2 * rows:] + lo[rows:]) + (hi[rows:2 * rows] + lo[:rows])
    o_ref[...] = (small + hi[:rows]) + b_ref[...]


def _modulation(cc, w_mod, b_mod):
    rows, d = cc.shape
    n = w_mod.shape[1]
    tn = 1024
    return pl.pallas_call(
        _mod_kernel,
        grid=(n // tn,),
        in_specs=[pl.BlockSpec((rows, d), lambda j: (0, 0)),
                  pl.BlockSpec((d, tn), lambda j: (0, j)),
                  pl.BlockSpec((1, tn), lambda j: (0, j))],
        out_specs=pl.BlockSpec((rows, tn), lambda j: (0, j)),
        out_shape=jax.ShapeDtypeStruct((rows, n), F32),
        compiler_params=_params("arbitrary"),
        name="modulation",
    )(cc, w_mod, b_mod)


def _in_kernel(x_ref, mod_ref, n1_ref, w_ref, ws_ref, bs_ref, gn_ref, onb_ref,
               q_ref, k_ref, v_ref, ob_ref, sp_ref, *, sub):
    mod = mod_ref[0]
    tiles = [slice(r, r + sub) for r in range(0, x_ref.shape[1], sub)]
    uzs = []
    for rs in tiles:
        hb = _rms_mod(x_ref[0, rs], n1_ref[...], mod[0:1], mod[1:2]).astype(BF16)
        qkv = jnp.dot(hb, w_ref[:, :3 * D_NA], preferred_element_type=F32)
        q_ref[0, rs] = (qkv[:, :D_NA] * (HEAD_DIM ** -0.5 * LOG2E)).astype(BF16)
        k_ref[0, rs] = qkv[:, D_NA:2 * D_NA].astype(BF16)
        v_ref[0, rs] = qkv[:, 2 * D_NA:].astype(BF16)
        uzs.append(jnp.dot(hb, w_ref[:, 3 * D_NA:], preferred_element_type=F32))

    nch = sub // CHUNK
    lane = lax.broadcasted_iota(jnp.int32, (CHUNK, LANES), 1)
    first = lane < SG_GROUP_DIM
    for rs, uz in zip(tiles, uzs):
        u = _gelu_tanh(uz[:, :D_SG])
        z = _gelu_tanh(uz[:, D_SG:])
        mu = jnp.mean(z, axis=-1, keepdims=True)
        zc = z - mu
        var = jnp.mean(zc * zc, axis=-1, keepdims=True)
        zb = (zc * lax.rsqrt(var + EPS) * gn_ref[...]).astype(BF16)
        for p in range(N_GROUPS_SG // 2):
            zp = jnp.concatenate(
                [zb[c * CHUNK:(c + 1) * CHUNK, p * LANES:(p + 1) * LANES] for c in range(nch)], axis=1)
            r = jnp.dot(ws_ref[p], zp, preferred_element_type=F32)
            bs = bs_ref[p]
            for c in range(nch):
                top = r[:CHUNK, c * LANES:(c + 1) * LANES] + bs[:CHUNK]
                bot = r[CHUNK:, c * LANES:(c + 1) * LANES] + bs[CHUNK:]
                sp_ref[rs.start + c * CHUNK:rs.start + (c + 1) * CHUNK, p * LANES:(p + 1) * LANES] = (
                    jnp.where(first, top, bot))
        ob = u * sp_ref[rs, :]
        ob_ref[0, rs] = _rms(ob, onb_ref[...]).astype(BF16)


def _in_proj(x, mod, norm1, w_in_b, ws2, bs2, gmlp_norm, out_norm_b, tm=1024, sub=512):
    b, n, d = x.shape
    d_in = w_in_b.shape[1]
    full2 = lambda i, j: (0, 0)
    full3 = lambda i, j: (0, 0, 0)
    tile = lambda i, j: (i, j, 0)
    act = jax.ShapeDtypeStruct((b, n, D_NA), BF16)
    return pl.pallas_call(
        functools.partial(_in_kernel, sub=sub),
        grid=(b, n // tm),
        in_specs=[pl.BlockSpec((1, tm, d), tile),
                  pl.BlockSpec((1, N_MOD, d), lambda i, j: (i, 0, 0)),
                  pl.BlockSpec((1, d), full2),
                  pl.BlockSpec((d, d_in), full2),
                  pl.BlockSpec(ws2.shape, full3),
                  pl.BlockSpec(bs2.shape, full3),
                  pl.BlockSpec((1, D_SG), full2),
                  pl.BlockSpec((1, D_SG), full2)],
        out_specs=[pl.BlockSpec((1, tm, D_NA), tile)] * 4,
        out_shape=[act] * 4,
        scratch_shapes=[pltpu.VMEM((tm, D_SG), F32)],
        compiler_params=_params("parallel", "arbitrary"),
        name="in_proj_gmlp",
    )(x, mod, norm1, w_in_b, ws2, bs2, gmlp_norm, out_norm_b)


def _ctx_kernel(x_ref, mod_ref, n1_ref, wk_ref, wv_ref, k_ref, v_ref):
    mod = mod_ref[0]
    hb = _rms_mod(x_ref[...], n1_ref[...], mod[0:1], mod[1:2]).astype(BF16)
    k_ref[...] = jnp.dot(hb, wk_ref[...], preferred_element_type=F32).astype(BF16)
    v_ref[...] = jnp.dot(hb, wv_ref[...], preferred_element_type=F32).astype(BF16)


def _ctx_proj(ctx2, mod, mod_row, norm1, w_in_b, tm=512):
    rows, d = ctx2.shape
    act = jax.ShapeDtypeStruct((rows, D_NA), BF16)
    return pl.pallas_call(
        _ctx_kernel,
        grid=(rows // tm,),
        in_specs=[pl.BlockSpec((tm, d), lambda i: (i, 0)),
                  pl.BlockSpec((1, N_MOD, d), lambda i: (mod_row, 0, 0)),
                  pl.BlockSpec((1, d), lambda i: (0, 0)),
                  pl.BlockSpec((d, D_NA), lambda i: (0, 1)),
                  pl.BlockSpec((d, D_NA), lambda i: (0, 2))],
        out_specs=[pl.BlockSpec((tm, D_NA), lambda i: (i, 0))] * 2,
        out_shape=[act, act],
        compiler_params=_params("arbitrary"),
        name="ctx_kv_proj",
    )(ctx2, mod, norm1, w_in_b, w_in_b)


def _window_start(t, rows):
    return jnp.clip(Q_ROWS * t - NA_KH // 2, 0, rows - W_ROWS)


def _attn_kernel(tbl_ref, q_ref, k_ref, v_ref, kc_ref, vc_ref, pb_ref, ona_ref, o_ref, *, rows):
    t = pl.program_id(1)
    n_tiles = rows // Q_ROWS
    ws = pl.multiple_of(_window_start(t, rows) * GRID_W, GRID_W)
    kw = k_ref[0, pl.ds(ws, TW), :]
    vw = v_ref[0, pl.ds(ws, TW), :]
    q = q_ref[0]
    kc = kc_ref[0]
    vc = vc_ref[0]
    lane = lax.broadcasted_iota(jnp.int32, (TQ, LANES), 1)
    first = lane < HEAD_DIM
    nt = (((1,), (1,)), ((), ()))
    cls = jnp.where(t == 0, 0, jnp.where(t == n_tiles - 1, 2, 1))
    pair = [[tbl_ref[(cls * Q_ROWS + r) * (W_ROWS // 2) + j] for j in range(W_ROWS // 2)] for r in range(Q_ROWS)]

    def operands(h):
        sl = slice((h // 2) * LANES, (h // 2 + 1) * LANES)
        return q[:, sl], kw[:, sl], vw[:, sl], kc[:, sl], vc[:, sl]

    def scores(h):
        qp, kp, _, kcp, _ = operands(h)
        qm = jnp.where(first if h % 2 == 0 else jnp.logical_not(first), qp, jnp.zeros_like(qp))
        bias = jnp.concatenate(
            [jnp.concatenate([pb_ref[h, pair[r][c]] for c in range(W_ROWS // 2)], axis=1)
             for r in range(Q_ROWS)], axis=0)
        s_nb = lax.dot_general(qm, kp, nt, preferred_element_type=F32) + bias
        s_cx = lax.dot_general(qm, kcp, nt, preferred_element_type=F32)
        return s_nb, s_cx

    def softmax(s_nb, s_cx):
        m = jnp.maximum(jnp.max(s_nb, axis=1, keepdims=True), jnp.max(s_cx, axis=1, keepdims=True))
        e_nb = jnp.exp2(s_nb - m)
        e_cx = jnp.exp2(s_cx - m)
        l = jnp.sum(e_nb, axis=1, keepdims=True) + jnp.sum(e_cx, axis=1, keepdims=True)
        return e_nb.astype(BF16), e_cx.astype(BF16), l

    def values(h, e_nb, e_cx, l):
        _, _, vp, _, vcp = operands(h)
        o = jnp.dot(e_nb, vp, preferred_element_type=F32) + jnp.dot(e_cx, vcp, preferred_element_type=F32)
        return o / l

    s, p, o = {}, {}, {}
    for step in range(N_HEADS_NA + 2):
        if step < N_HEADS_NA:
            s[step] = scores(step)
        if 0 <= step - 1 < N_HEADS_NA:
            p[step - 1] = softmax(*s.pop(step - 1))
        if 0 <= step - 2 < N_HEADS_NA:
            o[step - 2] = values(step - 2, *p.pop(step - 2))
    out = jnp.concatenate([jnp.where(first, o[h], o[h + 1]) for h in range(0, N_HEADS_NA, 2)], axis=1)
    o_ref[0] = _rms(out, ona_ref[...]).astype(BF16)


def _bias_blocks(rpb, rows):
    n_tiles = rows // Q_ROWS
    n_ro, n_co = 2 * NA_KH - 1, 2 * NA_KW - 1
    qc = np.arange(GRID_W)[:, None]
    kc = np.arange(GRID_W)[None, :]
    cs = np.clip(qc - NA_KW // 2, 0, GRID_W - NA_KW)
    col_ok = (kc >= cs) & (kc < cs + NA_KW)
    spread = ((kc - qc + NA_KW - 1)[None] == np.arange(n_co)[:, None, None]) & col_ok[None]
    blocks = jnp.dot(rpb.reshape(-1, n_co), jnp.asarray(spread.reshape(n_co, -1), F32), precision=HIGHEST)
    blocks = jnp.where(jnp.asarray(col_ok.reshape(-1)), blocks * LOG2E, NEG_INF)
    blocks = blocks.reshape(N_HEADS_NA, n_ro, GRID_W, GRID_W)
    masked = jnp.full((N_HEADS_NA, 1, GRID_W, GRID_W), NEG_INF, F32)
    ext = jnp.concatenate([blocks, masked], axis=1)
    pairs, table = [], []
    for t in (0, 1, n_tiles - 1):
        ws = int(np.clip(Q_ROWS * t - NA_KH // 2, 0, rows - W_ROWS))
        for rho in range(Q_ROWS):
            r = Q_ROWS * t + rho
            rs = int(np.clip(r - NA_KH // 2, 0, rows - NA_KH))
            off = [kr - r + NA_KH - 1 if rs <= kr < rs + NA_KH else n_ro for kr in range(ws, ws + W_ROWS)]
            for c in range(W_ROWS // 2):
                pr = (off[2 * c], off[2 * c + 1])
                if pr not in pairs:
                    pairs.append(pr)
                table.append(pairs.index(pr))
    pb = jnp.stack([jnp.concatenate([ext[:, a], ext[:, b]], axis=-1) for a, b in pairs], axis=1)
    return pb, jnp.asarray(np.asarray(table, np.int32))


def _attention(q, k, v, kc, vc, pb, table, out_norm_a):
    b, n, _ = q.shape
    rows = n // GRID_W
    n_tiles = rows // Q_ROWS
    per_b = lambda i, t, *_: (i, 0, 0)
    return pl.pallas_call(
        functools.partial(_attn_kernel, rows=rows),
        grid_spec=pltpu.PrefetchScalarGridSpec(
            num_scalar_prefetch=1,
            grid=(b, n_tiles),
            in_specs=[pl.BlockSpec((1, TQ, D_NA), lambda i, t, *_: (i, t, 0)),
                      pl.BlockSpec((1, n, D_NA), per_b),
                      pl.BlockSpec((1, n, D_NA), per_b),
                      pl.BlockSpec((1, CTX_LEN, D_NA), per_b),
                      pl.BlockSpec((1, CTX_LEN, D_NA), per_b),
                      pl.BlockSpec(pb.shape, lambda i, t, *_: (0, 0, 0, 0)),
                      pl.BlockSpec((1, D_NA), lambda i, t, *_: (0, 0))],
            out_specs=pl.BlockSpec((1, TQ, D_NA), lambda i, t, *_: (i, t, 0))),
        out_shape=jax.ShapeDtypeStruct((b, n, D_NA), BF16),
        compiler_params=_params("parallel", "arbitrary"),
        name="nbr_attention",
    )(table, q, k, v, kc, vc, pb, out_norm_a)


def _out_kernel(oa_ref, ob_ref, x_ref, mod_ref, w_ref, n2_ref, wr_ref, xn_ref, h2_ref, aff_ref, terms_ref, *,
                sub):
    mod = mod_ref[0]
    nt = (((1,), (1,)), ((), ()))
    w_hi, w_lo = _bf16_terms(wr_ref[...], 2)
    w2 = jnp.concatenate([w_hi, w_lo], axis=0)
    tiles = [slice(r, r + sub) for r in range(0, x_ref.shape[1], sub)]
    mixes = [jnp.dot(oa_ref[0, rs], w_ref[:D_NA], preferred_element_type=F32)
             + jnp.dot(ob_ref[0, rs], w_ref[D_NA:], preferred_element_type=F32) for rs in tiles]
    for rs, mix in zip(tiles, mixes):
        xn = x_ref[0, rs] + mod[2:3] * mix
        xn_ref[0, rs] = xn
        h2 = _rms_mod(xn, n2_ref[...], mod[3:4], mod[4:5])
        h_hi, h_lo = _bf16_terms(h2, 2)
        h2_ref[0, rs] = h_hi
        l_hi = lax.dot_general(w2, h_hi, nt, preferred_element_type=F32)
        l_lo = lax.dot_general(w_hi, h_lo, nt, preferred_element_type=F32)
        logits = l_hi[:N_EXPERTS] + l_hi[N_EXPERTS:] + l_lo
        e = jnp.exp(logits - jnp.max(logits, axis=0, keepdims=True))
        aff = e / jnp.sum(e, axis=0, keepdims=True)
        aff_ref[0, :, rs] = aff
        terms_ref[0, :, rs] = jnp.concatenate(_bf16_terms(aff, 3), axis=0)


def _out_proj(oa, ob, x, mod, w_out_b, norm2, w_router_t, tm=1024, sub=512):
    b, n, d = x.shape
    tile = lambda i, j: (i, j, 0)
    full = lambda i, j: (0, 0)
    return pl.pallas_call(
        functools.partial(_out_kernel, sub=sub),
        grid=(b, n // tm),
        in_specs=[pl.BlockSpec((1, tm, D_NA), tile),
                  pl.BlockSpec((1, tm, D_SG), tile),
                  pl.BlockSpec((1, tm, d), tile),
                  pl.BlockSpec((1, N_MOD, d), lambda i, j: (i, 0, 0)),
                  pl.BlockSpec(w_out_b.shape, full),
                  pl.BlockSpec((1, d), full),
                  pl.BlockSpec((N_EXPERTS, d), full)],
        out_specs=[pl.BlockSpec((1, tm, d), tile),
                   pl.BlockSpec((1, tm, d), tile),
                   pl.BlockSpec((1, N_EXPERTS, tm), lambda i, j: (i, 0, j)),
                   pl.BlockSpec((1, GATE_TERMS * N_EXPERTS, tm), lambda i, j: (i, 0, j))],
        out_shape=[jax.ShapeDtypeStruct((b, n, d), F32),
                   jax.ShapeDtypeStruct((b, n, d), BF16),
                   jax.ShapeDtypeStruct((b, N_EXPERTS, n), F32),
                   jax.ShapeDtypeStruct((b, GATE_TERMS * N_EXPERTS, n), BF16)],
        compiler_params=_params("parallel", "arbitrary"),
        name="out_proj_router",
    )(oa, ob, x, mod, w_out_b, norm2, w_router_t)


def _prefix_count(mask_f, tri):
    rows, n = mask_f.shape
    parts = []
    carry = jnp.zeros((rows, 1), F32)
    for j in range(n // LANES):
        blk = mask_f[:, j * LANES:(j + 1) * LANES]
        parts.append(jnp.dot(blk.astype(BF16), tri, preferred_element_type=F32) + carry)
        carry = carry + jnp.sum(blk, axis=1, keepdims=True)
    return jnp.concatenate(parts, axis=1)


def _topk_kernel(aff_ref, slot_ref, eb_ref, *, cap):
    a = aff_ref[...]
    rows = a.shape[0]

    def enough(t):
        return jnp.sum(jnp.where(a >= t, 1.0, 0.0), axis=1, keepdims=True) >= cap

    tiny = jnp.full((rows, 1), 2.0 ** -126, F32)
    normal = enough(tiny)
    pw = tiny
    hi = jnp.full((rows, 1), 4.0, F32)
    for bit in range(6, -1, -1):
        cand = pw * (2.0 ** (1 << bit))
        ok = enough(cand)
        pw = jnp.where(ok, cand, pw)
        hi = jnp.where(ok, hi, cand)
    lo = jnp.where(normal, pw, 0.0)
    hi = jnp.where(normal, hi, tiny)
    step = lo
    for _ in range(MANTISSA_STEPS):
        step = step * 0.5
        cand = lo + step
        ok = enough(cand)
        lo = jnp.where(ok, cand, lo)
        hi = jnp.where(ok, hi, cand)
    above = a >= hi
    tie = jnp.logical_and(a >= lo, jnp.logical_not(above))
    n_above = jnp.sum(jnp.where(above, 1.0, 0.0), axis=1, keepdims=True)
    ri = lax.broadcasted_iota(jnp.int32, (LANES, LANES), 0)
    ci = lax.broadcasted_iota(jnp.int32, (LANES, LANES), 1)
    tri = jnp.where(ri <= ci, 1.0, 0.0).astype(BF16)
    tie_rank = _prefix_count(jnp.where(tie, 1.0, 0.0), tri)
    sel = jnp.logical_or(above, jnp.logical_and(tie, tie_rank <= cap - n_above))
    sel_f = jnp.where(sel, 1.0, 0.0)
    pos = _prefix_count(sel_f, tri) - 1.0
    slot_ref[...] = jnp.where(sel, pos, -1.0).astype(jnp.int32)
    lane = lax.broadcasted_iota(jnp.int32, (rows, LANES), 1)
    cnt = jnp.zeros((rows, LANES), F32)
    for j in range(a.shape[1] // TCH):
        cnt = jnp.where(lane == j, jnp.sum(sel_f[:, j * TCH:(j + 1) * TCH], axis=1, keepdims=True), cnt)
    before = jnp.where(ri < ci, 1.0, 0.0).astype(BF16)
    eb_ref[...] = jnp.dot(cnt.astype(BF16), before, preferred_element_type=F32).astype(jnp.int32)


def _topk_slots(aff2, cap):
    rows, n = aff2.shape
    tr = rows
    return pl.pallas_call(
        functools.partial(_topk_kernel, cap=cap),
        grid=(rows // tr,),
        in_specs=[pl.BlockSpec((tr, n), lambda i: (i, 0))],
        out_specs=[pl.BlockSpec((tr, n), lambda i: (i, 0)), pl.BlockSpec((tr, LANES), lambda i: (i, 0))],
        out_shape=[jax.ShapeDtypeStruct((rows, n), jnp.int32), jax.ShapeDtypeStruct((rows, LANES), jnp.int32)],
        compiler_params=_params("parallel"),
        name="expert_topk",
    )(aff2)


def _slot_window(eb_ref, bi, e, j, n_e, n_chunks, cap):
    base = (bi * n_e + e) * (n_chunks + 1) + j
    s0 = eb_ref[base]
    s1 = eb_ref[base + 1]
    start = jnp.minimum(s0 & -SLOT_ALIGN, cap - SLOT_W)
    return s0, s1, pl.multiple_of(start, SLOT_ALIGN)


def _window_overflow(eb_ref, bi, j, n_e, n_chunks, cap):
    over = None
    for e in range(n_e):
        _, s1, start = _slot_window(eb_ref, bi, e, j, n_e, n_chunks, cap)
        o = s1 > start + SLOT_W
        over = o if over is None else jnp.logical_or(over, o)
    return over


def _gate_lanes(res, e, n_e):
    lane = lax.broadcasted_iota(jnp.int32, res.shape, 1)
    head = jnp.logical_or(lane == e, lane == n_e + e)
    t12 = jnp.sum(jnp.where(head, res, 0.0), axis=1, keepdims=True)
    t3 = jnp.sum(jnp.where(lane == 2 * n_e + e, res, 0.0), axis=1, keepdims=True)
    return t12 + t3


def _gather_kernel(eb_ref, h2_ref, slot_ref, ap_ref, xs_ref, gate_ref, *, cap, jc, n_chunks):
    bi = pl.program_id(0)
    jo = pl.program_id(1)
    n_e = slot_ref.shape[1]
    wi = lax.broadcasted_iota(jnp.int32, (SLOT_W, TCH), 0)
    wcol = lax.broadcasted_iota(jnp.int32, (SLOT_W, 1), 0)
    nt = (((1,), (1,)), ((), ()))

    @pl.when(jo == 0)
    def _():
        xs_ref[...] = jnp.zeros_like(xs_ref)
        gate_ref[...] = jnp.zeros_like(gate_ref)

    for jj in range(jc):
        j = jo * jc + jj
        tok = slice(jj * TCH, (jj + 1) * TCH)
        h2c = h2_ref[0, tok, :]
        apc = ap_ref[0, :, tok]
        for g in range(n_e // EXPERT_GROUP):
            blocks, meta = [], []
            for q in range(EXPERT_GROUP):
                e = EXPERT_GROUP * g + q
                s0, s1, start = _slot_window(eb_ref, bi, e, j, n_e, n_chunks, cap)
                blocks.append(jnp.where(slot_ref[0, e:e + 1, tok] - start == wi, 1.0, 0.0).astype(BF16))
                meta.append((e, s0, s1, start))
            onehot = jnp.concatenate(blocks, axis=0)
            rows = jnp.dot(onehot, h2c, preferred_element_type=F32).astype(BF16)
            aff3 = lax.dot_general(onehot, apc, nt, preferred_element_type=F32)
            for q, (e, s0, s1, start) in enumerate(meta):
                own = jnp.logical_and(wcol + start >= s0, wcol + start < s1)
                win = pl.ds(start, SLOT_W)
                blk = slice(q * SLOT_W, (q + 1) * SLOT_W)
                xs_ref[0, e, win, :] = jnp.where(own, rows[blk], xs_ref[0, e, win, :])
                gate_ref[0, e, win, :] = jnp.where(own, _gate_lanes(aff3[blk], e, n_e), gate_ref[0, e, win, :])

    for jj in range(jc):
        j = jo * jc + jj
        tok = slice(jj * TCH, (jj + 1) * TCH)

        @pl.when(_window_overflow(eb_ref, bi, j, n_e, n_chunks, cap))
        def _():
            h2c = h2_ref[0, tok, :]
            apc = ap_ref[0, :, tok]
            ci = lax.broadcasted_iota(jnp.int32, (cap, TCH), 0)
            ccol = lax.broadcasted_iota(jnp.int32, (cap, 1), 0)
            for e in range(n_e):
                s0, s1, _ = _slot_window(eb_ref, bi, e, j, n_e, n_chunks, cap)
                onehot = jnp.where(slot_ref[0, e:e + 1, tok] == ci, 1.0, 0.0).astype(BF16)
                rows = jnp.dot(onehot, h2c, preferred_element_type=F32).astype(BF16)
                aff3 = lax.dot_general(onehot, apc, nt, preferred_element_type=F32)
                own = jnp.logical_and(ccol >= s0, ccol < s1)
                xs_ref[0, e] = jnp.where(own, rows, xs_ref[0, e])
                gate_ref[0, e] = jnp.where(own, _gate_lanes(aff3, e, n_e), gate_ref[0, e])


def _gather_tokens(ebound, h2, slot, aff_terms, cap, jc=2):
    b, n, d = h2.shape
    e = slot.shape[1]
    n_chunks = n // TCH
    tok = lambda i, j, *_: (i, j, 0)
    per_b = lambda i, j, *_: (i, 0, 0, 0)
    return pl.pallas_call(
        functools.partial(_gather_kernel, cap=cap, jc=jc, n_chunks=n_chunks),
        grid_spec=pltpu.PrefetchScalarGridSpec(
            num_scalar_prefetch=1,
            grid=(b, n_chunks // jc),
            in_specs=[pl.BlockSpec((1, jc * TCH, d), tok),
                      pl.BlockSpec((1, e, jc * TCH), lambda i, j, *_: (i, 0, j)),
                      pl.BlockSpec((1, GATE_TERMS * e, jc * TCH), lambda i, j, *_: (i, 0, j))],
            out_specs=[pl.BlockSpec((1, e, cap, d), per_b),
                       pl.BlockSpec((1, e, cap, 1), per_b)]),
        out_shape=[jax.ShapeDtypeStruct((b, e, cap, d), BF16),
                   jax.ShapeDtypeStruct((b, e, cap, 1), F32)],
        compiler_params=_params("parallel", "arbitrary"),
        name="moe_gather",
    )(ebound, h2, slot, aff_terms)


def _expert_kernel(xs_ref, wg_ref, wu_ref, wd_ref, gate_ref, y_ref, acc_ref, *, rb, n_f):
    f = pl.program_id(1)
    nb, _, cap, d = xs_ref.shape

    def step(first, last):
        wg = wg_ref[0].astype(BF16)
        wu = wu_ref[0].astype(BF16)
        wd = wd_ref[0].astype(BF16)
        for i in range(nb // rb):
            blk = slice(i * rb, (i + 1) * rb)
            xs = xs_ref[blk, 0].reshape(rb * cap, d)
            a = jnp.dot(xs, wg, preferred_element_type=F32)
            u = jnp.dot(xs, wu, preferred_element_type=F32)
            hm = (_silu(a) * u).astype(BF16)
            part = jnp.dot(hm, wd, preferred_element_type=F32).reshape(rb, cap, d)
            total = part if first else acc_ref[blk] + part
            if last:
                y_ref[blk, 0] = (total * gate_ref[blk, 0]).astype(BF16)
            else:
                acc_ref[blk] = total

    if n_f == 1:
        step(True, True)
    else:
        pl.when(f == 0)(lambda: step(True, False))
        if n_f > 2:
            pl.when(jnp.logical_and(f > 0, f < n_f - 1))(lambda: step(False, False))
        pl.when(f == n_f - 1)(lambda: step(False, True))


def _experts(xs, w_gate, w_up, w_down, gate, fc=768, rb=4):
    b, e, cap, d = xs.shape
    dff = w_gate.shape[2]
    per_e = lambda i, f: (0, i, 0, 0)
    return pl.pallas_call(
        functools.partial(_expert_kernel, rb=rb, n_f=dff // fc),
        grid=(e, dff // fc),
        in_specs=[pl.BlockSpec((b, 1, cap, d), per_e),
                  pl.BlockSpec((1, d, fc), lambda i, f: (i, 0, f)),
                  pl.BlockSpec((1, d, fc), lambda i, f: (i, 0, f)),
                  pl.BlockSpec((1, fc, d), lambda i, f: (i, f, 0)),
                  pl.BlockSpec((b, 1, cap, 1), per_e)],
        out_specs=pl.BlockSpec((b, 1, cap, d), per_e),
        out_shape=jax.ShapeDtypeStruct((b, e, cap, d), BF16),
        scratch_shapes=[pltpu.VMEM((b, cap, d), F32)],
        compiler_params=_params("parallel", "arbitrary"),
        name="moe_experts",
    )(xs, w_gate, w_up, w_down, gate)


def _combine_kernel(eb_ref, slot_t_ref, y_ref, xn_ref, mod_ref, nf_ref, o_ref, *, cap, jc, n_chunks):
    bi = pl.program_id(0)
    jo = pl.program_id(1)
    n_e = y_ref.shape[1]
    ci = lax.broadcasted_iota(jnp.int32, (TCH, EXPERT_GROUP * SLOT_W), 1)
    toks = [slice(jj * TCH, (jj + 1) * TCH) for jj in range(jc)]

    def finish(tok, moe):
        x = xn_ref[0, tok] + mod_ref[0][5:6] * moe
        o_ref[0, tok] = _rms(x, nf_ref[...])

    moes = []
    for jj, tok in enumerate(toks):
        j = jo * jc + jj
        st = slot_t_ref[0, tok, :]
        s_blocks, y_blocks = [], []
        for g in range(n_e // EXPERT_GROUP):
            col = None
            for q in reversed(range(EXPERT_GROUP)):
                e = EXPERT_GROUP * g + q
                _, _, start = _slot_window(eb_ref, bi, e, j, n_e, n_chunks, cap)
                rel = st[:, e:e + 1] - start
                tgt = jnp.where(jnp.logical_and(rel >= 0, rel < SLOT_W), rel + q * SLOT_W, -1)
                col = tgt if col is None else jnp.where(ci < (q + 1) * SLOT_W, tgt, col)
                y_blocks.insert(g * EXPERT_GROUP, y_ref[0, e, pl.ds(start, SLOT_W), :])
            s_blocks.append(jnp.where(col == ci, 1.0, 0.0).astype(BF16))
        scat = jnp.concatenate(s_blocks, axis=1)
        ywin = jnp.concatenate(y_blocks, axis=0)
        moes.append(jnp.dot(scat, ywin, preferred_element_type=F32))
    for tok, moe in zip(toks, moes):
        finish(tok, moe)

    for jj, tok in enumerate(toks):
        @pl.when(_window_overflow(eb_ref, bi, jo * jc + jj, n_e, n_chunks, cap))
        def _():
            st = slot_t_ref[0, tok, :]
            cf = lax.broadcasted_iota(jnp.int32, (TCH, cap), 1)
            dense = jnp.concatenate(
                [jnp.where(st[:, e:e + 1] == cf, 1.0, 0.0).astype(BF16) for e in range(n_e)], axis=1)
            finish(tok, jnp.dot(dense, y_ref[0].reshape(n_e * cap, y_ref.shape[3]),
                                preferred_element_type=F32))


def _combine(ebound, slot_t, y, x_new, mod, norm_final, cap, jc=2):
    b, n, d = x_new.shape
    e = slot_t.shape[2]
    n_chunks = n // TCH
    tile = lambda i, j, *_: (i, j, 0)
    return pl.pallas_call(
        functools.partial(_combine_kernel, cap=cap, jc=jc, n_chunks=n_chunks),
        grid_spec=pltpu.PrefetchScalarGridSpec(
            num_scalar_prefetch=1,
            grid=(b, n_chunks // jc),
            in_specs=[pl.BlockSpec((1, jc * TCH, e), tile),
                      pl.BlockSpec((1, e, cap, d), lambda i, j, *_: (i, 0, 0, 0)),
                      pl.BlockSpec((1, jc * TCH, d), tile),
                      pl.BlockSpec((1, N_MOD, d), lambda i, j, *_: (i, 0, 0)),
                      pl.BlockSpec((1, d), lambda i, j, *_: (0, 0))],
            out_specs=pl.BlockSpec((1, jc * TCH, d), tile)),
        out_shape=jax.ShapeDtypeStruct((b, n, d), F32),
        compiler_params=_params("parallel", "arbitrary"),
        name="moe_combine_norm",
    )(ebound, slot_t, y, x_new, mod, norm_final)


def kernel(x, c, ctx, c_ctx, w_mod, b_mod, norm1, w_in, rpb, w_s, b_s, gmlp_norm, out_norm_a, out_norm_b,
           w_out, norm2, w_router, w_gate, w_up, w_down, norm_final):
    b, n, d = x.shape
    assert w_mod.shape[0] == 1, "single-layer stack only"
    assert n % (GRID_W * Q_ROWS) == 0 and n // GRID_W >= W_ROWS
    assert n % (2 * TCH) == 0 and N_EXPERTS % EXPERT_GROUP == 0
    cap = EC_CAPACITY_FACTOR * n // N_EXPERTS

    pad = (-(b + 1)) % 8
    cc = jnp.concatenate([c, c_ctx[None], jnp.zeros((pad, d), F32)], axis=0)
    m = _modulation(cc, w_mod[0], b_mod[0][None])
    mod = m.reshape(-1, N_MOD, d)

    w_in_b = w_in[0].astype(BF16)
    ws2 = w_s[0].astype(BF16).reshape(N_GROUPS_SG // 2, 2 * CHUNK, CHUNK)
    bs2 = jnp.broadcast_to(b_s[0].reshape(N_GROUPS_SG // 2, 2 * CHUNK, 1), (N_GROUPS_SG // 2, 2 * CHUNK, LANES))
    q, k, v, ob = _in_proj(x, mod, norm1, w_in_b, ws2, bs2, gmlp_norm, out_norm_b)

    kc, vc = _ctx_proj(ctx.reshape(b * CTX_LEN, d), mod, b, norm1, w_in_b)
    kc = kc.reshape(b, CTX_LEN, D_NA)
    vc = vc.reshape(b, CTX_LEN, D_NA)

    pb, table = _bias_blocks(rpb[0], n // GRID_W)
    oa = _attention(q, k, v, kc, vc, pb, table, out_norm_a)

    x_new, h2, aff, aff_terms = _out_proj(oa, ob, x, mod, w_out[0].astype(BF16), norm2, w_router[0].T)

    slot, ebound = _topk_slots(aff.reshape(b * N_EXPERTS, n), cap)
    slot = slot.reshape(b, N_EXPERTS, n)
    ebound = ebound[:, :n // TCH + 1].reshape(-1)

    xs, gate = _gather_tokens(ebound, h2, slot, aff_terms, cap)
    y = _experts(xs, w_gate[0], w_up[0], w_down[0], gate)
    return _combine(ebound, jnp.swapaxes(slot, 1, 2), y, x_new, mod, norm_final[None], cap)
```

```python
import functools

import numpy as np
import jax
import jax.numpy as jnp
from jax import lax
from jax.experimental import pallas as pl
from jax.experimental.pallas import tpu as pltpu

D_MODEL = 1024
GRID_W = 64
CTX_LEN = 256
N_HEADS_NA = 8
HEAD_DIM = 64
D_NA = N_HEADS_NA * HEAD_DIM
NA_KH = 8
NA_KW = 16
D_SG = D_MODEL - D_NA
N_GROUPS_SG = 8
SG_GROUP_DIM = D_SG // N_GROUPS_SG
CHUNK = 128
N_EXPERTS = 16
EC_CAPACITY_FACTOR = 2
D_EXPERT = 1536
N_MOD = 6
EPS = 1e-6
NEG_INF = -1e30
LOG2E = float(np.log2(np.e))

LANES = 128
VMEM_LIMIT = 56 * 1024 * 1024

F32 = jnp.float32
BF16 = jnp.bfloat16
HIGHEST = lax.Precision.HIGHEST

Q_ROWS = 4
W_ROWS = 12
TQ = Q_ROWS * GRID_W
TW = W_ROWS * GRID_W

MANTISSA_STEPS = 36

TCH = 256
SLOT_W = 64
SLOT_ALIGN = 16
EXPERT_GROUP = 4


def _params(*sem):
    return pltpu.CompilerParams(dimension_semantics=sem, vmem_limit_bytes=VMEM_LIMIT)


def _rms_mod(x, g, shift, scale):
    r = lax.rsqrt(jnp.mean(x * x, axis=-1, keepdims=True) + EPS)
    return (x * r) * g * (1.0 + scale) + shift


def _rms(x, g):
    return x * lax.rsqrt(jnp.mean(x * x, axis=-1, keepdims=True) + EPS) * g


def _gelu_tanh(x):
    return 0.5 * x * (1.0 + jnp.tanh(np.sqrt(2.0 / np.pi).astype(np.float32) * (x + 0.044715 * (x * x * x))))


def _silu(x):
    return x * jax.nn.sigmoid(x)


def _bf16_terms(x, n_terms):
    terms = []
    for _ in range(n_terms):
        t = x.astype(BF16)
        terms.append(t)
        x = x - t.astype(F32)
    return terms


def _mod_kernel(c_ref, w_ref, b_ref, o_ref):
    rows = c_ref.shape[0]
    s = jnp.concatenate(_bf16_terms(_silu(c_ref[...]), 3), axis=0)
    w_hi, w_lo = _bf16_terms(w_ref[...], 2)
    hi = jnp.dot(s, w_hi, preferred_element_type=F32)
    lo = jnp.dot(s[:2 * rows], w_lo, preferred_element_type=F32)
    small = (hi[2 * rows:] + lo[rows:]) + (hi[rows:2 * rows] + lo[:rows])
    o_ref[...] = (small + hi[:rows]) + b_ref[...]


def _modulation(cc, w_mod, b_mod):
    rows, d = cc.shape
    n = w_mod.shape[1]
    tn = 1024
    return pl.pallas_call(
        _mod_kernel,
        grid=(n // tn,),
        in_specs=[pl.BlockSpec((rows, d), lambda j: (0, 0)),
                  pl.BlockSpec((d, tn), lambda j: (0, j)),
                  pl.BlockSpec((1, tn), lambda j: (0, j))],
        out_specs=pl.BlockSpec((rows, tn), lambda j: (0, j)),
        out_shape=jax.ShapeDtypeStruct((rows, n), F32),
        compiler_params=_params("arbitrary"),
        name="modulation",
    )(cc, w_mod, b_mod)


def _in_kernel(x_ref, mod_ref, n1_ref, w_ref, ws_ref, bs_ref, gn_ref, onb_ref,
               q_ref, k_ref, v_ref, ob_ref, sp_ref, *, sub):
    mod = mod_ref[0]
    tiles = [slice(r, r + sub) for r in range(0, x_ref.shape[1], sub)]
    nch = sub // CHUNK
    lane = lax.broadcasted_iota(jnp.int32, (CHUNK, LANES), 1)
    first = lane < SG_GROUP_DIM

    def project(rs):
        hb = _rms_mod(x_ref[0, rs], n1_ref[...], mod[0:1], mod[1:2]).astype(BF16)
        qkv = jnp.dot(hb, w_ref[:, :3 * D_NA], preferred_element_type=F32)
        q_ref[0, rs] = (qkv[:, :D_NA] * (HEAD_DIM ** -0.5 * LOG2E)).astype(BF16)
        k_ref[0, rs] = qkv[:, D_NA:2 * D_NA].astype(BF16)
        v_ref[0, rs] = qkv[:, 2 * D_NA:].astype(BF16)
        return jnp.dot(hb, w_ref[:, 3 * D_NA:], preferred_element_type=F32)

    def gate_mlp(rs, uz):
        u = _gelu_tanh(uz[:, :D_SG])
        z = _gelu_tanh(uz[:, D_SG:])
        mu = jnp.mean(z, axis=-1, keepdims=True)
        zc = z - mu
        var = jnp.mean(zc * zc, axis=-1, keepdims=True)
        zb = (zc * lax.rsqrt(var + EPS) * gn_ref[...]).astype(BF16)
        for p in range(N_GROUPS_SG // 2):
            zp = jnp.concatenate(
                [zb[c * CHUNK:(c + 1) * CHUNK, p * LANES:(p + 1) * LANES] for c in range(nch)], axis=1)
            r = jnp.dot(ws_ref[p], zp, preferred_element_type=F32)
            bs = bs_ref[p]
            for c in range(nch):
                top = r[:CHUNK, c * LANES:(c + 1) * LANES] + bs[:CHUNK]
                bot = r[CHUNK:, c * LANES:(c + 1) * LANES] + bs[CHUNK:]
                sp_ref[rs.start + c * CHUNK:rs.start + (c + 1) * CHUNK, p * LANES:(p + 1) * LANES] = (
                    jnp.where(first, top, bot))
        ob = u * sp_ref[rs, :]
        ob_ref[0, rs] = _rms(ob, onb_ref[...]).astype(BF16)

    uz = {0: project(tiles[0])}
    for i, rs in enumerate(tiles):
        if i + 1 < len(tiles):
            uz[i + 1] = project(tiles[i + 1])
        gate_mlp(rs, uz.pop(i))


def _in_proj(x, mod, norm1, w_in_b, ws2, bs2, gmlp_norm, out_norm_b, tm=2048, sub=512):
    b, n, d = x.shape
    d_in = w_in_b.shape[1]
    full2 = lambda i, j: (0, 0)
    full3 = lambda i, j: (0, 0, 0)
    tile = lambda i, j: (i, j, 0)
    act = jax.ShapeDtypeStruct((b, n, D_NA), BF16)
    return pl.pallas_call(
        functools.partial(_in_kernel, sub=sub),
        grid=(b, n // tm),
        in_specs=[pl.BlockSpec((1, tm, d), tile),
                  pl.BlockSpec((1, N_MOD, d), lambda i, j: (i, 0, 0)),
                  pl.BlockSpec((1, d), full2),
                  pl.BlockSpec((d, d_in), full2),
                  pl.BlockSpec(ws2.shape, full3),
                  pl.BlockSpec(bs2.shape, full3),
                  pl.BlockSpec((1, D_SG), full2),
                  pl.BlockSpec((1, D_SG), full2)],
        out_specs=[pl.BlockSpec((1, tm, D_NA), tile)] * 4,
        out_shape=[act] * 4,
        scratch_shapes=[pltpu.VMEM((tm, D_SG), F32)],
        compiler_params=_params("parallel", "arbitrary"),
        name="in_proj_gmlp",
    )(x, mod, norm1, w_in_b, ws2, bs2, gmlp_norm, out_norm_b)


def _ctx_kernel(x_ref, mod_ref, n1_ref, wk_ref, wv_ref, k_ref, v_ref):
    mod = mod_ref[0]
    hb = _rms_mod(x_ref[...], n1_ref[...], mod[0:1], mod[1:2]).astype(BF16)
    k_ref[...] = jnp.dot(hb, wk_ref[...], preferred_element_type=F32).astype(BF16)
    v_ref[...] = jnp.dot(hb, wv_ref[...], preferred_element_type=F32).astype(BF16)


def _ctx_proj(ctx2, mod, mod_row, norm1, w_in_b, tm=512):
    rows, d = ctx2.shape
    act = jax.ShapeDtypeStruct((rows, D_NA), BF16)
    return pl.pallas_call(
        _ctx_kernel,
        grid=(rows // tm,),
        in_specs=[pl.BlockSpec((tm, d), lambda i: (i, 0)),
                  pl.BlockSpec((1, N_MOD, d), lambda i: (mod_row, 0, 0)),
                  pl.BlockSpec((1, d), lambda i: (0, 0)),
                  pl.BlockSpec((d, D_NA), lambda i: (0, 1)),
                  pl.BlockSpec((d, D_NA), lambda i: (0, 2))],
        out_specs=[pl.BlockSpec((tm, D_NA), lambda i: (i, 0))] * 2,
        out_shape=[act, act],
        compiler_params=_params("arbitrary"),
        name="ctx_kv_proj",
    )(ctx2, mod, norm1, w_in_b, w_in_b)


def _window_start(t, rows):
    return jnp.clip(Q_ROWS * t - NA_KH // 2, 0, rows - W_ROWS)


def _attn_kernel(tbl_ref, q_ref, k_ref, v_ref, kc_ref, vc_ref, pb_ref, ona_ref, o_ref, *, rows):
    t = pl.program_id(1)
    n_tiles = rows // Q_ROWS
    ws = pl.multiple_of(_window_start(t, rows) * GRID_W, GRID_W)
    kw = k_ref[0, pl.ds(ws, TW), :]
    vw = v_ref[0, pl.ds(ws, TW), :]
    q = q_ref[0]
    kc = kc_ref[0]
    vc = vc_ref[0]
    lane = lax.broadcasted_iota(jnp.int32, (TQ, LANES), 1)
    first = lane < HEAD_DIM
    nt = (((1,), (1,)), ((), ()))
    cls = jnp.where(t == 0, 0, jnp.where(t == n_tiles - 1, 2, 1))
    pair = [[tbl_ref[(cls * Q_ROWS + r) * (W_ROWS // 2) + j] for j in range(W_ROWS // 2)] for r in range(Q_ROWS)]

    def operands(h):
        sl = slice((h // 2) * LANES, (h // 2 + 1) * LANES)
        return q[:, sl], kw[:, sl], vw[:, sl], kc[:, sl], vc[:, sl]

    def scores(h):
        qp, kp, _, kcp, _ = operands(h)
        qm = jnp.where(first if h % 2 == 0 else jnp.logical_not(first), qp, jnp.zeros_like(qp))
        bias = jnp.concatenate(
            [jnp.concatenate([pb_ref[h, pair[r][c]] for c in range(W_ROWS // 2)], axis=1)
             for r in range(Q_ROWS)], axis=0)
        s_nb = lax.dot_general(qm, kp, nt, preferred_element_type=F32) + bias
        s_cx = lax.dot_general(qm, kcp, nt, preferred_element_type=F32)
        return s_nb, s_cx

    def softmax(s_nb, s_cx):
        m = jnp.maximum(jnp.max(s_nb, axis=1, keepdims=True), jnp.max(s_cx, axis=1, keepdims=True))
        e_nb = jnp.exp2(s_nb - m)
        e_cx = jnp.exp2(s_cx - m)
        l = jnp.sum(e_nb, axis=1, keepdims=True) + jnp.sum(e_cx, axis=1, keepdims=True)
        return e_nb.astype(BF16), e_cx.astype(BF16), l

    def values(h, e_nb, e_cx, l):
        _, _, vp, _, vcp = operands(h)
        o = jnp.dot(e_nb, vp, preferred_element_type=F32) + jnp.dot(e_cx, vcp, preferred_element_type=F32)
        return o / l

    s, p, o = {}, {}, {}
    for step in range(N_HEADS_NA + 2):
        if step < N_HEADS_NA:
            s[step] = scores(step)
        if 0 <= step - 1 < N_HEADS_NA:
            p[step - 1] = softmax(*s.pop(step - 1))
        if 0 <= step - 2 < N_HEADS_NA:
            o[step - 2] = values(step - 2, *p.pop(step - 2))
    out = jnp.concatenate([jnp.where(first, o[h], o[h + 1]) for h in range(0, N_HEADS_NA, 2)], axis=1)
    o_ref[0] = _rms(out, ona_ref[...]).astype(BF16)


def _bias_blocks(rpb, rows):
    n_tiles = rows // Q_ROWS
    n_ro, n_co = 2 * NA_KH - 1, 2 * NA_KW - 1
    qc = np.arange(GRID_W)[:, None]
    kc = np.arange(GRID_W)[None, :]
    cs = np.clip(qc - NA_KW // 2, 0, GRID_W - NA_KW)
    col_ok = (kc >= cs) & (kc < cs + NA_KW)
    spread = ((kc - qc + NA_KW - 1)[None] == np.arange(n_co)[:, None, None]) & col_ok[None]
    blocks = jnp.dot(rpb.reshape(-1, n_co), jnp.asarray(spread.reshape(n_co, -1), F32), precision=HIGHEST)
    blocks = jnp.where(jnp.asarray(col_ok.reshape(-1)), blocks * LOG2E, NEG_INF)
    blocks = blocks.reshape(N_HEADS_NA, n_ro, GRID_W, GRID_W)
    masked = jnp.full((N_HEADS_NA, 1, GRID_W, GRID_W), NEG_INF, F32)
    ext = jnp.concatenate([blocks, masked], axis=1)
    pairs, table = [], []
    for t in (0, 1, n_tiles - 1):
        ws = int(np.clip(Q_ROWS * t - NA_KH // 2, 0, rows - W_ROWS))
        for rho in range(Q_ROWS):
            r = Q_ROWS * t + rho
            rs = int(np.clip(r - NA_KH // 2, 0, rows - NA_KH))
            off = [kr - r + NA_KH - 1 if rs <= kr < rs + NA_KH else n_ro for kr in range(ws, ws + W_ROWS)]
            for c in range(W_ROWS // 2):
                pr = (off[2 * c], off[2 * c + 1])
                if pr not in pairs:
                    pairs.append(pr)
                table.append(pairs.index(pr))
    pb = jnp.stack([jnp.concatenate([ext[:, a], ext[:, b]], axis=-1) for a, b in pairs], axis=1)
    return pb, jnp.asarray(np.asarray(table, np.int32))


def _attention(q, k, v, kc, vc, pb, table, out_norm_a):
    b, n, _ = q.shape
    rows = n // GRID_W
    n_tiles = rows // Q_ROWS
    per_b = lambda i, t, *_: (i, 0, 0)
    return pl.pallas_call(
        functools.partial(_attn_kernel, rows=rows),
        grid_spec=pltpu.PrefetchScalarGridSpec(
            num_scalar_prefetch=1,
            grid=(b, n_tiles),
            in_specs=[pl.BlockSpec((1, TQ, D_NA), lambda i, t, *_: (i, t, 0)),
                      pl.BlockSpec((1, n, D_NA), per_b),
                      pl.BlockSpec((1, n, D_NA), per_b),
                      pl.BlockSpec((1, CTX_LEN, D_NA), per_b),
                      pl.BlockSpec((1, CTX_LEN, D_NA), per_b),
                      pl.BlockSpec(pb.shape, lambda i, t, *_: (0, 0, 0, 0)),
                      pl.BlockSpec((1, D_NA), lambda i, t, *_: (0, 0))],
            out_specs=pl.BlockSpec((1, TQ, D_NA), lambda i, t, *_: (i, t, 0))),
        out_shape=jax.ShapeDtypeStruct((b, n, D_NA), BF16),
        compiler_params=_params("parallel", "arbitrary"),
        name="nbr_attention",
    )(table, q, k, v, kc, vc, pb, out_norm_a)


def _out_kernel(oa_ref, ob_ref, x_ref, mod_ref, w_ref, n2_ref, wr_ref, xn_ref, h2_ref, aff_ref, *, sub):
    mod = mod_ref[0]
    nt = (((1,), (1,)), ((), ()))
    w_hi, w_lo = _bf16_terms(wr_ref[...], 2)
    w2 = jnp.concatenate([w_hi, w_lo], axis=0)
    tiles = [slice(r, r + sub) for r in range(0, x_ref.shape[1], sub)]
    mixes = [jnp.dot(oa_ref[0, rs], w_ref[:D_NA], preferred_element_type=F32)
             + jnp.dot(ob_ref[0, rs], w_ref[D_NA:], preferred_element_type=F32) for rs in tiles]
    for rs, mix in zip(tiles, mixes):
        xn = x_ref[0, rs] + mod[2:3] * mix
        xn_ref[0, rs] = xn
        h2 = _rms_mod(xn, n2_ref[...], mod[3:4], mod[4:5])
        h_hi, h_lo = _bf16_terms(h2, 2)
        h2_ref[0, rs] = h_hi
        l_hi = lax.dot_general(w2, h_hi, nt, preferred_element_type=F32)
        l_lo = lax.dot_general(w_hi, h_lo, nt, preferred_element_type=F32)
        logits = l_hi[:N_EXPERTS] + l_hi[N_EXPERTS:] + l_lo
        e = jnp.exp(logits - jnp.max(logits, axis=0, keepdims=True))
        aff_ref[0, :, rs] = e / jnp.sum(e, axis=0, keepdims=True)


def _out_proj(oa, ob, x, mod, w_out_b, norm2, w_router_t, tm=1024, sub=512):
    b, n, d = x.shape
    tile = lambda i, j: (i, j, 0)
    full = lambda i, j: (0, 0)
    return pl.pallas_call(
        functools.partial(_out_kernel, sub=sub),
        grid=(b, n // tm),
        in_specs=[pl.BlockSpec((1, tm, D_NA), tile),
                  pl.BlockSpec((1, tm, D_SG), tile),
                  pl.BlockSpec((1, tm, d), tile),
                  pl.BlockSpec((1, N_MOD, d), lambda i, j: (i, 0, 0)),
                  pl.BlockSpec(w_out_b.shape, full),
                  pl.BlockSpec((1, d), full),
                  pl.BlockSpec((N_EXPERTS, d), full)],
        out_specs=[pl.BlockSpec((1, tm, d), tile),
                   pl.BlockSpec((1, tm, d), tile),
                   pl.BlockSpec((1, N_EXPERTS, tm), lambda i, j: (i, 0, j))],
        out_shape=[jax.ShapeDtypeStruct((b, n, d), F32),
                   jax.ShapeDtypeStruct((b, n, d), BF16),
                   jax.ShapeDtypeStruct((b, N_EXPERTS, n), F32)],
        compiler_params=_params("parallel", "arbitrary"),
        name="out_proj_router",
    )(oa, ob, x, mod, w_out_b, norm2, w_router_t)


def _prefix_count(mask_f, tri):
    rows, n = mask_f.shape
    parts = []
    carry = jnp.zeros((rows, 1), F32)
    for j in range(n // LANES):
        blk = mask_f[:, j * LANES:(j + 1) * LANES]
        parts.append(jnp.dot(blk.astype(BF16), tri, preferred_element_type=F32) + carry)
        carry = carry + jnp.sum(blk, axis=1, keepdims=True)
    return jnp.concatenate(parts, axis=1)


def _topk_kernel(aff_ref, slot_ref, eb_ref, *, cap):
    a = aff_ref[...]
    rows = a.shape[0]

    def enough(t):
        return jnp.sum(jnp.where(a >= t, 1.0, 0.0), axis=1, keepdims=True) >= cap

    tiny = jnp.full((rows, 1), 2.0 ** -126, F32)
    normal = enough(tiny)
    pw = tiny
    hi = jnp.full((rows, 1), 4.0, F32)
    for bit in range(6, -1, -1):
        cand = pw * (2.0 ** (1 << bit))
        ok = enough(cand)
        pw = jnp.where(ok, cand, pw)
        hi = jnp.where(ok, hi, cand)
    lo = jnp.where(normal, pw, 0.0)
    hi = jnp.where(normal, hi, tiny)
    step = lo
    for _ in range(MANTISSA_STEPS):
        step = step * 0.5
        cand = lo + step
        ok = enough(cand)
        lo = jnp.where(ok, cand, lo)
        hi = jnp.where(ok, hi, cand)
    above = a >= hi
    tie = jnp.logical_and(a >= lo, jnp.logical_not(above))
    n_above = jnp.sum(jnp.where(above, 1.0, 0.0), axis=1, keepdims=True)
    ri = lax.broadcasted_iota(jnp.int32, (LANES, LANES), 0)
    ci = lax.broadcasted_iota(jnp.int32, (LANES, LANES), 1)
    tri = jnp.where(ri <= ci, 1.0, 0.0).astype(BF16)
    tie_rank = _prefix_count(jnp.where(tie, 1.0, 0.0), tri)
    sel = jnp.logical_or(above, jnp.logical_and(tie, tie_rank <= cap - n_above))
    sel_f = jnp.where(sel, 1.0, 0.0)
    pos = _prefix_count(sel_f, tri) - 1.0
    slot_ref[...] = jnp.where(sel, pos, -1.0).astype(jnp.int32)
    lane = lax.broadcasted_iota(jnp.int32, (rows, LANES), 1)
    cnt = jnp.zeros((rows, LANES), F32)
    for j in range(a.shape[1] // TCH):
        cnt = jnp.where(lane == j, jnp.sum(sel_f[:, j * TCH:(j + 1) * TCH], axis=1, keepdims=True), cnt)
    before = jnp.where(ri < ci, 1.0, 0.0).astype(BF16)
    eb_ref[...] = jnp.dot(cnt.astype(BF16), before, preferred_element_type=F32).astype(jnp.int32)


def _topk_slots(aff2, cap):
    rows, n = aff2.shape
    tr = rows
    return pl.pallas_call(
        functools.partial(_topk_kernel, cap=cap),
        grid=(rows // tr,),
        in_specs=[pl.BlockSpec((tr, n), lambda i: (i, 0))],
        out_specs=[pl.BlockSpec((tr, n), lambda i: (i, 0)), pl.BlockSpec((tr, LANES), lambda i: (i, 0))],
        out_shape=[jax.ShapeDtypeStruct((rows, n), jnp.int32), jax.ShapeDtypeStruct((rows, LANES), jnp.int32)],
        compiler_params=_params("parallel"),
        name="expert_topk",
    )(aff2)


def _slot_window(eb_ref, bi, e, j, n_e, n_chunks, cap):
    base = (bi * n_e + e) * (n_chunks + 1) + j
    s0 = eb_ref[base]
    s1 = eb_ref[base + 1]
    start = jnp.minimum(s0 & -SLOT_ALIGN, cap - SLOT_W)
    return s0, s1, pl.multiple_of(start, SLOT_ALIGN)


def _window_overflow(eb_ref, bi, j, n_e, n_chunks, cap):
    over = None
    for e in range(n_e):
        _, s1, start = _slot_window(eb_ref, bi, e, j, n_e, n_chunks, cap)
        o = s1 > start + SLOT_W
        over = o if over is None else jnp.logical_or(over, o)
    return over


def _gather_kernel(eb_ref, h2_ref, slot_ref, aff_ref, xs_ref, gate_ref, *, cap, jc, n_chunks):
    bi = pl.program_id(0)
    jo = pl.program_id(1)
    n_e = slot_ref.shape[1]
    wi = lax.broadcasted_iota(jnp.int32, (SLOT_W, TCH), 0)
    wcol = lax.broadcasted_iota(jnp.int32, (SLOT_W, 1), 0)

    @pl.when(jo == 0)
    def _():
        xs_ref[...] = jnp.zeros_like(xs_ref)
        gate_ref[...] = jnp.zeros_like(gate_ref)

    for jj in range(jc):
        j = jo * jc + jj
        tok = slice(jj * TCH, (jj + 1) * TCH)
        h2c = h2_ref[0, tok, :]
        for g in range(n_e // EXPERT_GROUP):
            blocks, meta = [], []
            for q in range(EXPERT_GROUP):
                e = EXPERT_GROUP * g + q
                s0, s1, start = _slot_window(eb_ref, bi, e, j, n_e, n_chunks, cap)
                hit = slot_ref[0, e:e + 1, tok] - start == wi
                blocks.append(jnp.where(hit, 1.0, 0.0).astype(BF16))
                gate = jnp.sum(jnp.where(hit, aff_ref[0, e:e + 1, tok], 0.0), axis=1, keepdims=True)
                meta.append((e, s0, s1, start, gate))
            onehot = jnp.concatenate(blocks, axis=0)
            rows = jnp.dot(onehot, h2c, preferred_element_type=F32).astype(BF16)
            for q, (e, s0, s1, start, gate) in enumerate(meta):
                own = jnp.logical_and(wcol + start >= s0, wcol + start < s1)
                win = pl.ds(start, SLOT_W)
                xs_ref[0, e, win, :] = jnp.where(own, rows[q * SLOT_W:(q + 1) * SLOT_W], xs_ref[0, e, win, :])
                gate_ref[0, e, win, :] = jnp.where(own, gate, gate_ref[0, e, win, :])

    for jj in range(jc):
        j = jo * jc + jj
        tok = slice(jj * TCH, (jj + 1) * TCH)

        @pl.when(_window_overflow(eb_ref, bi, j, n_e, n_chunks, cap))
        def _():
            h2c = h2_ref[0, tok, :]
            ci = lax.broadcasted_iota(jnp.int32, (cap, TCH), 0)
            ccol = lax.broadcasted_iota(jnp.int32, (cap, 1), 0)
            for e in range(n_e):
                s0, s1, _ = _slot_window(eb_ref, bi, e, j, n_e, n_chunks, cap)
                hit = slot_ref[0, e:e + 1, tok] == ci
                rows = jnp.dot(jnp.where(hit, 1.0, 0.0).astype(BF16), h2c, preferred_element_type=F32).astype(BF16)
                gate = jnp.sum(jnp.where(hit, aff_ref[0, e:e + 1, tok], 0.0), axis=1, keepdims=True)
                own = jnp.logical_and(ccol >= s0, ccol < s1)
                xs_ref[0, e] = jnp.where(own, rows, xs_ref[0, e])
                gate_ref[0, e] = jnp.where(own, gate, gate_ref[0, e])


def _gather_tokens(ebound, h2, slot, aff, cap, jc=2):
    b, n, d = h2.shape
    e = slot.shape[1]
    n_chunks = n // TCH
    tok = lambda i, j, *_: (i, j, 0)
    per_b = lambda i, j, *_: (i, 0, 0, 0)
    return pl.pallas_call(
        functools.partial(_gather_kernel, cap=cap, jc=jc, n_chunks=n_chunks),
        grid_spec=pltpu.PrefetchScalarGridSpec(
            num_scalar_prefetch=1,
            grid=(b, n_chunks // jc),
            in_specs=[pl.BlockSpec((1, jc * TCH, d), tok),
                      pl.BlockSpec((1, e, jc * TCH), lambda i, j, *_: (i, 0, j)),
                      pl.BlockSpec((1, e, jc * TCH), lambda i, j, *_: (i, 0, j))],
            out_specs=[pl.BlockSpec((1, e, cap, d), per_b),
                       pl.BlockSpec((1, e, cap, 1), per_b)]),
        out_shape=[jax.ShapeDtypeStruct((b, e, cap, d), BF16),
                   jax.ShapeDtypeStruct((b, e, cap, 1), F32)],
        compiler_params=_params("parallel", "arbitrary"),
        name="moe_gather",
    )(ebound, h2, slot, aff)


def _expert_kernel(xs_ref, wg_ref, wu_ref, wd_ref, gate_ref, y_ref, acc_ref, *, rb, n_f):
    f = pl.program_id(1)
    nb, _, cap, d = xs_ref.shape

    def step(first, last):
        wg = wg_ref[0].astype(BF16)
        wu = wu_ref[0].astype(BF16)
        wd = wd_ref[0].astype(BF16)
        for i in range(nb // rb):
            blk = slice(i * rb, (i + 1) * rb)
            xs = xs_ref[blk, 0].reshape(rb * cap, d)
            a = jnp.dot(xs, wg, preferred_element_type=F32)
            u = jnp.dot(xs, wu, preferred_element_type=F32)
            hm = (_silu(a) * u).astype(BF16)
            part = jnp.dot(hm, wd, preferred_element_type=F32).reshape(rb, cap, d)
            total = part if first else acc_ref[blk] + part
            if last:
                y_ref[blk, 0] = (total * gate_ref[blk, 0]).astype(BF16)
            else:
                acc_ref[blk] = total

    if n_f == 1:
        step(True, True)
    else:
        pl.when(f == 0)(lambda: step(True, False))
        if n_f > 2:
            pl.when(jnp.logical_and(f > 0, f < n_f - 1))(lambda: step(False, False))
        pl.when(f == n_f - 1)(lambda: step(False, True))


def _experts(xs, w_gate, w_up, w_down, gate, fc=768, rb=4):
    b, e, cap, d = xs.shape
    dff = w_gate.shape[2]
    per_e = lambda i, f: (0, i, 0, 0)
    return pl.pallas_call(
        functools.partial(_expert_kernel, rb=rb, n_f=dff // fc),
        grid=(e, dff // fc),
        in_specs=[pl.BlockSpec((b, 1, cap, d), per_e),
                  pl.BlockSpec((1, d, fc), lambda i, f: (i, 0, f)),
                  pl.BlockSpec((1, d, fc), lambda i, f: (i, 0, f)),
                  pl.BlockSpec((1, fc, d), lambda i, f: (i, f, 0)),
                  pl.BlockSpec((b, 1, cap, 1), per_e)],
        out_specs=pl.BlockSpec((b, 1, cap, d), per_e),
        out_shape=jax.ShapeDtypeStruct((b, e, cap, d), BF16),
        scratch_shapes=[pltpu.VMEM((b, cap, d), F32)],
        compiler_params=_params("parallel", "arbitrary"),
        name="moe_experts",
    )(xs, w_gate, w_up, w_down, gate)


def _combine_kernel(eb_ref, slot_t_ref, y_ref, xn_ref, mod_ref, nf_ref, o_ref, *, cap, jc, n_chunks):
    bi = pl.program_id(0)
    jo = pl.program_id(1)
    n_e = y_ref.shape[1]
    ci = lax.broadcasted_iota(jnp.int32, (TCH, EXPERT_GROUP * SLOT_W), 1)
    toks = [slice(jj * TCH, (jj + 1) * TCH) for jj in range(jc)]

    def finish(tok, moe):
        x = xn_ref[0, tok] + mod_ref[0][5:6] * moe
        o_ref[0, tok] = _rms(x, nf_ref[...])

    moes = []
    for jj, tok in enumerate(toks):
        j = jo * jc + jj
        st = slot_t_ref[0, tok, :]
        s_blocks, y_blocks = [], []
        for g in range(n_e // EXPERT_GROUP):
            col = None
            for q in reversed(range(EXPERT_GROUP)):
                e = EXPERT_GROUP * g + q
                _, _, start = _slot_window(eb_ref, bi, e, j, n_e, n_chunks, cap)
                rel = st[:, e:e + 1] - start
                tgt = jnp.where(jnp.logical_and(rel >= 0, rel < SLOT_W), rel + q * SLOT_W, -1)
                col = tgt if col is None else jnp.where(ci < (q + 1) * SLOT_W, tgt, col)
                y_blocks.insert(g * EXPERT_GROUP, y_ref[0, e, pl.ds(start, SLOT_W), :])
            s_blocks.append(jnp.where(col == ci, 1.0, 0.0).astype(BF16))
        scat = jnp.concatenate(s_blocks, axis=1)
        ywin = jnp.concatenate(y_blocks, axis=0)
        moes.append(jnp.dot(scat, ywin, preferred_element_type=F32))
    for tok, moe in zip(toks, moes):
        finish(tok, moe)

    for jj, tok in enumerate(toks):
        @pl.when(_window_overflow(eb_ref, bi, jo * jc + jj, n_e, n_chunks, cap))
        def _():
            st = slot_t_ref[0, tok, :]
            cf = lax.broadcasted_iota(jnp.int32, (TCH, cap), 1)
            dense = jnp.concatenate(
                [jnp.where(st[:, e:e + 1] == cf, 1.0, 0.0).astype(BF16) for e in range(n_e)], axis=1)
            finish(tok, jnp.dot(dense, y_ref[0].reshape(n_e * cap, y_ref.shape[3]),
                                preferred_element_type=F32))


def _combine(ebound, slot_t, y, x_new, mod, norm_final, cap, jc=2):
    b, n, d = x_new.shape
    e = slot_t.shape[2]
    n_chunks = n // TCH
    tile = lambda i, j, *_: (i, j, 0)
    return pl.pallas_call(
        functools.partial(_combine_kernel, cap=cap, jc=jc, n_chunks=n_chunks),
        grid_spec=pltpu.PrefetchScalarGridSpec(
            num_scalar_prefetch=1,
            grid=(b, n_chunks // jc),
            in_specs=[pl.BlockSpec((1, jc * TCH, e), tile),
                      pl.BlockSpec((1, e, cap, d), lambda i, j, *_: (i, 0, 0, 0)),
                      pl.BlockSpec((1, jc * TCH, d), tile),
                      pl.BlockSpec((1, N_MOD, d), lambda i, j, *_: (i, 0, 0)),
                      pl.BlockSpec((1, d), lambda i, j, *_: (0, 0))],
            out_specs=pl.BlockSpec((1, jc * TCH, d), tile)),
        out_shape=jax.ShapeDtypeStruct((b, n, d), F32),
        compiler_params=_params("parallel", "arbitrary"),
        name="moe_combine_norm",
    )(ebound, slot_t, y, x_new, mod, norm_final)


def kernel(x, c, ctx, c_ctx, w_mod, b_mod, norm1, w_in, rpb, w_s, b_s, gmlp_norm, out_norm_a, out_norm_b,
           w_out, norm2, w_router, w_gate, w_up, w_down, norm_final):
    b, n, d = x.shape
    assert w_mod.shape[0] == 1, "single-layer stack only"
    assert n % (GRID_W * Q_ROWS) == 0 and n // GRID_W >= W_ROWS
    assert n % (2 * TCH) == 0 and N_EXPERTS % EXPERT_GROUP == 0
    cap = EC_CAPACITY_FACTOR * n // N_EXPERTS

    pad = (-(b + 1)) % 8
    cc = jnp.concatenate([c, c_ctx[None], jnp.zeros((pad, d), F32)], axis=0)
    m = _modulation(cc, w_mod[0], b_mod[0][None])
    mod = m.reshape(-1, N_MOD, d)

    w_in_b = w_in[0].astype(BF16)
    ws2 = w_s[0].astype(BF16).reshape(N_GROUPS_SG // 2, 2 * CHUNK, CHUNK)
    bs2 = jnp.broadcast_to(b_s[0].reshape(N_GROUPS_SG // 2, 2 * CHUNK, 1), (N_GROUPS_SG // 2, 2 * CHUNK, LANES))
    q, k, v, ob = _in_proj(x, mod, norm1, w_in_b, ws2, bs2, gmlp_norm, out_norm_b)

    kc, vc = _ctx_proj(ctx.reshape(b * CTX_LEN, d), mod, b, norm1, w_in_b)
    kc = kc.reshape(b, CTX_LEN, D_NA)
    vc = vc.reshape(b, CTX_LEN, D_NA)

    pb, table = _bias_blocks(rpb[0], n // GRID_W)
    oa = _attention(q, k, v, kc, vc, pb, table, out_norm_a)

    x_new, h2, aff = _out_proj(oa, ob, x, mod, w_out[0].astype(BF16), norm2, w_router[0].T)

    slot, ebound = _topk_slots(aff.reshape(b * N_EXPERTS, n), cap)
    slot = slot.reshape(b, N_EXPERTS, n)
    ebound = ebound[:, :n // TCH + 1].reshape(-1)

    xs, gate = _gather_tokens(ebound, h2, slot, aff, cap)
    y = _experts(xs, w_gate[0], w_up[0], w_down[0], gate)
    return _combine(ebound, jnp.swapaxes(slot, 1, 2), y, x_new, mod, norm_final[None], cap)
```

```python
import functools

import numpy as np
import jax
import jax.numpy as jnp
from jax import lax
from jax.experimental import pallas as pl
from jax.experimental.pallas import tpu as pltpu

D_MODEL = 1024
GRID_W = 64
CTX_LEN = 256
N_HEADS_NA = 8
HEAD_DIM = 64
D_NA = N_HEADS_NA * HEAD_DIM
NA_KH = 8
NA_KW = 16
D_SG = D_MODEL - D_NA
N_GROUPS_SG = 8
SG_GROUP_DIM = D_SG // N_GROUPS_SG
CHUNK = 128
N_EXPERTS = 16
EC_CAPACITY_FACTOR = 2
D_EXPERT = 1536
N_MOD = 6
EPS = 1e-6
NEG_INF = -1e30
LOG2E = float(np.log2(np.e))

LANES = 128
VMEM_LIMIT = 56 * 1024 * 1024

F32 = jnp.float32
BF16 = jnp.bfloat16
HIGHEST = lax.Precision.HIGHEST

Q_ROWS = 4
W_ROWS = 12
TQ = Q_ROWS * GRID_W
TW = W_ROWS * GRID_W
TILES_PER_STEP = 2

MANTISSA_STEPS = 36

TCH = 256
SLOT_W = 64
SLOT_ALIGN = 16
EXPERT_GROUP = 4


def _params(*sem):
    return pltpu.CompilerParams(dimension_semantics=sem, vmem_limit_bytes=VMEM_LIMIT)


def _rms_mod(x, g, shift, scale):
    r = lax.rsqrt(jnp.mean(x * x, axis=-1, keepdims=True) + EPS)
    return (x * r) * g * (1.0 + scale) + shift


def _rms(x, g):
    return x * lax.rsqrt(jnp.mean(x * x, axis=-1, keepdims=True) + EPS) * g


def _gelu_tanh(x):
    return 0.5 * x * (1.0 + jnp.tanh(np.sqrt(2.0 / np.pi).astype(np.float32) * (x + 0.044715 * (x * x * x))))


def _silu(x):
    return x * jax.nn.sigmoid(x)


def _bf16_terms(x, n_terms):
    terms = []
    for _ in range(n_terms):
        t = x.astype(BF16)
        terms.append(t)
        x = x - t.astype(F32)
    return terms


def _mod_kernel(c_ref, w_ref, b_ref, o_ref):
    rows = c_ref.shape[0]
    s = jnp.concatenate(_bf16_terms(_silu(c_ref[...]), 3), axis=0)
    w_hi, w_lo = _bf16_terms(w_ref[...], 2)
    hi = jnp.dot(s, w_hi, preferred_element_type=F32)
    lo = jnp.dot(s[:2 * rows], w_lo, preferred_element_type=F32)
    small = (hi[2 * rows:] + lo[rows:]) + (hi[rows:2 * rows] + lo[:rows])
    o_ref[...] = (small + hi[:rows]) + b_ref[...]


def _modulation(cc, w_mod, b_mod):
    rows, d = cc.shape
    n = w_mod.shape[1]
    tn = 1024
    return pl.pallas_call(
        _mod_kernel,
        grid=(n // tn,),
        in_specs=[pl.BlockSpec((rows, d), lambda j: (0, 0)),
                  pl.BlockSpec((d, tn), lambda j: (0, j)),
                  pl.BlockSpec((1, tn), lambda j: (0, j))],
        out_specs=pl.BlockSpec((rows, tn), lambda j: (0, j)),
        out_shape=jax.ShapeDtypeStruct((rows, n), F32),
        compiler_params=_params("arbitrary"),
        name="modulation",
    )(cc, w_mod, b_mod)


def _in_kernel(x_ref, mod_ref, n1_ref, w_ref, ws_ref, bs_ref, gn_ref, onb_ref,
               q_ref, k_ref, v_ref, ob_ref, sp_ref, *, sub):
    mod = mod_ref[0]
    tiles = [slice(r, r + sub) for r in range(0, x_ref.shape[1], sub)]
    nch = sub // CHUNK
    lane = lax.broadcasted_iota(jnp.int32, (CHUNK, LANES), 1)
    first = lane < SG_GROUP_DIM

    def project(rs):
        hb = _rms_mod(x_ref[0, rs], n1_ref[...], mod[0:1], mod[1:2]).astype(BF16)
        qkv = jnp.dot(hb, w_ref[:, :3 * D_NA], preferred_element_type=F32)
        q_ref[0, rs] = (qkv[:, :D_NA] * (HEAD_DIM ** -0.5 * LOG2E)).astype(BF16)
        k_ref[0, rs] = qkv[:, D_NA:2 * D_NA].astype(BF16)
        v_ref[0, rs] = qkv[:, 2 * D_NA:].astype(BF16)
        return jnp.dot(hb, w_ref[:, 3 * D_NA:], preferred_element_type=F32)

    def gate_mlp(rs, uz):
        u = _gelu_tanh(uz[:, :D_SG])
        z = _gelu_tanh(uz[:, D_SG:])
        mu = jnp.mean(z, axis=-1, keepdims=True)
        zc = z - mu
        var = jnp.mean(zc * zc, axis=-1, keepdims=True)
        zb = (zc * lax.rsqrt(var + EPS) * gn_ref[...]).astype(BF16)
        for p in range(N_GROUPS_SG // 2):
            zp = jnp.concatenate(
                [zb[c * CHUNK:(c + 1) * CHUNK, p * LANES:(p + 1) * LANES] for c in range(nch)], axis=1)
            r = jnp.dot(ws_ref[p], zp, preferred_element_type=F32)
            bs = bs_ref[p]
            for c in range(nch):
                top = r[:CHUNK, c * LANES:(c + 1) * LANES] + bs[:CHUNK]
                bot = r[CHUNK:, c * LANES:(c + 1) * LANES] + bs[CHUNK:]
                sp_ref[rs.start + c * CHUNK:rs.start + (c + 1) * CHUNK, p * LANES:(p + 1) * LANES] = (
                    jnp.where(first, top, bot))
        ob = u * sp_ref[rs, :]
        ob_ref[0, rs] = _rms(ob, onb_ref[...]).astype(BF16)

    uz = {0: project(tiles[0])}
    for i, rs in enumerate(tiles):
        if i + 1 < len(tiles):
            uz[i + 1] = project(tiles[i + 1])
        gate_mlp(rs, uz.pop(i))


def _in_proj(x, mod, norm1, w_in_b, ws2, bs2, gmlp_norm, out_norm_b, tm=2048, sub=512):
    b, n, d = x.shape
    d_in = w_in_b.shape[1]
    full2 = lambda i, j: (0, 0)
    full3 = lambda i, j: (0, 0, 0)
    tile = lambda i, j: (i, j, 0)
    act = jax.ShapeDtypeStruct((b, n, D_NA), BF16)
    return pl.pallas_call(
        functools.partial(_in_kernel, sub=sub),
        grid=(b, n // tm),
        in_specs=[pl.BlockSpec((1, tm, d), tile),
                  pl.BlockSpec((1, N_MOD, d), lambda i, j: (i, 0, 0)),
                  pl.BlockSpec((1, d), full2),
                  pl.BlockSpec((d, d_in), full2),
                  pl.BlockSpec(ws2.shape, full3),
                  pl.BlockSpec(bs2.shape, full3),
                  pl.BlockSpec((1, D_SG), full2),
                  pl.BlockSpec((1, D_SG), full2)],
        out_specs=[pl.BlockSpec((1, tm, D_NA), tile)] * 4,
        out_shape=[act] * 4,
        scratch_shapes=[pltpu.VMEM((tm, D_SG), F32)],
        compiler_params=_params("parallel", "arbitrary"),
        name="in_proj_gmlp",
    )(x, mod, norm1, w_in_b, ws2, bs2, gmlp_norm, out_norm_b)


def _ctx_kernel(x_ref, mod_ref, n1_ref, wk_ref, wv_ref, k_ref, v_ref):
    mod = mod_ref[0]
    hb = _rms_mod(x_ref[...], n1_ref[...], mod[0:1], mod[1:2]).astype(BF16)
    k_ref[...] = jnp.dot(hb, wk_ref[...], preferred_element_type=F32).astype(BF16)
    v_ref[...] = jnp.dot(hb, wv_ref[...], preferred_element_type=F32).astype(BF16)


def _ctx_proj(ctx2, mod, mod_row, norm1, w_in_b, tm=512):
    rows, d = ctx2.shape
    act = jax.ShapeDtypeStruct((rows, D_NA), BF16)
    return pl.pallas_call(
        _ctx_kernel,
        grid=(rows // tm,),
        in_specs=[pl.BlockSpec((tm, d), lambda i: (i, 0)),
                  pl.BlockSpec((1, N_MOD, d), lambda i: (mod_row, 0, 0)),
                  pl.BlockSpec((1, d), lambda i: (0, 0)),
                  pl.BlockSpec((d, D_NA), lambda i: (0, 1)),
                  pl.BlockSpec((d, D_NA), lambda i: (0, 2))],
        out_specs=[pl.BlockSpec((tm, D_NA), lambda i: (i, 0))] * 2,
        out_shape=[act, act],
        compiler_params=_params("arbitrary"),
        name="ctx_kv_proj",
    )(ctx2, mod, norm1, w_in_b, w_in_b)


def _window_start(t, rows):
    return jnp.clip(Q_ROWS * t - NA_KH // 2, 0, rows - W_ROWS)


def _attn_kernel(tbl_ref, q_ref, k_ref, v_ref, kc_ref, vc_ref, pb_ref, ona_ref, o_ref, *, rows):
    n_tiles = rows // Q_ROWS
    kc = kc_ref[0]
    vc = vc_ref[0]
    lane = lax.broadcasted_iota(jnp.int32, (TQ, LANES), 1)
    first = lane < HEAD_DIM
    nt = (((1,), (1,)), ((), ()))
    tiles = []
    for tt in range(TILES_PER_STEP):
        t = pl.program_id(1) * TILES_PER_STEP + tt
        ws = pl.multiple_of(_window_start(t, rows) * GRID_W, GRID_W)
        cls = jnp.where(t == 0, 0, jnp.where(t == n_tiles - 1, 2, 1))
        pair = [[tbl_ref[(cls * Q_ROWS + r) * (W_ROWS // 2) + j] for j in range(W_ROWS // 2)]
                for r in range(Q_ROWS)]
        tiles.append((q_ref[0, tt * TQ:(tt + 1) * TQ, :], k_ref[0, pl.ds(ws, TW), :], v_ref[0, pl.ds(ws, TW), :], pair))
    units = [(tt, h) for tt in range(TILES_PER_STEP) for h in range(N_HEADS_NA)]

    def operands(tt, h):
        q, kw, vw, _ = tiles[tt]
        sl = slice((h // 2) * LANES, (h // 2 + 1) * LANES)
        return q[:, sl], kw[:, sl], vw[:, sl], kc[:, sl], vc[:, sl]

    def scores(tt, h):
        qp, kp, _, kcp, _ = operands(tt, h)
        pair = tiles[tt][3]
        qm = jnp.where(first if h % 2 == 0 else jnp.logical_not(first), qp, jnp.zeros_like(qp))
        bias = jnp.concatenate(
            [jnp.concatenate([pb_ref[h, pair[r][c]] for c in range(W_ROWS // 2)], axis=1)
             for r in range(Q_ROWS)], axis=0)
        s_nb = lax.dot_general(qm, kp, nt, preferred_element_type=F32) + bias
        s_cx = lax.dot_general(qm, kcp, nt, preferred_element_type=F32)
        return s_nb, s_cx

    def softmax(s_nb, s_cx):
        m = jnp.maximum(jnp.max(s_nb, axis=1, keepdims=True), jnp.max(s_cx, axis=1, keepdims=True))
        e_nb = jnp.exp2(s_nb - m)
        e_cx = jnp.exp2(s_cx - m)
        l = jnp.sum(e_nb, axis=1, keepdims=True) + jnp.sum(e_cx, axis=1, keepdims=True)
        return e_nb.astype(BF16), e_cx.astype(BF16), l

    def values(tt, h, e_nb, e_cx, l):
        _, _, vp, _, vcp = operands(tt, h)
        o = jnp.dot(e_nb, vp, preferred_element_type=F32) + jnp.dot(e_cx, vcp, preferred_element_type=F32)
        return o / l

    s, p, o = {}, {}, {}
    n_u = len(units)
    for step in range(n_u + 2):
        if step < n_u:
            s[step] = scores(*units[step])
        if 0 <= step - 1 < n_u:
            p[step - 1] = softmax(*s.pop(step - 1))
        if 0 <= step - 2 < n_u:
            o[step - 2] = values(*units[step - 2], *p.pop(step - 2))
    for tt in range(TILES_PER_STEP):
        oh = [o[tt * N_HEADS_NA + h] for h in range(N_HEADS_NA)]
        out = jnp.concatenate([jnp.where(first, oh[h], oh[h + 1]) for h in range(0, N_HEADS_NA, 2)], axis=1)
        o_ref[0, tt * TQ:(tt + 1) * TQ, :] = _rms(out, ona_ref[...]).astype(BF16)


def _bias_blocks(rpb, rows):
    n_tiles = rows // Q_ROWS
    n_ro, n_co = 2 * NA_KH - 1, 2 * NA_KW - 1
    qc = np.arange(GRID_W)[:, None]
    kc = np.arange(GRID_W)[None, :]
    cs = np.clip(qc - NA_KW // 2, 0, GRID_W - NA_KW)
    col_ok = (kc >= cs) & (kc < cs + NA_KW)
    spread = ((kc - qc + NA_KW - 1)[None] == np.arange(n_co)[:, None, None]) & col_ok[None]
    blocks = jnp.dot(rpb.reshape(-1, n_co), jnp.asarray(spread.reshape(n_co, -1), F32), precision=HIGHEST)
    blocks = jnp.where(jnp.asarray(col_ok.reshape(-1)), blocks * LOG2E, NEG_INF)
    blocks = blocks.reshape(N_HEADS_NA, n_ro, GRID_W, GRID_W)
    masked = jnp.full((N_HEADS_NA, 1, GRID_W, GRID_W), NEG_INF, F32)
    ext = jnp.concatenate([blocks, masked], axis=1)
    pairs, table = [], []
    for t in (0, 1, n_tiles - 1):
        ws = int(np.clip(Q_ROWS * t - NA_KH // 2, 0, rows - W_ROWS))
        for rho in range(Q_ROWS):
            r = Q_ROWS * t + rho
            rs = int(np.clip(r - NA_KH // 2, 0, rows - NA_KH))
            off = [kr - r + NA_KH - 1 if rs <= kr < rs + NA_KH else n_ro for kr in range(ws, ws + W_ROWS)]
            for c in range(W_ROWS // 2):
                pr = (off[2 * c], off[2 * c + 1])
                if pr not in pairs:
                    pairs.append(pr)
                table.append(pairs.index(pr))
    pb = jnp.stack([jnp.concatenate([ext[:, a], ext[:, b]], axis=-1) for a, b in pairs], axis=1)
    return pb, jnp.asarray(np.asarray(table, np.int32))


def _attention(q, k, v, kc, vc, pb, table, out_norm_a):
    b, n, _ = q.shape
    rows = n // GRID_W
    n_tiles = rows // Q_ROWS
    per_b = lambda i, t, *_: (i, 0, 0)
    return pl.pallas_call(
        functools.partial(_attn_kernel, rows=rows),
        grid_spec=pltpu.PrefetchScalarGridSpec(
            num_scalar_prefetch=1,
            grid=(b, n_tiles // TILES_PER_STEP),
            in_specs=[pl.BlockSpec((1, TILES_PER_STEP * TQ, D_NA), lambda i, t, *_: (i, t, 0)),
                      pl.BlockSpec((1, n, D_NA), per_b),
                      pl.BlockSpec((1, n, D_NA), per_b),
                      pl.BlockSpec((1, CTX_LEN, D_NA), per_b),
                      pl.BlockSpec((1, CTX_LEN, D_NA), per_b),
                      pl.BlockSpec(pb.shape, lambda i, t, *_: (0, 0, 0, 0)),
                      pl.BlockSpec((1, D_NA), lambda i, t, *_: (0, 0))],
            out_specs=pl.BlockSpec((1, TILES_PER_STEP * TQ, D_NA), lambda i, t, *_: (i, t, 0))),
        out_shape=jax.ShapeDtypeStruct((b, n, D_NA), BF16),
        compiler_params=_params("parallel", "arbitrary"),
        name="nbr_attention",
    )(table, q, k, v, kc, vc, pb, out_norm_a)


def _out_kernel(oa_ref, ob_ref, x_ref, mod_ref, w_ref, n2_ref, wr_ref, xn_ref, h2_ref, aff_ref, *, sub):
    mod = mod_ref[0]
    nt = (((1,), (1,)), ((), ()))
    w_hi, w_lo = _bf16_terms(wr_ref[...], 2)
    w2 = jnp.concatenate([w_hi, w_lo], axis=0)
    tiles = [slice(r, r + sub) for r in range(0, x_ref.shape[1], sub)]
    mixes = [jnp.dot(oa_ref[0, rs], w_ref[:D_NA], preferred_element_type=F32)
             + jnp.dot(ob_ref[0, rs], w_ref[D_NA:], preferred_element_type=F32) for rs in tiles]
    for rs, mix in zip(tiles, mixes):
        xn = x_ref[0, rs] + mod[2:3] * mix
        xn_ref[0, rs] = xn
        h2 = _rms_mod(xn, n2_ref[...], mod[3:4], mod[4:5])
        h_hi, h_lo = _bf16_terms(h2, 2)
        h2_ref[0, rs] = h_hi
        l_hi = lax.dot_general(w2, h_hi, nt, preferred_element_type=F32)
        l_lo = lax.dot_general(w_hi, h_lo, nt, preferred_element_type=F32)
        logits = l_hi[:N_EXPERTS] + l_hi[N_EXPERTS:] + l_lo
        e = jnp.exp(logits - jnp.max(logits, axis=0, keepdims=True))
        aff_ref[0, :, rs] = e / jnp.sum(e, axis=0, keepdims=True)


def _out_proj(oa, ob, x, mod, w_out_b, norm2, w_router_t, tm=1024, sub=512):
    b, n, d = x.shape
    tile = lambda i, j: (i, j, 0)
    full = lambda i, j: (0, 0)
    return pl.pallas_call(
        functools.partial(_out_kernel, sub=sub),
        grid=(b, n // tm),
        in_specs=[pl.BlockSpec((1, tm, D_NA), tile),
                  pl.BlockSpec((1, tm, D_SG), tile),
                  pl.BlockSpec((1, tm, d), tile),
                  pl.BlockSpec((1, N_MOD, d), lambda i, j: (i, 0, 0)),
                  pl.BlockSpec(w_out_b.shape, full),
                  pl.BlockSpec((1, d), full),
                  pl.BlockSpec((N_EXPERTS, d), full)],
        out_specs=[pl.BlockSpec((1, tm, d), tile),
                   pl.BlockSpec((1, tm, d), tile),
                   pl.BlockSpec((1, N_EXPERTS, tm), lambda i, j: (i, 0, j))],
        out_shape=[jax.ShapeDtypeStruct((b, n, d), F32),
                   jax.ShapeDtypeStruct((b, n, d), BF16),
                   jax.ShapeDtypeStruct((b, N_EXPERTS, n), F32)],
        compiler_params=_params("parallel", "arbitrary"),
        name="out_proj_router",
    )(oa, ob, x, mod, w_out_b, norm2, w_router_t)


def _prefix_count(mask_f, tri):
    rows, n = mask_f.shape
    parts = []
    carry = jnp.zeros((rows, 1), F32)
    for j in range(n // LANES):
        blk = mask_f[:, j * LANES:(j + 1) * LANES]
        parts.append(jnp.dot(blk.astype(BF16), tri, preferred_element_type=F32) + carry)
        carry = carry + jnp.sum(blk, axis=1, keepdims=True)
    return jnp.concatenate(parts, axis=1)


def _topk_kernel(aff_ref, slot_ref, eb_ref, *, cap):
    a = aff_ref[...]
    rows = a.shape[0]

    def enough(t):
        return jnp.sum(jnp.where(a >= t, 1.0, 0.0), axis=1, keepdims=True) >= cap

    tiny = jnp.full((rows, 1), 2.0 ** -126, F32)
    normal = enough(tiny)
    pw = tiny
    hi = jnp.full((rows, 1), 4.0, F32)
    for bit in range(6, -1, -1):
        cand = pw * (2.0 ** (1 << bit))
        ok = enough(cand)
        pw = jnp.where(ok, cand, pw)
        hi = jnp.where(ok, hi, cand)
    lo = jnp.where(normal, pw, 0.0)
    hi = jnp.where(normal, hi, tiny)
    step = lo
    for _ in range(MANTISSA_STEPS):
        step = step * 0.5
        cand = lo + step
        ok = enough(cand)
        lo = jnp.where(ok, cand, lo)
        hi = jnp.where(ok, hi, cand)
    above = a >= hi
    tie = jnp.logical_and(a >= lo, jnp.logical_not(above))
    n_above = jnp.sum(jnp.where(above, 1.0, 0.0), axis=1, keepdims=True)
    ri = lax.broadcasted_iota(jnp.int32, (LANES, LANES), 0)
    ci = lax.broadcasted_iota(jnp.int32, (LANES, LANES), 1)
    tri = jnp.where(ri <= ci, 1.0, 0.0).astype(BF16)
    tie_rank = _prefix_count(jnp.where(tie, 1.0, 0.0), tri)
    sel = jnp.logical_or(above, jnp.logical_and(tie, tie_rank <= cap - n_above))
    sel_f = jnp.where(sel, 1.0, 0.0)
    pos = _prefix_count(sel_f, tri) - 1.0
    slot_ref[...] = jnp.where(sel, pos, -1.0).astype(jnp.int32)
    lane = lax.broadcasted_iota(jnp.int32, (rows, LANES), 1)
    cnt = jnp.zeros((rows, LANES), F32)
    for j in range(a.shape[1] // TCH):
        cnt = jnp.where(lane == j, jnp.sum(sel_f[:, j * TCH:(j + 1) * TCH], axis=1, keepdims=True), cnt)
    before = jnp.where(ri < ci, 1.0, 0.0).astype(BF16)
    eb_ref[...] = jnp.dot(cnt.astype(BF16), before, preferred_element_type=F32).astype(jnp.int32)


def _topk_slots(aff2, cap):
    rows, n = aff2.shape
    tr = rows
    return pl.pallas_call(
        functools.partial(_topk_kernel, cap=cap),
        grid=(rows // tr,),
        in_specs=[pl.BlockSpec((tr, n), lambda i: (i, 0))],
        out_specs=[pl.BlockSpec((tr, n), lambda i: (i, 0)), pl.BlockSpec((tr, LANES), lambda i: (i, 0))],
        out_shape=[jax.ShapeDtypeStruct((rows, n), jnp.int32), jax.ShapeDtypeStruct((rows, LANES), jnp.int32)],
        compiler_params=_params("parallel"),
        name="expert_topk",
    )(aff2)


def _slot_window(eb_ref, bi, e, j, n_e, n_chunks, cap):
    base = (bi * n_e + e) * (n_chunks + 1) + j
    s0 = eb_ref[base]
    s1 = eb_ref[base + 1]
    start = jnp.minimum(s0 & -SLOT_ALIGN, cap - SLOT_W)
    return s0, s1, pl.multiple_of(start, SLOT_ALIGN)


def _window_overflow(eb_ref, bi, j, n_e, n_chunks, cap):
    over = None
    for e in range(n_e):
        _, s1, start = _slot_window(eb_ref, bi, e, j, n_e, n_chunks, cap)
        o = s1 > start + SLOT_W
        over = o if over is None else jnp.logical_or(over, o)
    return over


def _gather_kernel(eb_ref, h2_ref, slot_ref, aff_ref, xs_ref, gate_ref, *, cap, jc, n_chunks):
    bi = pl.program_id(0)
    jo = pl.program_id(1)
    n_e = slot_ref.shape[1]
    wi = lax.broadcasted_iota(jnp.int32, (SLOT_W, TCH), 0)
    wcol = lax.broadcasted_iota(jnp.int32, (SLOT_W, 1), 0)

    @pl.when(jo == 0)
    def _():
        xs_ref[...] = jnp.zeros_like(xs_ref)
        gate_ref[...] = jnp.zeros_like(gate_ref)

    for jj in range(jc):
        j = jo * jc + jj
        tok = slice(jj * TCH, (jj + 1) * TCH)
        h2c = h2_ref[0, tok, :]
        for g in range(n_e // EXPERT_GROUP):
            blocks, meta = [], []
            for q in range(EXPERT_GROUP):
                e = EXPERT_GROUP * g + q
                s0, s1, start = _slot_window(eb_ref, bi, e, j, n_e, n_chunks, cap)
                hit = slot_ref[0, e:e + 1, tok] - start == wi
                blocks.append(jnp.where(hit, 1.0, 0.0).astype(BF16))
                gate = jnp.sum(jnp.where(hit, aff_ref[0, e:e + 1, tok], 0.0), axis=1, keepdims=True)
                meta.append((e, s0, s1, start, gate))
            onehot = jnp.concatenate(blocks, axis=0)
            rows = jnp.dot(onehot, h2c, preferred_element_type=F32).astype(BF16)
            for q, (e, s0, s1, start, gate) in enumerate(meta):
                own = jnp.logical_and(wcol + start >= s0, wcol + start < s1)
                win = pl.ds(start, SLOT_W)
                xs_ref[0, e, win, :] = jnp.where(own, rows[q * SLOT_W:(q + 1) * SLOT_W], xs_ref[0, e, win, :])
                gate_ref[0, e, win, :] = jnp.where(own, gate, gate_ref[0, e, win, :])

    for jj in range(jc):
        j = jo * jc + jj
        tok = slice(jj * TCH, (jj + 1) * TCH)

        @pl.when(_window_overflow(eb_ref, bi, j, n_e, n_chunks, cap))
        def _():
            h2c = h2_ref[0, tok, :]
            ci = lax.broadcasted_iota(jnp.int32, (cap, TCH), 0)
            ccol = lax.broadcasted_iota(jnp.int32, (cap, 1), 0)
            for e in range(n_e):
                s0, s1, _ = _slot_window(eb_ref, bi, e, j, n_e, n_chunks, cap)
                hit = slot_ref[0, e:e + 1, tok] == ci
                rows = jnp.dot(jnp.where(hit, 1.0, 0.0).astype(BF16), h2c, preferred_element_type=F32).astype(BF16)
                gate = jnp.sum(jnp.where(hit, aff_ref[0, e:e + 1, tok], 0.0), axis=1, keepdims=True)
                own = jnp.logical_and(ccol >= s0, ccol < s1)
                xs_ref[0, e] = jnp.where(own, rows, xs_ref[0, e])
                gate_ref[0, e] = jnp.where(own, gate, gate_ref[0, e])


def _gather_tokens(ebound, h2, slot, aff, cap, jc=4):
    b, n, d = h2.shape
    e = slot.shape[1]
    n_chunks = n // TCH
    assert n_chunks % jc == 0
    tok = lambda i, j, *_: (i, j, 0)
    per_b = lambda i, j, *_: (i, 0, 0, 0)
    return pl.pallas_call(
        functools.partial(_gather_kernel, cap=cap, jc=jc, n_chunks=n_chunks),
        grid_spec=pltpu.PrefetchScalarGridSpec(
            num_scalar_prefetch=1,
            grid=(b, n_chunks // jc),
            in_specs=[pl.BlockSpec((1, jc * TCH, d), tok),
                      pl.BlockSpec((1, e, jc * TCH), lambda i, j, *_: (i, 0, j)),
                      pl.BlockSpec((1, e, jc * TCH), lambda i, j, *_: (i, 0, j))],
            out_specs=[pl.BlockSpec((1, e, cap, d), per_b),
                       pl.BlockSpec((1, e, cap, 1), per_b)]),
        out_shape=[jax.ShapeDtypeStruct((b, e, cap, d), BF16),
                   jax.ShapeDtypeStruct((b, e, cap, 1), F32)],
        compiler_params=_params("parallel", "arbitrary"),
        name="moe_gather",
    )(ebound, h2, slot, aff)


def _expert_kernel(xs_ref, wg_ref, wu_ref, wd_ref, gate_ref, y_ref, acc_ref, *, rb, n_f):
    f = pl.program_id(1)
    nb, _, cap, d = xs_ref.shape

    def step(first, last):
        wg = wg_ref[0].astype(BF16)
        wu = wu_ref[0].astype(BF16)
        wd = wd_ref[0].astype(BF16)
        for i in range(nb // rb):
            blk = slice(i * rb, (i + 1) * rb)
            xs = xs_ref[blk, 0].reshape(rb * cap, d)
            a = jnp.dot(xs, wg, preferred_element_type=F32)
            u = jnp.dot(xs, wu, preferred_element_type=F32)
            hm = (_silu(a) * u).astype(BF16)
            part = jnp.dot(hm, wd, preferred_element_type=F32).reshape(rb, cap, d)
            total = part if first else acc_ref[blk] + part
            if last:
                y_ref[blk, 0] = (total * gate_ref[blk, 0]).astype(BF16)
            else:
                acc_ref[blk] = total

    if n_f == 1:
        step(True, True)
    else:
        pl.when(f == 0)(lambda: step(True, False))
        if n_f > 2:
            pl.when(jnp.logical_and(f > 0, f < n_f - 1))(lambda: step(False, False))
        pl.when(f == n_f - 1)(lambda: step(False, True))


def _experts(xs, w_gate, w_up, w_down, gate, fc=768, rb=4):
    b, e, cap, d = xs.shape
    dff = w_gate.shape[2]
    per_e = lambda i, f: (0, i, 0, 0)
    return pl.pallas_call(
        functools.partial(_expert_kernel, rb=rb, n_f=dff // fc),
        grid=(e, dff // fc),
        in_specs=[pl.BlockSpec((b, 1, cap, d), per_e),
                  pl.BlockSpec((1, d, fc), lambda i, f: (i, 0, f)),
                  pl.BlockSpec((1, d, fc), lambda i, f: (i, 0, f)),
                  pl.BlockSpec((1, fc, d), lambda i, f: (i, f, 0)),
                  pl.BlockSpec((b, 1, cap, 1), per_e)],
        out_specs=pl.BlockSpec((b, 1, cap, d), per_e),
        out_shape=jax.ShapeDtypeStruct((b, e, cap, d), BF16),
        scratch_shapes=[pltpu.VMEM((b, cap, d), F32)],
        compiler_params=_params("parallel", "arbitrary"),
        name="moe_experts",
    )(xs, w_gate, w_up, w_down, gate)


def _combine_kernel(eb_ref, slot_t_ref, y_ref, xn_ref, mod_ref, nf_ref, o_ref, *, cap, jc, n_chunks):
    bi = pl.program_id(0)
    jo = pl.program_id(1)
    n_e = y_ref.shape[1]
    ci = lax.broadcasted_iota(jnp.int32, (TCH, EXPERT_GROUP * SLOT_W), 1)
    toks = [slice(jj * TCH, (jj + 1) * TCH) for jj in range(jc)]

    def finish(tok, moe):
        x = xn_ref[0, tok] + mod_ref[0][5:6] * moe
        o_ref[0, tok] = _rms(x, nf_ref[...])

    moes = []
    for jj, tok in enumerate(toks):
        j = jo * jc + jj
        st = slot_t_ref[0, tok, :]
        s_blocks, y_blocks = [], []
        for g in range(n_e // EXPERT_GROUP):
            col = None
            for q in reversed(range(EXPERT_GROUP)):
                e = EXPERT_GROUP * g + q
                _, _, start = _slot_window(eb_ref, bi, e, j, n_e, n_chunks, cap)
                rel = st[:, e:e + 1] - start
                tgt = jnp.where(jnp.logical_and(rel >= 0, rel < SLOT_W), rel + q * SLOT_W, -1)
                col = tgt if col is None else jnp.where(ci < (q + 1) * SLOT_W, tgt, col)
                y_blocks.insert(g * EXPERT_GROUP, y_ref[0, e, pl.ds(start, SLOT_W), :])
            s_blocks.append(jnp.where(col == ci, 1.0, 0.0).astype(BF16))
        scat = jnp.concatenate(s_blocks, axis=1)
        ywin = jnp.concatenate(y_blocks, axis=0)
        moes.append(jnp.dot(scat, ywin, preferred_element_type=F32))
    for tok, moe in zip(toks, moes):
        finish(tok, moe)

    for jj, tok in enumerate(toks):
        @pl.when(_window_overflow(eb_ref, bi, jo * jc + jj, n_e, n_chunks, cap))
        def _():
            st = slot_t_ref[0, tok, :]
            cf = lax.broadcasted_iota(jnp.int32, (TCH, cap), 1)
            dense = jnp.concatenate(
                [jnp.where(st[:, e:e + 1] == cf, 1.0, 0.0).astype(BF16) for e in range(n_e)], axis=1)
            finish(tok, jnp.dot(dense, y_ref[0].reshape(n_e * cap, y_ref.shape[3]),
                                preferred_element_type=F32))


def _combine(ebound, slot_t, y, x_new, mod, norm_final, cap, jc=4):
    b, n, d = x_new.shape
    e = slot_t.shape[2]
    n_chunks = n // TCH
    assert n_chunks % jc == 0
    tile = lambda i, j, *_: (i, j, 0)
    return pl.pallas_call(
        functools.partial(_combine_kernel, cap=cap, jc=jc, n_chunks=n_chunks),
        grid_spec=pltpu.PrefetchScalarGridSpec(
            num_scalar_prefetch=1,
            grid=(b, n_chunks // jc),
            in_specs=[pl.BlockSpec((1, jc * TCH, e), tile),
                      pl.BlockSpec((1, e, cap, d), lambda i, j, *_: (i, 0, 0, 0)),
                      pl.BlockSpec((1, jc * TCH, d), tile),
                      pl.BlockSpec((1, N_MOD, d), lambda i, j, *_: (i, 0, 0)),
                      pl.BlockSpec((1, d), lambda i, j, *_: (0, 0))],
            out_specs=pl.BlockSpec((1, jc * TCH, d), tile)),
        out_shape=jax.ShapeDtypeStruct((b, n, d), F32),
        compiler_params=_params("parallel", "arbitrary"),
        name="moe_combine_norm",
    )(ebound, slot_t, y, x_new, mod, norm_final)


def kernel(x, c, ctx, c_ctx, w_mod, b_mod, norm1, w_in, rpb, w_s, b_s, gmlp_norm, out_norm_a, out_norm_b,
           w_out, norm2, w_router, w_gate, w_up, w_down, norm_final):
    b, n, d = x.shape
    assert w_mod.shape[0] == 1, "single-layer stack only"
    assert n % (GRID_W * Q_ROWS * TILES_PER_STEP) == 0 and n // GRID_W >= W_ROWS
    assert n % TCH == 0 and N_EXPERTS % EXPERT_GROUP == 0
    cap = EC_CAPACITY_FACTOR * n // N_EXPERTS

    pad = (-(b + 1)) % 8
    cc = jnp.concatenate([c, c_ctx[None], jnp.zeros((pad, d), F32)], axis=0)
    m = _modulation(cc, w_mod[0], b_mod[0][None])
    mod = m.reshape(-1, N_MOD, d)

    w_in_b = w_in[0].astype(BF16)
    ws2 = w_s[0].astype(BF16).reshape(N_GROUPS_SG // 2, 2 * CHUNK, CHUNK)
    bs2 = jnp.broadcast_to(b_s[0].reshape(N_GROUPS_SG // 2, 2 * CHUNK, 1), (N_GROUPS_SG // 2, 2 * CHUNK, LANES))
    q, k, v, ob = _in_proj(x, mod, norm1, w_in_b, ws2, bs2, gmlp_norm, out_norm_b)

    kc, vc = _ctx_proj(ctx.reshape(b * CTX_LEN, d), mod, b, norm1, w_in_b)
    kc = kc.reshape(b, CTX_LEN, D_NA)
    vc = vc.reshape(b, CTX_LEN, D_NA)

    pb, table = _bias_blocks(rpb[0], n // GRID_W)
    oa = _attention(q, k, v, kc, vc, pb, table, out_norm_a)

    x_new, h2, aff = _out_proj(oa, ob, x, mod, w_out[0].astype(BF16), norm2, w_router[0].T)

    slot, ebound = _topk_slots(aff.reshape(b * N_EXPERTS, n), cap)
    slot = slot.reshape(b, N_EXPERTS, n)
    ebound = ebound[:, :n // TCH + 1].reshape(-1)

    xs, gate = _gather_tokens(ebound, h2, slot, aff, cap)
    y = _experts(xs, w_gate[0], w_up[0], w_down[0], gate)
    return _combine(ebound, jnp.swapaxes(slot, 1, 2), y, x_new, mod, norm_final[None], cap)
```

```python
import functools

import numpy as np
import jax
import jax.numpy as jnp
from jax import lax
from jax.experimental import pallas as pl
from jax.experimental.pallas import tpu as pltpu

D_MODEL = 1024
GRID_W = 64
CTX_LEN = 256
N_HEADS_NA = 8
HEAD_DIM = 64
D_NA = N_HEADS_NA * HEAD_DIM
NA_KH = 8
NA_KW = 16
D_SG = D_MODEL - D_NA
N_GROUPS_SG = 8
SG_GROUP_DIM = D_SG // N_GROUPS_SG
CHUNK = 128
N_EXPERTS = 16
EC_CAPACITY_FACTOR = 2
D_EXPERT = 1536
N_MOD = 6
EPS = 1e-6
NEG_INF = -1e30
LOG2E = float(np.log2(np.e))

LANES = 128
VMEM_LIMIT = 56 * 1024 * 1024

F32 = jnp.float32
BF16 = jnp.bfloat16
HIGHEST = lax.Precision.HIGHEST

ROWS_PER_STEP = 8

MANTISSA_STEPS = 36

TCH = 256
SLOT_W = 64
SLOT_ALIGN = 16
EXPERT_GROUP = 4


def _params(*sem):
    return pltpu.CompilerParams(dimension_semantics=sem, vmem_limit_bytes=VMEM_LIMIT)


def _rms_mod(x, g, shift, scale):
    r = lax.rsqrt(jnp.mean(x * x, axis=-1, keepdims=True) + EPS)
    return (x * r) * g * (1.0 + scale) + shift


def _rms(x, g):
    return x * lax.rsqrt(jnp.mean(x * x, axis=-1, keepdims=True) + EPS) * g


def _gelu_tanh(x):
    return 0.5 * x * (1.0 + jnp.tanh(np.sqrt(2.0 / np.pi).astype(np.float32) * (x + 0.044715 * (x * x * x))))


def _silu(x):
    return x * jax.nn.sigmoid(x)


def _bf16_terms(x, n_terms):
    terms = []
    for _ in range(n_terms):
        t = x.astype(BF16)
        terms.append(t)
        x = x - t.astype(F32)
    return terms


def _mod_kernel(c_ref, w_ref, b_ref, o_ref):
    rows = c_ref.shape[0]
    s = jnp.concatenate(_bf16_terms(_silu(c_ref[...]), 3), axis=0)
    w_hi, w_lo = _bf16_terms(w_ref[...], 2)
    hi = jnp.dot(s, w_hi, preferred_element_type=F32)
    lo = jnp.dot(s[:2 * rows], w_lo, preferred_element_type=F32)
    small = (hi[2 * rows:] + lo[rows:]) + (hi[rows:2 * rows] + lo[:rows])
    o_ref[...] = (small + hi[:rows]) + b_ref[...]


def _modulation(cc, w_mod, b_mod):
    rows, d = cc.shape
    n = w_mod.shape[1]
    tn = 1024
    return pl.pallas_call(
        _mod_kernel,
        grid=(n // tn,),
        in_specs=[pl.BlockSpec((rows, d), lambda j: (0, 0)),
                  pl.BlockSpec((d, tn), lambda j: (0, j)),
                  pl.BlockSpec((1, tn), lambda j: (0, j))],
        out_specs=pl.BlockSpec((rows, tn), lambda j: (0, j)),
        out_shape=jax.ShapeDtypeStruct((rows, n), F32),
        compiler_params=_params("arbitrary"),
        name="modulation",
    )(cc, w_mod, b_mod)


def _in_kernel(x_ref, mod_ref, n1_ref, w_ref, ws_ref, bs_ref, gn_ref, onb_ref,
               q_ref, k_ref, v_ref, ob_ref, sp_ref, *, sub):
    mod = mod_ref[0]
    tiles = [slice(r, r + sub) for r in range(0, x_ref.shape[1], sub)]
    nch = sub // CHUNK
    lane = lax.broadcasted_iota(jnp.int32, (CHUNK, LANES), 1)
    first = lane < SG_GROUP_DIM

    def project(rs):
        hb = _rms_mod(x_ref[0, rs], n1_ref[...], mod[0:1], mod[1:2]).astype(BF16)
        qkv = jnp.dot(hb, w_ref[:, :3 * D_NA], preferred_element_type=F32)
        q_ref[0, rs] = (qkv[:, :D_NA] * (HEAD_DIM ** -0.5 * LOG2E)).astype(BF16)
        k_ref[0, rs] = qkv[:, D_NA:2 * D_NA].astype(BF16)
        v_ref[0, rs] = qkv[:, 2 * D_NA:].astype(BF16)
        return jnp.dot(hb, w_ref[:, 3 * D_NA:], preferred_element_type=F32)

    def gate_mlp(rs, uz):
        u = _gelu_tanh(uz[:, :D_SG])
        z = _gelu_tanh(uz[:, D_SG:])
        mu = jnp.mean(z, axis=-1, keepdims=True)
        zc = z - mu
        var = jnp.mean(zc * zc, axis=-1, keepdims=True)
        zb = (zc * lax.rsqrt(var + EPS) * gn_ref[...]).astype(BF16)
        for p in range(N_GROUPS_SG // 2):
            zp = jnp.concatenate(
                [zb[c * CHUNK:(c + 1) * CHUNK, p * LANES:(p + 1) * LANES] for c in range(nch)], axis=1)
            r = jnp.dot(ws_ref[p], zp, preferred_element_type=F32)
            bs = bs_ref[p]
            for c in range(nch):
                top = r[:CHUNK, c * LANES:(c + 1) * LANES] + bs[:CHUNK]
                bot = r[CHUNK:, c * LANES:(c + 1) * LANES] + bs[CHUNK:]
                sp_ref[rs.start + c * CHUNK:rs.start + (c + 1) * CHUNK, p * LANES:(p + 1) * LANES] = (
                    jnp.where(first, top, bot))
        ob = u * sp_ref[rs, :]
        ob_ref[0, rs] = _rms(ob, onb_ref[...]).astype(BF16)

    uz = {0: project(tiles[0])}
    for i, rs in enumerate(tiles):
        if i + 1 < len(tiles):
            uz[i + 1] = project(tiles[i + 1])
        gate_mlp(rs, uz.pop(i))


def _in_proj(x, mod, norm1, w_in_b, ws2, bs2, gmlp_norm, out_norm_b, tm=2048, sub=512):
    b, n, d = x.shape
    d_in = w_in_b.shape[1]
    full2 = lambda i, j: (0, 0)
    full3 = lambda i, j: (0, 0, 0)
    tile = lambda i, j: (i, j, 0)
    act = jax.ShapeDtypeStruct((b, n, D_NA), BF16)
    return pl.pallas_call(
        functools.partial(_in_kernel, sub=sub),
        grid=(b, n // tm),
        in_specs=[pl.BlockSpec((1, tm, d), tile),
                  pl.BlockSpec((1, N_MOD, d), lambda i, j: (i, 0, 0)),
                  pl.BlockSpec((1, d), full2),
                  pl.BlockSpec((d, d_in), full2),
                  pl.BlockSpec(ws2.shape, full3),
                  pl.BlockSpec(bs2.shape, full3),
                  pl.BlockSpec((1, D_SG), full2),
                  pl.BlockSpec((1, D_SG), full2)],
        out_specs=[pl.BlockSpec((1, tm, D_NA), tile)] * 4,
        out_shape=[act] * 4,
        scratch_shapes=[pltpu.VMEM((tm, D_SG), F32)],
        compiler_params=_params("parallel", "arbitrary"),
        name="in_proj_gmlp",
    )(x, mod, norm1, w_in_b, ws2, bs2, gmlp_norm, out_norm_b)


def _ctx_kernel(x_ref, mod_ref, n1_ref, wk_ref, wv_ref, k_ref, v_ref):
    mod = mod_ref[0]
    hb = _rms_mod(x_ref[...], n1_ref[...], mod[0:1], mod[1:2]).astype(BF16)
    k_ref[...] = jnp.dot(hb, wk_ref[...], preferred_element_type=F32).astype(BF16)
    v_ref[...] = jnp.dot(hb, wv_ref[...], preferred_element_type=F32).astype(BF16)


def _ctx_proj(ctx2, mod, mod_row, norm1, w_in_b, tm=512):
    rows, d = ctx2.shape
    act = jax.ShapeDtypeStruct((rows, D_NA), BF16)
    return pl.pallas_call(
        _ctx_kernel,
        grid=(rows // tm,),
        in_specs=[pl.BlockSpec((tm, d), lambda i: (i, 0)),
                  pl.BlockSpec((1, N_MOD, d), lambda i: (mod_row, 0, 0)),
                  pl.BlockSpec((1, d), lambda i: (0, 0)),
                  pl.BlockSpec((d, D_NA), lambda i: (0, 1)),
                  pl.BlockSpec((d, D_NA), lambda i: (0, 2))],
        out_specs=[pl.BlockSpec((tm, D_NA), lambda i: (i, 0))] * 2,
        out_shape=[act, act],
        compiler_params=_params("arbitrary"),
        name="ctx_kv_proj",
    )(ctx2, mod, norm1, w_in_b, w_in_b)


def _attn_kernel(q_ref, k_ref, v_ref, kc_ref, vc_ref, pb_ref, ona_ref, o_ref, *, rows):
    kc = kc_ref[0]
    vc = vc_ref[0]
    nk = NA_KH * GRID_W
    lane = lax.broadcasted_iota(jnp.int32, (GRID_W, LANES), 1)
    first = lane < HEAD_DIM
    nt = (((1,), (1,)), ((), ()))
    units = []
    for r in range(ROWS_PER_STEP):
        row = pl.program_id(1) * ROWS_PER_STEP + r
        rs = jnp.clip(row - NA_KH // 2, 0, rows - NA_KH)
        start = pl.multiple_of(rs * GRID_W, GRID_W)
        off = rs - row + NA_KH - 1
        units += [(r, start, off, p) for p in range(N_HEADS_NA // 2)]

    def scores(r, start, off, p):
        sl = slice(p * LANES, (p + 1) * LANES)
        qp = q_ref[0, r * GRID_W:(r + 1) * GRID_W, sl]
        q2 = jnp.concatenate([jnp.where(first, qp, jnp.zeros_like(qp)),
                              jnp.where(first, jnp.zeros_like(qp), qp)], axis=0)
        bias = jnp.concatenate(
            [jnp.concatenate([pb_ref[2 * p + j, off + 2 * c] for c in range(NA_KH // 2)], axis=1)
             for j in range(2)], axis=0)
        s_nb = lax.dot_general(q2, k_ref[0, pl.ds(start, nk), sl], nt, preferred_element_type=F32) + bias
        s_cx = lax.dot_general(q2, kc[:, sl], nt, preferred_element_type=F32)
        return s_nb, s_cx

    def softmax(s_nb, s_cx):
        m = jnp.maximum(jnp.max(s_nb, axis=1, keepdims=True), jnp.max(s_cx, axis=1, keepdims=True))
        e_nb = jnp.exp2(s_nb - m)
        e_cx = jnp.exp2(s_cx - m)
        l = jnp.sum(e_nb, axis=1, keepdims=True) + jnp.sum(e_cx, axis=1, keepdims=True)
        return e_nb.astype(BF16), e_cx.astype(BF16), l

    def values(r, start, off, p, e_nb, e_cx, l):
        sl = slice(p * LANES, (p + 1) * LANES)
        o2 = (jnp.dot(e_nb, v_ref[0, pl.ds(start, nk), sl], preferred_element_type=F32)
              + jnp.dot(e_cx, vc[:, sl], preferred_element_type=F32)) / l
        return jnp.where(first, o2[:GRID_W], o2[GRID_W:])

    s, pr, o = {}, {}, {}
    n_u = len(units)
    for step in range(n_u + 2):
        if step < n_u:
            s[step] = scores(*units[step])
        if 0 <= step - 1 < n_u:
            pr[step - 1] = softmax(*s.pop(step - 1))
        if 0 <= step - 2 < n_u:
            o[step - 2] = values(*units[step - 2], *pr.pop(step - 2))
    n_p = N_HEADS_NA // 2
    for r in range(ROWS_PER_STEP):
        out = jnp.concatenate([o[r * n_p + p] for p in range(n_p)], axis=1)
        o_ref[0, r * GRID_W:(r + 1) * GRID_W, :] = _rms(out, ona_ref[...]).astype(BF16)


def _bias_blocks(rpb):
    n_ro, n_co = 2 * NA_KH - 1, 2 * NA_KW - 1
    qc = np.arange(GRID_W)[:, None]
    kc = np.arange(GRID_W)[None, :]
    cs = np.clip(qc - NA_KW // 2, 0, GRID_W - NA_KW)
    col_ok = (kc >= cs) & (kc < cs + NA_KW)
    spread = ((kc - qc + NA_KW - 1)[None] == np.arange(n_co)[:, None, None]) & col_ok[None]
    blocks = jnp.dot(rpb.reshape(-1, n_co), jnp.asarray(spread.reshape(n_co, -1), F32), precision=HIGHEST)
    blocks = jnp.where(jnp.asarray(col_ok.reshape(-1)), blocks * LOG2E, NEG_INF)
    blocks = blocks.reshape(N_HEADS_NA, n_ro, GRID_W, GRID_W)
    return jnp.concatenate([blocks[:, :-1], blocks[:, 1:]], axis=-1)


def _attention(q, k, v, kc, vc, pb, out_norm_a):
    b, n, _ = q.shape
    rows = n // GRID_W
    tq = ROWS_PER_STEP * GRID_W
    per_b = lambda i, t: (i, 0, 0)
    return pl.pallas_call(
        functools.partial(_attn_kernel, rows=rows),
        grid=(b, n // tq),
        in_specs=[pl.BlockSpec((1, tq, D_NA), lambda i, t: (i, t, 0)),
                  pl.BlockSpec((1, n, D_NA), per_b),
                  pl.BlockSpec((1, n, D_NA), per_b),
                  pl.BlockSpec((1, CTX_LEN, D_NA), per_b),
                  pl.BlockSpec((1, CTX_LEN, D_NA), per_b),
                  pl.BlockSpec(pb.shape, lambda i, t: (0, 0, 0, 0)),
                  pl.BlockSpec((1, D_NA), lambda i, t: (0, 0))],
        out_specs=pl.BlockSpec((1, tq, D_NA), lambda i, t: (i, t, 0)),
        out_shape=jax.ShapeDtypeStruct((b, n, D_NA), BF16),
        compiler_params=_params("parallel", "arbitrary"),
        name="nbr_attention",
    )(q, k, v, kc, vc, pb, out_norm_a)


def _out_kernel(oa_ref, ob_ref, x_ref, mod_ref, w_ref, n2_ref, wr_ref, xn_ref, h2_ref, aff_ref, *, sub):
    mod = mod_ref[0]
    nt = (((1,), (1,)), ((), ()))
    w_hi, w_lo = _bf16_terms(wr_ref[...], 2)
    w2 = jnp.concatenate([w_hi, w_lo], axis=0)
    tiles = [slice(r, r + sub) for r in range(0, x_ref.shape[1], sub)]
    mixes = [jnp.dot(oa_ref[0, rs], w_ref[:D_NA], preferred_element_type=F32)
             + jnp.dot(ob_ref[0, rs], w_ref[D_NA:], preferred_element_type=F32) for rs in tiles]
    for rs, mix in zip(tiles, mixes):
        xn = x_ref[0, rs] + mod[2:3] * mix
        xn_ref[0, rs] = xn
        h2 = _rms_mod(xn, n2_ref[...], mod[3:4], mod[4:5])
        h_hi, h_lo = _bf16_terms(h2, 2)
        h2_ref[0, rs] = h_hi
        l_hi = lax.dot_general(w2, h_hi, nt, preferred_element_type=F32)
        l_lo = lax.dot_general(w_hi, h_lo, nt, preferred_element_type=F32)
        logits = l_hi[:N_EXPERTS] + l_hi[N_EXPERTS:] + l_lo
        e = jnp.exp(logits - jnp.max(logits, axis=0, keepdims=True))
        aff_ref[0, :, rs] = e / jnp.sum(e, axis=0, keepdims=True)


def _out_proj(oa, ob, x, mod, w_out_b, norm2, w_router_t, tm=1024, sub=512):
    b, n, d = x.shape
    tile = lambda i, j: (i, j, 0)
    full = lambda i, j: (0, 0)
    return pl.pallas_call(
        functools.partial(_out_kernel, sub=sub),
        grid=(b, n // tm),
        in_specs=[pl.BlockSpec((1, tm, D_NA), tile),
                  pl.BlockSpec((1, tm, D_SG), tile),
                  pl.BlockSpec((1, tm, d), tile),
                  pl.BlockSpec((1, N_MOD, d), lambda i, j: (i, 0, 0)),
                  pl.BlockSpec(w_out_b.shape, full),
                  pl.BlockSpec((1, d), full),
                  pl.BlockSpec((N_EXPERTS, d), full)],
        out_specs=[pl.BlockSpec((1, tm, d), tile),
                   pl.BlockSpec((1, tm, d), tile),
                   pl.BlockSpec((1, N_EXPERTS, tm), lambda i, j: (i, 0, j))],
        out_shape=[jax.ShapeDtypeStruct((b, n, d), F32),
                   jax.ShapeDtypeStruct((b, n, d), BF16),
                   jax.ShapeDtypeStruct((b, N_EXPERTS, n), F32)],
        compiler_params=_params("parallel", "arbitrary"),
        name="out_proj_router",
    )(oa, ob, x, mod, w_out_b, norm2, w_router_t)


def _prefix_count(mask_f, tri):
    rows, n = mask_f.shape
    parts = []
    carry = jnp.zeros((rows, 1), F32)
    for j in range(n // LANES):
        blk = mask_f[:, j * LANES:(j + 1) * LANES]
        parts.append(jnp.dot(blk.astype(BF16), tri, preferred_element_type=F32) + carry)
        carry = carry + jnp.sum(blk, axis=1, keepdims=True)
    return jnp.concatenate(parts, axis=1)


def _topk_kernel(aff_ref, slot_ref, eb_ref, *, cap):
    a = aff_ref[...]
    rows = a.shape[0]

    def enough(t):
        return jnp.sum(jnp.where(a >= t, 1.0, 0.0), axis=1, keepdims=True) >= cap

    tiny = jnp.full((rows, 1), float(np.finfo(np.float32).tiny), F32)
    normal = enough(tiny)
    pw = tiny
    hi = jnp.full((rows, 1), 4.0, F32)
    for bit in range(6, -1, -1):
        cand = pw * (2.0 ** (1 << bit))
        ok = enough(cand)
        pw = jnp.where(ok, cand, pw)
        hi = jnp.where(ok, hi, cand)
    lo = jnp.where(normal, pw, 0.0)
    hi = jnp.where(normal, hi, tiny)
    step = lo
    for _ in range(MANTISSA_STEPS):
        step = step * 0.5
        cand = lo + step
        ok = enough(cand)
        lo = jnp.where(ok, cand, lo)
        hi = jnp.where(ok, hi, cand)
    above = a >= hi
    tie = jnp.logical_and(a >= lo, jnp.logical_not(above))
    n_above = jnp.sum(jnp.where(above, 1.0, 0.0), axis=1, keepdims=True)
    ri = lax.broadcasted_iota(jnp.int32, (LANES, LANES), 0)
    ci = lax.broadcasted_iota(jnp.int32, (LANES, LANES), 1)
    tri = jnp.where(ri <= ci, 1.0, 0.0).astype(BF16)
    tie_rank = _prefix_count(jnp.where(tie, 1.0, 0.0), tri)
    sel = jnp.logical_or(above, jnp.logical_and(tie, tie_rank <= cap - n_above))
    sel_f = jnp.where(sel, 1.0, 0.0)
    pos = _prefix_count(sel_f, tri) - 1.0
    slot_ref[...] = jnp.where(sel, pos, -1.0).astype(jnp.int32)
    lane = lax.broadcasted_iota(jnp.int32, (rows, LANES), 1)
    cnt = jnp.zeros((rows, LANES), F32)
    for j in range(a.shape[1] // TCH):
        cnt = jnp.where(lane == j, jnp.sum(sel_f[:, j * TCH:(j + 1) * TCH], axis=1, keepdims=True), cnt)
    before = jnp.where(ri < ci, 1.0, 0.0).astype(BF16)
    eb_ref[...] = jnp.dot(cnt.astype(BF16), before, preferred_element_type=F32).astype(jnp.int32)


def _topk_slots(aff2, cap):
    rows, n = aff2.shape
    tr = rows
    return pl.pallas_call(
        functools.partial(_topk_kernel, cap=cap),
        grid=(rows // tr,),
        in_specs=[pl.BlockSpec((tr, n), lambda i: (i, 0))],
        out_specs=[pl.BlockSpec((tr, n), lambda i: (i, 0)), pl.BlockSpec((tr, LANES), lambda i: (i, 0))],
        out_shape=[jax.ShapeDtypeStruct((rows, n), jnp.int32), jax.ShapeDtypeStruct((rows, LANES), jnp.int32)],
        compiler_params=_params("parallel"),
        name="expert_topk",
    )(aff2)


def _slot_window(eb_ref, bi, e, j, n_e, n_chunks, cap):
    base = (bi * n_e + e) * (n_chunks + 1) + j
    s0 = eb_ref[base]
    s1 = eb_ref[base + 1]
    start = jnp.minimum(s0 & -SLOT_ALIGN, cap - SLOT_W)
    return s0, s1, pl.multiple_of(start, SLOT_ALIGN)


def _window_overflow(eb_ref, bi, j, n_e, n_chunks, cap):
    over = None
    for e in range(n_e):
        _, s1, start = _slot_window(eb_ref, bi, e, j, n_e, n_chunks, cap)
        o = s1 > start + SLOT_W
        over = o if over is None else jnp.logical_or(over, o)
    return over


def _gather_kernel(eb_ref, h2_ref, slot_ref, aff_ref, xs_ref, gate_ref, *, cap, jc, n_chunks):
    bi = pl.program_id(0)
    jo = pl.program_id(1)
    n_e = slot_ref.shape[1]
    wi = lax.broadcasted_iota(jnp.int32, (SLOT_W, TCH), 0)
    wcol = lax.broadcasted_iota(jnp.int32, (SLOT_W, 1), 0)

    @pl.when(jo == 0)
    def _():
        xs_ref[...] = jnp.zeros_like(xs_ref)
        gate_ref[...] = jnp.zeros_like(gate_ref)

    for jj in range(jc):
        j = jo * jc + jj
        tok = slice(jj * TCH, (jj + 1) * TCH)
        h2c = h2_ref[0, tok, :]
        for g in range(n_e // EXPERT_GROUP):
            blocks, meta = [], []
            for q in range(EXPERT_GROUP):
                e = EXPERT_GROUP * g + q
                s0, s1, start = _slot_window(eb_ref, bi, e, j, n_e, n_chunks, cap)
                hit = slot_ref[0, e:e + 1, tok] - start == wi
                blocks.append(jnp.where(hit, 1.0, 0.0).astype(BF16))
                gate = jnp.sum(jnp.where(hit, aff_ref[0, e:e + 1, tok], 0.0), axis=1, keepdims=True)
                meta.append((e, s0, s1, start, gate))
            onehot = jnp.concatenate(blocks, axis=0)
            rows = jnp.dot(onehot, h2c, preferred_element_type=F32).astype(BF16)
            for q, (e, s0, s1, start, gate) in enumerate(meta):
                own = jnp.logical_and(wcol + start >= s0, wcol + start < s1)
                win = pl.ds(start, SLOT_W)
                xs_ref[0, e, win, :] = jnp.where(own, rows[q * SLOT_W:(q + 1) * SLOT_W], xs_ref[0, e, win, :])
                gate_ref[0, e, win, :] = jnp.where(own, gate, gate_ref[0, e, win, :])

    for jj in range(jc):
        j = jo * jc + jj
        tok = slice(jj * TCH, (jj + 1) * TCH)

        @pl.when(_window_overflow(eb_ref, bi, j, n_e, n_chunks, cap))
        def _():
            h2c = h2_ref[0, tok, :]
            ci = lax.broadcasted_iota(jnp.int32, (cap, TCH), 0)
            ccol = lax.broadcasted_iota(jnp.int32, (cap, 1), 0)
            for e in range(n_e):
                s0, s1, _ = _slot_window(eb_ref, bi, e, j, n_e, n_chunks, cap)
                hit = slot_ref[0, e:e + 1, tok] == ci
                rows = jnp.dot(jnp.where(hit, 1.0, 0.0).astype(BF16), h2c, preferred_element_type=F32).astype(BF16)
                gate = jnp.sum(jnp.where(hit, aff_ref[0, e:e + 1, tok], 0.0), axis=1, keepdims=True)
                own = jnp.logical_and(ccol >= s0, ccol < s1)
                xs_ref[0, e] = jnp.where(own, rows, xs_ref[0, e])
                gate_ref[0, e] = jnp.where(own, gate, gate_ref[0, e])


def _gather_tokens(ebound, h2, slot, aff, cap, jc=4):
    b, n, d = h2.shape
    e = slot.shape[1]
    n_chunks = n // TCH
    assert n_chunks % jc == 0
    tok = lambda i, j, *_: (i, j, 0)
    per_b = lambda i, j, *_: (i, 0, 0, 0)
    return pl.pallas_call(
        functools.partial(_gather_kernel, cap=cap, jc=jc, n_chunks=n_chunks),
        grid_spec=pltpu.PrefetchScalarGridSpec(
            num_scalar_prefetch=1,
            grid=(b, n_chunks // jc),
            in_specs=[pl.BlockSpec((1, jc * TCH, d), tok),
                      pl.BlockSpec((1, e, jc * TCH), lambda i, j, *_: (i, 0, j)),
                      pl.BlockSpec((1, e, jc * TCH), lambda i, j, *_: (i, 0, j))],
            out_specs=[pl.BlockSpec((1, e, cap, d), per_b),
                       pl.BlockSpec((1, e, cap, 1), per_b)]),
        out_shape=[jax.ShapeDtypeStruct((b, e, cap, d), BF16),
                   jax.ShapeDtypeStruct((b, e, cap, 1), F32)],
        compiler_params=_params("parallel", "arbitrary"),
        name="moe_gather",
    )(ebound, h2, slot, aff)


def _expert_kernel(xs_ref, wg_ref, wu_ref, wd_ref, gate_ref, y_ref, acc_ref, *, rb, n_f):
    f = pl.program_id(1)
    nb, _, cap, d = xs_ref.shape

    def step(first, last):
        wg = wg_ref[0].astype(BF16)
        wu = wu_ref[0].astype(BF16)
        wd = wd_ref[0].astype(BF16)
        for i in range(nb // rb):
            blk = slice(i * rb, (i + 1) * rb)
            xs = xs_ref[blk, 0].reshape(rb * cap, d)
            a = jnp.dot(xs, wg, preferred_element_type=F32)
            u = jnp.dot(xs, wu, preferred_element_type=F32)
            hm = (_silu(a) * u).astype(BF16)
            part = jnp.dot(hm, wd, preferred_element_type=F32).reshape(rb, cap, d)
            total = part if first else acc_ref[blk] + part
            if last:
                y_ref[blk, 0] = (total * gate_ref[blk, 0]).astype(BF16)
            else:
                acc_ref[blk] = total

    if n_f == 1:
        step(True, True)
    else:
        pl.when(f == 0)(lambda: step(True, False))
        if n_f > 2:
            pl.when(jnp.logical_and(f > 0, f < n_f - 1))(lambda: step(False, False))
        pl.when(f == n_f - 1)(lambda: step(False, True))


def _experts(xs, w_gate, w_up, w_down, gate, fc=768, rb=4):
    b, e, cap, d = xs.shape
    dff = w_gate.shape[2]
    per_e = lambda i, f: (0, i, 0, 0)
    return pl.pallas_call(
        functools.partial(_expert_kernel, rb=rb, n_f=dff // fc),
        grid=(e, dff // fc),
        in_specs=[pl.BlockSpec((b, 1, cap, d), per_e),
                  pl.BlockSpec((1, d, fc), lambda i, f: (i, 0, f)),
                  pl.BlockSpec((1, d, fc), lambda i, f: (i, 0, f)),
                  pl.BlockSpec((1, fc, d), lambda i, f: (i, f, 0)),
                  pl.BlockSpec((b, 1, cap, 1), per_e)],
        out_specs=pl.BlockSpec((b, 1, cap, d), per_e),
        out_shape=jax.ShapeDtypeStruct((b, e, cap, d), BF16),
        scratch_shapes=[pltpu.VMEM((b, cap, d), F32)],
        compiler_params=_params("parallel", "arbitrary"),
        name="moe_experts",
    )(xs, w_gate, w_up, w_down, gate)


def _combine_kernel(eb_ref, slot_t_ref, y_ref, xn_ref, mod_ref, nf_ref, o_ref, *, cap, jc, n_chunks):
    bi = pl.program_id(0)
    jo = pl.program_id(1)
    n_e = y_ref.shape[1]
    ci = lax.broadcasted_iota(jnp.int32, (TCH, EXPERT_GROUP * SLOT_W), 1)
    toks = [slice(jj * TCH, (jj + 1) * TCH) for jj in range(jc)]

    def finish(tok, moe):
        x = xn_ref[0, tok] + mod_ref[0][5:6] * moe
        o_ref[0, tok] = _rms(x, nf_ref[...])

    moes = []
    for jj, tok in enumerate(toks):
        j = jo * jc + jj
        st = slot_t_ref[0, tok, :]
        s_blocks, y_blocks = [], []
        for g in range(n_e // EXPERT_GROUP):
            col = None
            for q in reversed(range(EXPERT_GROUP)):
                e = EXPERT_GROUP * g + q
                _, _, start = _slot_window(eb_ref, bi, e, j, n_e, n_chunks, cap)
                rel = st[:, e:e + 1] - start
                tgt = jnp.where(jnp.logical_and(rel >= 0, rel < SLOT_W), rel + q * SLOT_W, -1)
                col = tgt if col is None else jnp.where(ci < (q + 1) * SLOT_W, tgt, col)
                y_blocks.insert(g * EXPERT_GROUP, y_ref[0, e, pl.ds(start, SLOT_W), :])
            s_blocks.append(jnp.where(col == ci, 1.0, 0.0).astype(BF16))
        scat = jnp.concatenate(s_blocks, axis=1)
        ywin = jnp.concatenate(y_blocks, axis=0)
        moes.append(jnp.dot(scat, ywin, preferred_element_type=F32))
    for tok, moe in zip(toks, moes):
        finish(tok, moe)

    for jj, tok in enumerate(toks):
        @pl.when(_window_overflow(eb_ref, bi, jo * jc + jj, n_e, n_chunks, cap))
        def _():
            st = slot_t_ref[0, tok, :]
            cf = lax.broadcasted_iota(jnp.int32, (TCH, cap), 1)
            dense = jnp.concatenate(
                [jnp.where(st[:, e:e + 1] == cf, 1.0, 0.0).astype(BF16) for e in range(n_e)], axis=1)
            finish(tok, jnp.dot(dense, y_ref[0].reshape(n_e * cap, y_ref.shape[3]),
                                preferred_element_type=F32))


def _combine(ebound, slot_t, y, x_new, mod, norm_final, cap, jc=4):
    b, n, d = x_new.shape
    e = slot_t.shape[2]
    n_chunks = n // TCH
    assert n_chunks % jc == 0
    tile = lambda i, j, *_: (i, j, 0)
    return pl.pallas_call(
        functools.partial(_combine_kernel, cap=cap, jc=jc, n_chunks=n_chunks),
        grid_spec=pltpu.PrefetchScalarGridSpec(
            num_scalar_prefetch=1,
            grid=(b, n_chunks // jc),
            in_specs=[pl.BlockSpec((1, jc * TCH, e), tile),
                      pl.BlockSpec((1, e, cap, d), lambda i, j, *_: (i, 0, 0, 0)),
                      pl.BlockSpec((1, jc * TCH, d), tile),
                      pl.BlockSpec((1, N_MOD, d), lambda i, j, *_: (i, 0, 0)),
                      pl.BlockSpec((1, d), lambda i, j, *_: (0, 0))],
            out_specs=pl.BlockSpec((1, jc * TCH, d), tile)),
        out_shape=jax.ShapeDtypeStruct((b, n, d), F32),
        compiler_params=_params("parallel", "arbitrary"),
        name="moe_combine_norm",
    )(ebound, slot_t, y, x_new, mod, norm_final)


def kernel(x, c, ctx, c_ctx, w_mod, b_mod, norm1, w_in, rpb, w_s, b_s, gmlp_norm, out_norm_a, out_norm_b,
           w_out, norm2, w_router, w_gate, w_up, w_down, norm_final):
    b, n, d = x.shape
    assert w_mod.shape[0] == 1, "single-layer stack only"
    assert n % (GRID_W * ROWS_PER_STEP) == 0 and n // GRID_W >= NA_KH
    assert n % TCH == 0 and N_EXPERTS % EXPERT_GROUP == 0
    cap = EC_CAPACITY_FACTOR * n // N_EXPERTS

    pad = (-(b + 1)) % 8
    cc = jnp.concatenate([c, c_ctx[None], jnp.zeros((pad, d), F32)], axis=0)
    m = _modulation(cc, w_mod[0], b_mod[0][None])
    mod = m.reshape(-1, N_MOD, d)

    w_in_b = w_in[0].astype(BF16)
    ws2 = w_s[0].astype(BF16).reshape(N_GROUPS_SG // 2, 2 * CHUNK, CHUNK)
    bs2 = jnp.broadcast_to(b_s[0].reshape(N_GROUPS_SG // 2, 2 * CHUNK, 1), (N_GROUPS_SG // 2, 2 * CHUNK, LANES))
    q, k, v, ob = _in_proj(x, mod, norm1, w_in_b, ws2, bs2, gmlp_norm, out_norm_b)

    kc, vc = _ctx_proj(ctx.reshape(b * CTX_LEN, d), mod, b, norm1, w_in_b)
    kc = kc.reshape(b, CTX_LEN, D_NA)
    vc = vc.reshape(b, CTX_LEN, D_NA)

    oa = _attention(q, k, v, kc, vc, _bias_blocks(rpb[0]), out_norm_a)

    x_new, h2, aff = _out_proj(oa, ob, x, mod, w_out[0].astype(BF16), norm2, w_router[0].T)

    slot, ebound = _topk_slots(aff.reshape(b * N_EXPERTS, n), cap)
    slot = slot.reshape(b, N_EXPERTS, n)
    ebound = ebound[:, :n // TCH + 1].reshape(-1)

    xs, gate = _gather_tokens(ebound, h2, slot, aff, cap)
    y = _experts(xs, w_gate[0], w_up[0], w_down[0], gate)
    return _combine(ebound, jnp.swapaxes(slot, 1, 2), y, x_new, mod, norm_final[None], cap)
```

```python
import functools

import numpy as np
import jax
import jax.numpy as jnp
from jax import lax
from jax.experimental import pallas as pl
from jax.experimental.pallas import tpu as pltpu

D_MODEL = 1024
GRID_W = 64
CTX_LEN = 256
N_HEADS_NA = 8
HEAD_DIM = 64
D_NA = N_HEADS_NA * HEAD_DIM
NA_KH = 8
NA_KW = 16
D_SG = D_MODEL - D_NA
N_GROUPS_SG = 8
SG_GROUP_DIM = D_SG // N_GROUPS_SG
CHUNK = 128
N_EXPERTS = 16
EC_CAPACITY_FACTOR = 2
D_EXPERT = 1536
N_MOD = 6
EPS = 1e-6
NEG_INF = -1e30
LOG2E = float(np.log2(np.e))

LANES = 128
VMEM_LIMIT = 56 * 1024 * 1024

F32 = jnp.float32
BF16 = jnp.bfloat16
HIGHEST = lax.Precision.HIGHEST

ROWS_PER_STEP = 16

MANTISSA_STEPS = 36

TCH = 256
SLOT_W = 64
SLOT_ALIGN = 16
EXPERT_GROUP = 4


def _params(*sem):
    return pltpu.CompilerParams(dimension_semantics=sem, vmem_limit_bytes=VMEM_LIMIT)


def _rms_mod(x, g, shift, scale):
    r = lax.rsqrt(jnp.mean(x * x, axis=-1, keepdims=True) + EPS)
    return (x * r) * g * (1.0 + scale) + shift


def _rms(x, g):
    return x * lax.rsqrt(jnp.mean(x * x, axis=-1, keepdims=True) + EPS) * g


def _gelu_tanh(x):
    return 0.5 * x * (1.0 + jnp.tanh(np.sqrt(2.0 / np.pi).astype(np.float32) * (x + 0.044715 * (x * x * x))))


def _silu(x):
    return x * jax.nn.sigmoid(x)


def _bf16_terms(x, n_terms):
    terms = []
    for _ in range(n_terms):
        t = x.astype(BF16)
        terms.append(t)
        x = x - t.astype(F32)
    return terms


def _mod_kernel(c_ref, w_ref, b_ref, o_ref):
    rows = c_ref.shape[0]
    s = jnp.concatenate(_bf16_terms(_silu(c_ref[...]), 3), axis=0)
    w_hi, w_lo = _bf16_terms(w_ref[...], 2)
    hi = jnp.dot(s, w_hi, preferred_element_type=F32)
    lo = jnp.dot(s[:2 * rows], w_lo, preferred_element_type=F32)
    small = (hi[2 * rows:] + lo[rows:]) + (hi[rows:2 * rows] + lo[:rows])
    o_ref[...] = (small + hi[:rows]) + b_ref[...]


def _modulation(cc, w_mod, b_mod):
    rows, d = cc.shape
    n = w_mod.shape[1]
    tn = 1024
    return pl.pallas_call(
        _mod_kernel,
        grid=(n // tn,),
        in_specs=[pl.BlockSpec((rows, d), lambda j: (0, 0)),
                  pl.BlockSpec((d, tn), lambda j: (0, j)),
                  pl.BlockSpec((1, tn), lambda j: (0, j))],
        out_specs=pl.BlockSpec((rows, tn), lambda j: (0, j)),
        out_shape=jax.ShapeDtypeStruct((rows, n), F32),
        compiler_params=_params("arbitrary"),
        name="modulation",
    )(cc, w_mod, b_mod)


def _in_kernel(x_ref, mod_ref, n1_ref, w_ref, ws_ref, bs_ref, gn_ref, onb_ref,
               q_ref, k_ref, v_ref, ob_ref, sp_ref, *, sub):
    mod = mod_ref[0]
    tiles = [slice(r, r + sub) for r in range(0, x_ref.shape[1], sub)]
    nch = sub // CHUNK
    lane = lax.broadcasted_iota(jnp.int32, (CHUNK, LANES), 1)
    first = lane < SG_GROUP_DIM

    def project(rs):
        hb = _rms_mod(x_ref[0, rs], n1_ref[...], mod[0:1], mod[1:2]).astype(BF16)
        qkv = jnp.dot(hb, w_ref[:, :3 * D_NA], preferred_element_type=F32)
        q_ref[0, rs] = (qkv[:, :D_NA] * (HEAD_DIM ** -0.5 * LOG2E)).astype(BF16)
        k_ref[0, rs] = qkv[:, D_NA:2 * D_NA].astype(BF16)
        v_ref[0, rs] = qkv[:, 2 * D_NA:].astype(BF16)
        return jnp.dot(hb, w_ref[:, 3 * D_NA:], preferred_element_type=F32)

    def gate_mlp(rs, uz):
        u = _gelu_tanh(uz[:, :D_SG])
        z = _gelu_tanh(uz[:, D_SG:])
        mu = jnp.mean(z, axis=-1, keepdims=True)
        zc = z - mu
        var = jnp.mean(zc * zc, axis=-1, keepdims=True)
        zb = (zc * lax.rsqrt(var + EPS) * gn_ref[...]).astype(BF16)
        for p in range(N_GROUPS_SG // 2):
            zp = jnp.concatenate(
                [zb[c * CHUNK:(c + 1) * CHUNK, p * LANES:(p + 1) * LANES] for c in range(nch)], axis=1)
            r = jnp.dot(ws_ref[p], zp, preferred_element_type=F32)
            bs = bs_ref[p]
            for c in range(nch):
                top = r[:CHUNK, c * LANES:(c + 1) * LANES] + bs[:CHUNK]
                bot = r[CHUNK:, c * LANES:(c + 1) * LANES] + bs[CHUNK:]
                sp_ref[rs.start + c * CHUNK:rs.start + (c + 1) * CHUNK, p * LANES:(p + 1) * LANES] = (
                    jnp.where(first, top, bot))
        ob = u * sp_ref[rs, :]
        ob_ref[0, rs] = _rms(ob, onb_ref[...]).astype(BF16)

    uz = {0: project(tiles[0])}
    for i, rs in enumerate(tiles):
        if i + 1 < len(tiles):
            uz[i + 1] = project(tiles[i + 1])
        gate_mlp(rs, uz.pop(i))


def _in_proj(x, mod, norm1, w_in_b, ws2, bs2, gmlp_norm, out_norm_b, tm=2048, sub=512):
    b, n, d = x.shape
    d_in = w_in_b.shape[1]
    full2 = lambda i, j: (0, 0)
    full3 = lambda i, j: (0, 0, 0)
    tile = lambda i, j: (i, j, 0)
    act = jax.ShapeDtypeStruct((b, n, D_NA), BF16)
    return pl.pallas_call(
        functools.partial(_in_kernel, sub=sub),
        grid=(b, n // tm),
        in_specs=[pl.BlockSpec((1, tm, d), tile),
                  pl.BlockSpec((1, N_MOD, d), lambda i, j: (i, 0, 0)),
                  pl.BlockSpec((1, d), full2),
                  pl.BlockSpec((d, d_in), full2),
                  pl.BlockSpec(ws2.shape, full3),
                  pl.BlockSpec(bs2.shape, full3),
                  pl.BlockSpec((1, D_SG), full2),
                  pl.BlockSpec((1, D_SG), full2)],
        out_specs=[pl.BlockSpec((1, tm, D_NA), tile)] * 4,
        out_shape=[act] * 4,
        scratch_shapes=[pltpu.VMEM((tm, D_SG), F32)],
        compiler_params=_params("parallel", "arbitrary"),
        name="in_proj_gmlp",
    )(x, mod, norm1, w_in_b, ws2, bs2, gmlp_norm, out_norm_b)


def _ctx_kernel(x_ref, mod_ref, n1_ref, wk_ref, wv_ref, k_ref, v_ref):
    mod = mod_ref[0]
    hb = _rms_mod(x_ref[...], n1_ref[...], mod[0:1], mod[1:2]).astype(BF16)
    k_ref[...] = jnp.dot(hb, wk_ref[...], preferred_element_type=F32).astype(BF16)
    v_ref[...] = jnp.dot(hb, wv_ref[...], preferred_element_type=F32).astype(BF16)


def _ctx_proj(ctx2, mod, mod_row, norm1, w_in_b, tm=512):
    rows, d = ctx2.shape
    act = jax.ShapeDtypeStruct((rows, D_NA), BF16)
    return pl.pallas_call(
        _ctx_kernel,
        grid=(rows // tm,),
        in_specs=[pl.BlockSpec((tm, d), lambda i: (i, 0)),
                  pl.BlockSpec((1, N_MOD, d), lambda i: (mod_row, 0, 0)),
                  pl.BlockSpec((1, d), lambda i: (0, 0)),
                  pl.BlockSpec((d, D_NA), lambda i: (0, 1)),
                  pl.BlockSpec((d, D_NA), lambda i: (0, 2))],
        out_specs=[pl.BlockSpec((tm, D_NA), lambda i: (i, 0))] * 2,
        out_shape=[act, act],
        compiler_params=_params("arbitrary"),
        name="ctx_kv_proj",
    )(ctx2, mod, norm1, w_in_b, w_in_b)


def _attn_kernel(q_ref, k_ref, v_ref, kc_ref, vc_ref, pb_ref, ona_ref, o_ref, *, rows):
    kc = kc_ref[0]
    vc = vc_ref[0]
    nk = NA_KH * GRID_W
    lane = lax.broadcasted_iota(jnp.int32, (GRID_W, LANES), 1)
    first = lane < HEAD_DIM
    nt = (((1,), (1,)), ((), ()))
    units = []
    for r in range(ROWS_PER_STEP):
        row = pl.program_id(1) * ROWS_PER_STEP + r
        rs = jnp.clip(row - NA_KH // 2, 0, rows - NA_KH)
        start = pl.multiple_of(rs * GRID_W, GRID_W)
        off = rs - row + NA_KH - 1
        units += [(r, start, off, p) for p in range(N_HEADS_NA // 2)]

    def scores(r, start, off, p):
        sl = slice(p * LANES, (p + 1) * LANES)
        qp = q_ref[0, r * GRID_W:(r + 1) * GRID_W, sl]
        q2 = jnp.concatenate([jnp.where(first, qp, jnp.zeros_like(qp)),
                              jnp.where(first, jnp.zeros_like(qp), qp)], axis=0)
        bias = jnp.concatenate(
            [jnp.concatenate([pb_ref[2 * p + j, off + 2 * c] for c in range(NA_KH // 2)], axis=1)
             for j in range(2)], axis=0)
        s_nb = lax.dot_general(q2, k_ref[0, pl.ds(start, nk), sl], nt, preferred_element_type=F32) + bias
        s_cx = lax.dot_general(q2, kc[:, sl], nt, preferred_element_type=F32)
        return s_nb, s_cx

    def softmax(s_nb, s_cx):
        m = jnp.maximum(jnp.max(s_nb, axis=1, keepdims=True), jnp.max(s_cx, axis=1, keepdims=True))
        e_nb = jnp.exp2(s_nb - m)
        e_cx = jnp.exp2(s_cx - m)
        l = jnp.sum(e_nb, axis=1, keepdims=True) + jnp.sum(e_cx, axis=1, keepdims=True)
        return e_nb.astype(BF16), e_cx.astype(BF16), l

    def values(r, start, off, p, e_nb, e_cx, l):
        sl = slice(p * LANES, (p + 1) * LANES)
        o2 = (jnp.dot(e_nb, v_ref[0, pl.ds(start, nk), sl], preferred_element_type=F32)
              + jnp.dot(e_cx, vc[:, sl], preferred_element_type=F32)) / l
        return jnp.where(first, o2[:GRID_W], o2[GRID_W:])

    s, pr, o = {}, {}, {}
    n_u = len(units)
    for step in range(n_u + 2):
        if step < n_u:
            s[step] = scores(*units[step])
        if 0 <= step - 1 < n_u:
            pr[step - 1] = softmax(*s.pop(step - 1))
        if 0 <= step - 2 < n_u:
            o[step - 2] = values(*units[step - 2], *pr.pop(step - 2))
    n_p = N_HEADS_NA // 2
    for r in range(ROWS_PER_STEP):
        out = jnp.concatenate([o[r * n_p + p] for p in range(n_p)], axis=1)
        o_ref[0, r * GRID_W:(r + 1) * GRID_W, :] = _rms(out, ona_ref[...]).astype(BF16)


def _bias_blocks(rpb):
    n_ro, n_co = 2 * NA_KH - 1, 2 * NA_KW - 1
    qc = np.arange(GRID_W)[:, None]
    kc = np.arange(GRID_W)[None, :]
    cs = np.clip(qc - NA_KW // 2, 0, GRID_W - NA_KW)
    col_ok = (kc >= cs) & (kc < cs + NA_KW)
    spread = ((kc - qc + NA_KW - 1)[None] == np.arange(n_co)[:, None, None]) & col_ok[None]
    blocks = jnp.dot(rpb.reshape(-1, n_co), jnp.asarray(spread.reshape(n_co, -1), F32), precision=HIGHEST)
    blocks = jnp.where(jnp.asarray(col_ok.reshape(-1)), blocks * LOG2E, NEG_INF)
    blocks = blocks.reshape(N_HEADS_NA, n_ro, GRID_W, GRID_W)
    return jnp.concatenate([blocks[:, :-1], blocks[:, 1:]], axis=-1)


def _attention(q, k, v, kc, vc, pb, out_norm_a):
    b, n, _ = q.shape
    rows = n // GRID_W
    tq = ROWS_PER_STEP * GRID_W
    per_b = lambda i, t: (i, 0, 0)
    return pl.pallas_call(
        functools.partial(_attn_kernel, rows=rows),
        grid=(b, n // tq),
        in_specs=[pl.BlockSpec((1, tq, D_NA), lambda i, t: (i, t, 0)),
                  pl.BlockSpec((1, n, D_NA), per_b),
                  pl.BlockSpec((1, n, D_NA), per_b),
                  pl.BlockSpec((1, CTX_LEN, D_NA), per_b),
                  pl.BlockSpec((1, CTX_LEN, D_NA), per_b),
                  pl.BlockSpec(pb.shape, lambda i, t: (0, 0, 0, 0)),
                  pl.BlockSpec((1, D_NA), lambda i, t: (0, 0))],
        out_specs=pl.BlockSpec((1, tq, D_NA), lambda i, t: (i, t, 0)),
        out_shape=jax.ShapeDtypeStruct((b, n, D_NA), BF16),
        compiler_params=_params("parallel", "arbitrary"),
        name="nbr_attention",
    )(q, k, v, kc, vc, pb, out_norm_a)


def _out_kernel(oa_ref, ob_ref, x_ref, mod_ref, w_ref, n2_ref, wr_ref, xn_ref, h2_ref, aff_ref, *, sub):
    mod = mod_ref[0]
    nt = (((1,), (1,)), ((), ()))
    w_hi, w_lo = _bf16_terms(wr_ref[...], 2)
    w2 = jnp.concatenate([w_hi, w_lo], axis=0)
    tiles = [slice(r, r + sub) for r in range(0, x_ref.shape[1], sub)]
    mixes = [jnp.dot(oa_ref[0, rs], w_ref[:D_NA], preferred_element_type=F32)
             + jnp.dot(ob_ref[0, rs], w_ref[D_NA:], preferred_element_type=F32) for rs in tiles]
    for rs, mix in zip(tiles, mixes):
        xn = x_ref[0, rs] + mod[2:3] * mix
        xn_ref[0, rs] = xn
        h2 = _rms_mod(xn, n2_ref[...], mod[3:4], mod[4:5])
        h_hi, h_lo = _bf16_terms(h2, 2)
        h2_ref[0, rs] = h_hi
        l_hi = lax.dot_general(w2, h_hi, nt, preferred_element_type=F32)
        l_lo = lax.dot_general(w_hi, h_lo, nt, preferred_element_type=F32)
        logits = l_hi[:N_EXPERTS] + l_hi[N_EXPERTS:] + l_lo
        e = jnp.exp(logits - jnp.max(logits, axis=0, keepdims=True))
        aff_ref[0, :, rs] = e / jnp.sum(e, axis=0, keepdims=True)


def _out_proj(oa, ob, x, mod, w_out_b, norm2, w_router_t, tm=1024, sub=512):
    b, n, d = x.shape
    tile = lambda i, j: (i, j, 0)
    full = lambda i, j: (0, 0)
    return pl.pallas_call(
        functools.partial(_out_kernel, sub=sub),
        grid=(b, n // tm),
        in_specs=[pl.BlockSpec((1, tm, D_NA), tile),
                  pl.BlockSpec((1, tm, D_SG), tile),
                  pl.BlockSpec((1, tm, d), tile),
                  pl.BlockSpec((1, N_MOD, d), lambda i, j: (i, 0, 0)),
                  pl.BlockSpec(w_out_b.shape, full),
                  pl.BlockSpec((1, d), full),
                  pl.BlockSpec((N_EXPERTS, d), full)],
        out_specs=[pl.BlockSpec((1, tm, d), tile),
                   pl.BlockSpec((1, tm, d), tile),
                   pl.BlockSpec((1, N_EXPERTS, tm), lambda i, j: (i, 0, j))],
        out_shape=[jax.ShapeDtypeStruct((b, n, d), F32),
                   jax.ShapeDtypeStruct((b, n, d), BF16),
                   jax.ShapeDtypeStruct((b, N_EXPERTS, n), F32)],
        compiler_params=_params("parallel", "arbitrary"),
        name="out_proj_router",
    )(oa, ob, x, mod, w_out_b, norm2, w_router_t)


def _prefix_count(mask_f, tri):
    rows, n = mask_f.shape
    parts = []
    carry = jnp.zeros((rows, 1), F32)
    for j in range(n // LANES):
        blk = mask_f[:, j * LANES:(j + 1) * LANES]
        parts.append(jnp.dot(blk.astype(BF16), tri, preferred_element_type=F32) + carry)
        carry = carry + jnp.sum(blk, axis=1, keepdims=True)
    return jnp.concatenate(parts, axis=1)


def _topk_kernel(aff_ref, slot_ref, eb_ref, *, cap):
    a = aff_ref[...]
    rows = a.shape[0]

    def enough(t):
        return jnp.sum(jnp.where(a >= t, 1.0, 0.0), axis=1, keepdims=True) >= cap

    tiny = jnp.full((rows, 1), float(np.finfo(np.float32).tiny), F32)
    normal = enough(tiny)
    pw = tiny
    hi = jnp.full((rows, 1), 4.0, F32)
    for bit in range(6, -1, -1):
        cand = pw * (2.0 ** (1 << bit))
        ok = enough(cand)
        pw = jnp.where(ok, cand, pw)
        hi = jnp.where(ok, hi, cand)
    lo = jnp.where(normal, pw, 0.0)
    hi = jnp.where(normal, hi, tiny)
    step = lo
    for _ in range(MANTISSA_STEPS):
        step = step * 0.5
        cand = lo + step
        ok = enough(cand)
        lo = jnp.where(ok, cand, lo)
        hi = jnp.where(ok, hi, cand)
    above = a >= hi
    tie = jnp.logical_and(a >= lo, jnp.logical_not(above))
    n_above = jnp.sum(jnp.where(above, 1.0, 0.0), axis=1, keepdims=True)
    ri = lax.broadcasted_iota(jnp.int32, (LANES, LANES), 0)
    ci = lax.broadcasted_iota(jnp.int32, (LANES, LANES), 1)
    tri = jnp.where(ri <= ci, 1.0, 0.0).astype(BF16)
    tie_rank = _prefix_count(jnp.where(tie, 1.0, 0.0), tri)
    sel = jnp.logical_or(above, jnp.logical_and(tie, tie_rank <= cap - n_above))
    sel_f = jnp.where(sel, 1.0, 0.0)
    pos = _prefix_count(sel_f, tri) - 1.0
    slot_ref[...] = jnp.where(sel, pos, -1.0).astype(jnp.int32)
    lane = lax.broadcasted_iota(jnp.int32, (rows, LANES), 1)
    cnt = jnp.zeros((rows, LANES), F32)
    for j in range(a.shape[1] // TCH):
        cnt = jnp.where(lane == j, jnp.sum(sel_f[:, j * TCH:(j + 1) * TCH], axis=1, keepdims=True), cnt)
    before = jnp.where(ri < ci, 1.0, 0.0).astype(BF16)
    eb_ref[...] = jnp.dot(cnt.astype(BF16), before, preferred_element_type=F32).astype(jnp.int32)


def _topk_slots(aff2, cap):
    rows, n = aff2.shape
    tr = rows
    return pl.pallas_call(
        functools.partial(_topk_kernel, cap=cap),
        grid=(rows // tr,),
        in_specs=[pl.BlockSpec((tr, n), lambda i: (i, 0))],
        out_specs=[pl.BlockSpec((tr, n), lambda i: (i, 0)), pl.BlockSpec((tr, LANES), lambda i: (i, 0))],
        out_shape=[jax.ShapeDtypeStruct((rows, n), jnp.int32), jax.ShapeDtypeStruct((rows, LANES), jnp.int32)],
        compiler_params=_params("parallel"),
        name="expert_topk",
    )(aff2)


def _slot_window(eb_ref, bi, e, j, n_e, n_chunks, cap):
    base = (bi * n_e + e) * (n_chunks + 1) + j
    s0 = eb_ref[base]
    s1 = eb_ref[base + 1]
    start = jnp.minimum(s0 & -SLOT_ALIGN, cap - SLOT_W)
    return s0, s1, pl.multiple_of(start, SLOT_ALIGN)


def _window_overflow(eb_ref, bi, j, n_e, n_chunks, cap):
    over = None
    for e in range(n_e):
        _, s1, start = _slot_window(eb_ref, bi, e, j, n_e, n_chunks, cap)
        o = s1 > start + SLOT_W
        over = o if over is None else jnp.logical_or(over, o)
    return over


def _gather_kernel(eb_ref, h2_ref, slot_ref, aff_ref, xs_ref, gate_ref, *, cap, jc, n_chunks):
    bi = pl.program_id(0)
    jo = pl.program_id(1)
    n_e = slot_ref.shape[1]
    wi = lax.broadcasted_iota(jnp.int32, (SLOT_W, TCH), 0)
    wcol = lax.broadcasted_iota(jnp.int32, (SLOT_W, 1), 0)

    @pl.when(jo == 0)
    def _():
        xs_ref[...] = jnp.zeros_like(xs_ref)
        gate_ref[...] = jnp.zeros_like(gate_ref)

    for jj in range(jc):
        j = jo * jc + jj
        tok = slice(jj * TCH, (jj + 1) * TCH)
        h2c = h2_ref[0, tok, :]
        for g in range(n_e // EXPERT_GROUP):
            blocks, meta = [], []
            for q in range(EXPERT_GROUP):
                e = EXPERT_GROUP * g + q
                s0, s1, start = _slot_window(eb_ref, bi, e, j, n_e, n_chunks, cap)
                hit = slot_ref[0, e:e + 1, tok] - start == wi
                blocks.append(jnp.where(hit, 1.0, 0.0).astype(BF16))
                gate = jnp.sum(jnp.where(hit, aff_ref[0, e:e + 1, tok], 0.0), axis=1, keepdims=True)
                meta.append((e, s0, s1, start, gate))
            onehot = jnp.concatenate(blocks, axis=0)
            rows = jnp.dot(onehot, h2c, preferred_element_type=F32).astype(BF16)
            for q, (e, s0, s1, start, gate) in enumerate(meta):
                own = jnp.logical_and(wcol + start >= s0, wcol + start < s1)
                win = pl.ds(start, SLOT_W)
                xs_ref[0, e, win, :] = jnp.where(own, rows[q * SLOT_W:(q + 1) * SLOT_W], xs_ref[0, e, win, :])
                gate_ref[0, e, win, :] = jnp.where(own, gate, gate_ref[0, e, win, :])

    for jj in range(jc):
        j = jo * jc + jj
        tok = slice(jj * TCH, (jj + 1) * TCH)

        @pl.when(_window_overflow(eb_ref, bi, j, n_e, n_chunks, cap))
        def _():
            h2c = h2_ref[0, tok, :]
            ci = lax.broadcasted_iota(jnp.int32, (cap, TCH), 0)
            ccol = lax.broadcasted_iota(jnp.int32, (cap, 1), 0)
            for e in range(n_e):
                s0, s1, _ = _slot_window(eb_ref, bi, e, j, n_e, n_chunks, cap)
                hit = slot_ref[0, e:e + 1, tok] == ci
                rows = jnp.dot(jnp.where(hit, 1.0, 0.0).astype(BF16), h2c, preferred_element_type=F32).astype(BF16)
                gate = jnp.sum(jnp.where(hit, aff_ref[0, e:e + 1, tok], 0.0), axis=1, keepdims=True)
                own = jnp.logical_and(ccol >= s0, ccol < s1)
                xs_ref[0, e] = jnp.where(own, rows, xs_ref[0, e])
                gate_ref[0, e] = jnp.where(own, gate, gate_ref[0, e])


def _gather_tokens(ebound, h2, slot, aff, cap, jc=4):
    b, n, d = h2.shape
    e = slot.shape[1]
    n_chunks = n // TCH
    assert n_chunks % jc == 0
    tok = lambda i, j, *_: (i, j, 0)
    per_b = lambda i, j, *_: (i, 0, 0, 0)
    return pl.pallas_call(
        functools.partial(_gather_kernel, cap=cap, jc=jc, n_chunks=n_chunks),
        grid_spec=pltpu.PrefetchScalarGridSpec(
            num_scalar_prefetch=1,
            grid=(b, n_chunks // jc),
            in_specs=[pl.BlockSpec((1, jc * TCH, d), tok),
                      pl.BlockSpec((1, e, jc * TCH), lambda i, j, *_: (i, 0, j)),
                      pl.BlockSpec((1, e, jc * TCH), lambda i, j, *_: (i, 0, j))],
            out_specs=[pl.BlockSpec((1, e, cap, d), per_b),
                       pl.BlockSpec((1, e, cap, 1), per_b)]),
        out_shape=[jax.ShapeDtypeStruct((b, e, cap, d), BF16),
                   jax.ShapeDtypeStruct((b, e, cap, 1), F32)],
        compiler_params=_params("parallel", "arbitrary"),
        name="moe_gather",
    )(ebound, h2, slot, aff)


def _expert_kernel(xs_ref, wg_ref, wu_ref, wd_ref, gate_ref, y_ref, acc_ref, *, rb, n_f):
    f = pl.program_id(1)
    nb, _, cap, d = xs_ref.shape

    def step(first, last):
        wg = wg_ref[0].astype(BF16)
        wu = wu_ref[0].astype(BF16)
        wd = wd_ref[0].astype(BF16)
        for i in range(nb // rb):
            blk = slice(i * rb, (i + 1) * rb)
            xs = xs_ref[blk, 0].reshape(rb * cap, d)
            a = jnp.dot(xs, wg, preferred_element_type=F32)
            u = jnp.dot(xs, wu, preferred_element_type=F32)
            hm = (_silu(a) * u).astype(BF16)
            part = jnp.dot(hm, wd, preferred_element_type=F32).reshape(rb, cap, d)
            total = part if first else acc_ref[blk] + part
            if last:
                y_ref[blk, 0] = (total * gate_ref[blk, 0]).astype(BF16)
            else:
                acc_ref[blk] = total

    if n_f == 1:
        step(True, True)
    else:
        pl.when(f == 0)(lambda: step(True, False))
        if n_f > 2:
            pl.when(jnp.logical_and(f > 0, f < n_f - 1))(lambda: step(False, False))
        pl.when(f == n_f - 1)(lambda: step(False, True))


def _experts(xs, w_gate, w_up, w_down, gate, fc=768, rb=4):
    b, e, cap, d = xs.shape
    dff = w_gate.shape[2]
    per_e = lambda i, f: (0, i, 0, 0)
    return pl.pallas_call(
        functools.partial(_expert_kernel, rb=rb, n_f=dff // fc),
        grid=(e, dff // fc),
        in_specs=[pl.BlockSpec((b, 1, cap, d), per_e),
                  pl.BlockSpec((1, d, fc), lambda i, f: (i, 0, f)),
                  pl.BlockSpec((1, d, fc), lambda i, f: (i, 0, f)),
                  pl.BlockSpec((1, fc, d), lambda i, f: (i, f, 0)),
                  pl.BlockSpec((b, 1, cap, 1), per_e)],
        out_specs=pl.BlockSpec((b, 1, cap, d), per_e),
        out_shape=jax.ShapeDtypeStruct((b, e, cap, d), BF16),
        scratch_shapes=[pltpu.VMEM((b, cap, d), F32)],
        compiler_params=_params("parallel", "arbitrary"),
        name="moe_experts",
    )(xs, w_gate, w_up, w_down, gate)


def _combine_kernel(eb_ref, slot_t_ref, y_ref, xn_ref, mod_ref, nf_ref, o_ref, *, cap, jc, n_chunks):
    bi = pl.program_id(0)
    jo = pl.program_id(1)
    n_e = y_ref.shape[1]
    ci = lax.broadcasted_iota(jnp.int32, (TCH, EXPERT_GROUP * SLOT_W), 1)
    toks = [slice(jj * TCH, (jj + 1) * TCH) for jj in range(jc)]

    def finish(tok, moe):
        x = xn_ref[0, tok] + mod_ref[0][5:6] * moe
        o_ref[0, tok] = _rms(x, nf_ref[...])

    moes = []
    for jj, tok in enumerate(toks):
        j = jo * jc + jj
        st = slot_t_ref[0, tok, :]
        s_blocks, y_blocks = [], []
        for g in range(n_e // EXPERT_GROUP):
            col = None
            for q in reversed(range(EXPERT_GROUP)):
                e = EXPERT_GROUP * g + q
                _, _, start = _slot_window(eb_ref, bi, e, j, n_e, n_chunks, cap)
                rel = st[:, e:e + 1] - start
                tgt = jnp.where(jnp.logical_and(rel >= 0, rel < SLOT_W), rel + q * SLOT_W, -1)
                col = tgt if col is None else jnp.where(ci < (q + 1) * SLOT_W, tgt, col)
                y_blocks.insert(g * EXPERT_GROUP, y_ref[0, e, pl.ds(start, SLOT_W), :])
            s_blocks.append(jnp.where(col == ci, 1.0, 0.0).astype(BF16))
        scat = jnp.concatenate(s_blocks, axis=1)
        ywin = jnp.concatenate(y_blocks, axis=0)
        moes.append(jnp.dot(scat, ywin, preferred_element_type=F32))
    for tok, moe in zip(toks, moes):
        finish(tok, moe)

    for jj, tok in enumerate(toks):
        @pl.when(_window_overflow(eb_ref, bi, jo * jc + jj, n_e, n_chunks, cap))
        def _():
            st = slot_t_ref[0, tok, :]
            cf = lax.broadcasted_iota(jnp.int32, (TCH, cap), 1)
            dense = jnp.concatenate(
                [jnp.where(st[:, e:e + 1] == cf, 1.0, 0.0).astype(BF16) for e in range(n_e)], axis=1)
            finish(tok, jnp.dot(dense, y_ref[0].reshape(n_e * cap, y_ref.shape[3]),
                                preferred_element_type=F32))


def _combine(ebound, slot_t, y, x_new, mod, norm_final, cap, jc=4):
    b, n, d = x_new.shape
    e = slot_t.shape[2]
    n_chunks = n // TCH
    assert n_chunks % jc == 0
    tile = lambda i, j, *_: (i, j, 0)
    return pl.pallas_call(
        functools.partial(_combine_kernel, cap=cap, jc=jc, n_chunks=n_chunks),
        grid_spec=pltpu.PrefetchScalarGridSpec(
            num_scalar_prefetch=1,
            grid=(b, n_chunks // jc),
            in_specs=[pl.BlockSpec((1, jc * TCH, e), tile),
                      pl.BlockSpec((1, e, cap, d), lambda i, j, *_: (i, 0, 0, 0)),
                      pl.BlockSpec((1, jc * TCH, d), tile),
                      pl.BlockSpec((1, N_MOD, d), lambda i, j, *_: (i, 0, 0)),
                      pl.BlockSpec((1, d), lambda i, j, *_: (0, 0))],
            out_specs=pl.BlockSpec((1, jc * TCH, d), tile)),
        out_shape=jax.ShapeDtypeStruct((b, n, d), F32),
        compiler_params=_params("parallel", "arbitrary"),
        name="moe_combine_norm",
    )(ebound, slot_t, y, x_new, mod, norm_final)


def kernel(x, c, ctx, c_ctx, w_mod, b_mod, norm1, w_in, rpb, w_s, b_s, gmlp_norm, out_norm_a, out_norm_b,
           w_out, norm2, w_router, w_gate, w_up, w_down, norm_final):
    b, n, d = x.shape
    assert w_mod.shape[0] == 1, "single-layer stack only"
    assert n % (GRID_W * ROWS_PER_STEP) == 0 and n // GRID_W >= NA_KH
    assert n % TCH == 0 and N_EXPERTS % EXPERT_GROUP == 0
    cap = EC_CAPACITY_FACTOR * n // N_EXPERTS

    pad = (-(b + 1)) % 8
    cc = jnp.concatenate([c, c_ctx[None], jnp.zeros((pad, d), F32)], axis=0)
    m = _modulation(cc, w_mod[0], b_mod[0][None])
    mod = m.reshape(-1, N_MOD, d)

    w_in_b = w_in[0].astype(BF16)
    ws2 = w_s[0].astype(BF16).reshape(N_GROUPS_SG // 2, 2 * CHUNK, CHUNK)
    bs2 = jnp.broadcast_to(b_s[0].reshape(N_GROUPS_SG // 2, 2 * CHUNK, 1), (N_GROUPS_SG // 2, 2 * CHUNK, LANES))
    q, k, v, ob = _in_proj(x, mod, norm1, w_in_b, ws2, bs2, gmlp_norm, out_norm_b)

    kc, vc = _ctx_proj(ctx.reshape(b * CTX_LEN, d), mod, b, norm1, w_in_b)
    kc = kc.reshape(b, CTX_LEN, D_NA)
    vc = vc.reshape(b, CTX_LEN, D_NA)

    oa = _attention(q, k, v, kc, vc, _bias_blocks(rpb[0]), out_norm_a)

    x_new, h2, aff = _out_proj(oa, ob, x, mod, w_out[0].astype(BF16), norm2, w_router[0].T)

    slot, ebound = _topk_slots(aff.reshape(b * N_EXPERTS, n), cap)
    slot = slot.reshape(b, N_EXPERTS, n)
    ebound = ebound[:, :n // TCH + 1].reshape(-1)

    xs, gate = _gather_tokens(ebound, h2, slot, aff, cap)
    y = _experts(xs, w_gate[0], w_up[0], w_down[0], gate)
    return _combine(ebound, jnp.swapaxes(slot, 1, 2), y, x_new, mod, norm_final[None], cap)
```

```python
import functools

import numpy as np
import jax
import jax.numpy as jnp
from jax import lax
from jax.experimental import pallas as pl
from jax.experimental.pallas import tpu as pltpu

D_MODEL = 1024
GRID_W = 64
CTX_LEN = 256
N_HEADS_NA = 8
HEAD_DIM = 64
D_NA = N_HEADS_NA * HEAD_DIM
NA_KH = 8
NA_KW = 16
D_SG = D_MODEL - D_NA
N_GROUPS_SG = 8
SG_GROUP_DIM = D_SG // N_GROUPS_SG
CHUNK = 128
N_EXPERTS = 16
EC_CAPACITY_FACTOR = 2
D_EXPERT = 1536
N_MOD = 6
EPS = 1e-6
NEG_INF = -1e30
LOG2E = float(np.log2(np.e))

LANES = 128
VMEM_LIMIT = 56 * 1024 * 1024

F32 = jnp.float32
BF16 = jnp.bfloat16
HIGHEST = lax.Precision.HIGHEST

ROWS_PER_STEP = 16

MANTISSA_STEPS = 36

TCH = 256
SLOT_W = 64
SLOT_ALIGN = 16
EXPERT_GROUP = 4


def _params(*sem):
    return pltpu.CompilerParams(dimension_semantics=sem, vmem_limit_bytes=VMEM_LIMIT)


def _rms_mod(x, g, shift, scale):
    r = lax.rsqrt(jnp.mean(x * x, axis=-1, keepdims=True) + EPS)
    return (x * r) * g * (1.0 + scale) + shift


def _rms(x, g):
    return x * lax.rsqrt(jnp.mean(x * x, axis=-1, keepdims=True) + EPS) * g


def _gelu_tanh(x):
    return 0.5 * x * (1.0 + jnp.tanh(np.sqrt(2.0 / np.pi).astype(np.float32) * (x + 0.044715 * (x * x * x))))


def _silu(x):
    return x * jax.nn.sigmoid(x)


def _bf16_terms(x, n_terms):
    terms = []
    for _ in range(n_terms):
        t = x.astype(BF16)
        terms.append(t)
        x = x - t.astype(F32)
    return terms


def _mod_kernel(c_ref, w_ref, b_ref, o_ref):
    rows = c_ref.shape[0]
    s = jnp.concatenate(_bf16_terms(_silu(c_ref[...]), 3), axis=0)
    w_hi, w_lo = _bf16_terms(w_ref[...], 2)
    hi = jnp.dot(s, w_hi, preferred_element_type=F32)
    lo = jnp.dot(s[:2 * rows], w_lo, preferred_element_type=F32)
    small = (hi[2 * rows:] + lo[rows:]) + (hi[rows:2 * rows] + lo[:rows])
    o_ref[...] = (small + hi[:rows]) + b_ref[...]


def _modulation(cc, w_mod, b_mod):
    rows, d = cc.shape
    n = w_mod.shape[1]
    tn = 1024
    return pl.pallas_call(
        _mod_kernel,
        grid=(n // tn,),
        in_specs=[pl.BlockSpec((rows, d), lambda j: (0, 0)),
                  pl.BlockSpec((d, tn), lambda j: (0, j)),
                  pl.BlockSpec((1, tn), lambda j: (0, j))],
        out_specs=pl.BlockSpec((rows, tn), lambda j: (0, j)),
        out_shape=jax.ShapeDtypeStruct((rows, n), F32),
        compiler_params=_params("arbitrary"),
        name="modulation",
    )(cc, w_mod, b_mod)


def _in_kernel(x_ref, mod_ref, n1_ref, w_ref, ws_ref, bs_ref, gn_ref, onb_ref,
               q_ref, k_ref, v_ref, ob_ref, sp_ref, *, sub):
    mod = mod_ref[0]
    tiles = [slice(r, r + sub) for r in range(0, x_ref.shape[1], sub)]
    nch = sub // CHUNK
    lane = lax.broadcasted_iota(jnp.int32, (CHUNK, LANES), 1)
    first = lane < SG_GROUP_DIM

    def project(rs):
        hb = _rms_mod(x_ref[0, rs], n1_ref[...], mod[0:1], mod[1:2]).astype(BF16)
        qkv = jnp.dot(hb, w_ref[:, :3 * D_NA], preferred_element_type=F32)
        q_ref[0, rs] = (qkv[:, :D_NA] * (HEAD_DIM ** -0.5 * LOG2E)).astype(BF16)
        k_ref[0, rs] = qkv[:, D_NA:2 * D_NA].astype(BF16)
        v_ref[0, rs] = qkv[:, 2 * D_NA:].astype(BF16)
        return jnp.dot(hb, w_ref[:, 3 * D_NA:], preferred_element_type=F32)

    def gate_mlp(rs, uz):
        u = _gelu_tanh(uz[:, :D_SG])
        z = _gelu_tanh(uz[:, D_SG:])
        mu = jnp.mean(z, axis=-1, keepdims=True)
        zc = z - mu
        var = jnp.mean(zc * zc, axis=-1, keepdims=True)
        zb = (zc * lax.rsqrt(var + EPS) * gn_ref[...]).astype(BF16)
        for p in range(N_GROUPS_SG // 2):
            zp = jnp.concatenate(
                [zb[c * CHUNK:(c + 1) * CHUNK, p * LANES:(p + 1) * LANES] for c in range(nch)], axis=1)
            r = jnp.dot(ws_ref[p], zp, preferred_element_type=F32)
            bs = bs_ref[p]
            for c in range(nch):
                top = r[:CHUNK, c * LANES:(c + 1) * LANES] + bs[:CHUNK]
                bot = r[CHUNK:, c * LANES:(c + 1) * LANES] + bs[CHUNK:]
                sp_ref[rs.start + c * CHUNK:rs.start + (c + 1) * CHUNK, p * LANES:(p + 1) * LANES] = (
                    jnp.where(first, top, bot))
        ob = u * sp_ref[rs, :]
        ob_ref[0, rs] = _rms(ob, onb_ref[...]).astype(BF16)

    uz = {0: project(tiles[0])}
    for i, rs in enumerate(tiles):
        if i + 1 < len(tiles):
            uz[i + 1] = project(tiles[i + 1])
        gate_mlp(rs, uz.pop(i))


def _in_proj(x, mod, norm1, w_in_b, ws2, bs2, gmlp_norm, out_norm_b, tm=2048, sub=512):
    b, n, d = x.shape
    d_in = w_in_b.shape[1]
    full2 = lambda i, j: (0, 0)
    full3 = lambda i, j: (0, 0, 0)
    tile = lambda i, j: (i, j, 0)
    act = jax.ShapeDtypeStruct((b, n, D_NA), BF16)
    return pl.pallas_call(
        functools.partial(_in_kernel, sub=sub),
        grid=(b, n // tm),
        in_specs=[pl.BlockSpec((1, tm, d), tile),
                  pl.BlockSpec((1, N_MOD, d), lambda i, j: (i, 0, 0)),
                  pl.BlockSpec((1, d), full2),
                  pl.BlockSpec((d, d_in), full2),
                  pl.BlockSpec(ws2.shape, full3),
                  pl.BlockSpec(bs2.shape, full3),
                  pl.BlockSpec((1, D_SG), full2),
                  pl.BlockSpec((1, D_SG), full2)],
        out_specs=[pl.BlockSpec((1, tm, D_NA), tile)] * 4,
        out_shape=[act] * 4,
        scratch_shapes=[pltpu.VMEM((tm, D_SG), F32)],
        compiler_params=_params("parallel", "arbitrary"),
        name="in_proj_gmlp",
    )(x, mod, norm1, w_in_b, ws2, bs2, gmlp_norm, out_norm_b)


def _ctx_kernel(x_ref, mod_ref, n1_ref, wk_ref, wv_ref, k_ref, v_ref):
    mod = mod_ref[0]
    hb = _rms_mod(x_ref[...], n1_ref[...], mod[0:1], mod[1:2]).astype(BF16)
    k_ref[...] = jnp.dot(hb, wk_ref[...], preferred_element_type=F32).astype(BF16)
    v_ref[...] = jnp.dot(hb, wv_ref[...], preferred_element_type=F32).astype(BF16)


def _ctx_proj(ctx2, mod, mod_row, norm1, w_in_b, tm=512):
    rows, d = ctx2.shape
    act = jax.ShapeDtypeStruct((rows, D_NA), BF16)
    return pl.pallas_call(
        _ctx_kernel,
        grid=(rows // tm,),
        in_specs=[pl.BlockSpec((tm, d), lambda i: (i, 0)),
                  pl.BlockSpec((1, N_MOD, d), lambda i: (mod_row, 0, 0)),
                  pl.BlockSpec((1, d), lambda i: (0, 0)),
                  pl.BlockSpec((d, D_NA), lambda i: (0, 1)),
                  pl.BlockSpec((d, D_NA), lambda i: (0, 2))],
        out_specs=[pl.BlockSpec((tm, D_NA), lambda i: (i, 0))] * 2,
        out_shape=[act, act],
        compiler_params=_params("arbitrary"),
        name="ctx_kv_proj",
    )(ctx2, mod, norm1, w_in_b, w_in_b)


def _attn_out_kernel(q_ref, k_ref, v_ref, kc_ref, vc_ref, pb_ref, ona_ref,
                     ob_ref, x_ref, mod_ref, w_ref, n2_ref, wr_ref, xn_ref, h2_ref, aff_ref, *, rows, sub):
    kc = kc_ref[0]
    vc = vc_ref[0]
    nk = NA_KH * GRID_W
    lane = lax.broadcasted_iota(jnp.int32, (GRID_W, LANES), 1)
    first = lane < HEAD_DIM
    nt = (((1,), (1,)), ((), ()))
    units = []
    for r in range(ROWS_PER_STEP):
        row = pl.program_id(1) * ROWS_PER_STEP + r
        rs = jnp.clip(row - NA_KH // 2, 0, rows - NA_KH)
        start = pl.multiple_of(rs * GRID_W, GRID_W)
        off = rs - row + NA_KH - 1
        units += [(r, start, off, p) for p in range(N_HEADS_NA // 2)]

    def scores(r, start, off, p):
        sl = slice(p * LANES, (p + 1) * LANES)
        qp = q_ref[0, r * GRID_W:(r + 1) * GRID_W, sl]
        q2 = jnp.concatenate([jnp.where(first, qp, jnp.zeros_like(qp)),
                              jnp.where(first, jnp.zeros_like(qp), qp)], axis=0)
        bias = jnp.concatenate(
            [jnp.concatenate([pb_ref[2 * p + j, off + 2 * c] for c in range(NA_KH // 2)], axis=1)
             for j in range(2)], axis=0)
        s_nb = lax.dot_general(q2, k_ref[0, pl.ds(start, nk), sl], nt, preferred_element_type=F32) + bias
        s_cx = lax.dot_general(q2, kc[:, sl], nt, preferred_element_type=F32)
        return s_nb, s_cx

    def softmax(s_nb, s_cx):
        m = jnp.maximum(jnp.max(s_nb, axis=1, keepdims=True), jnp.max(s_cx, axis=1, keepdims=True))
        e_nb = jnp.exp2(s_nb - m)
        e_cx = jnp.exp2(s_cx - m)
        l = jnp.sum(e_nb, axis=1, keepdims=True) + jnp.sum(e_cx, axis=1, keepdims=True)
        return e_nb.astype(BF16), e_cx.astype(BF16), l

    def values(r, start, off, p, e_nb, e_cx, l):
        sl = slice(p * LANES, (p + 1) * LANES)
        o2 = (jnp.dot(e_nb, v_ref[0, pl.ds(start, nk), sl], preferred_element_type=F32)
              + jnp.dot(e_cx, vc[:, sl], preferred_element_type=F32)) / l
        return jnp.where(first, o2[:GRID_W], o2[GRID_W:])

    s, pr, o = {}, {}, {}
    n_u = len(units)
    for step in range(n_u + 2):
        if step < n_u:
            s[step] = scores(*units[step])
        if 0 <= step - 1 < n_u:
            pr[step - 1] = softmax(*s.pop(step - 1))
        if 0 <= step - 2 < n_u:
            o[step - 2] = values(*units[step - 2], *pr.pop(step - 2))
    n_p = N_HEADS_NA // 2
    oa_rows = []
    for r in range(ROWS_PER_STEP):
        out = jnp.concatenate([o[r * n_p + p] for p in range(n_p)], axis=1)
        oa_rows.append(_rms(out, ona_ref[...]).astype(BF16))

    mod = mod_ref[0]
    w_hi, w_lo = _bf16_terms(wr_ref[...], 2)
    w2 = jnp.concatenate([w_hi, w_lo], axis=0)
    rps = sub // GRID_W
    tiles = [slice(t0, t0 + sub) for t0 in range(0, ROWS_PER_STEP * GRID_W, sub)]
    mixes = [jnp.dot(jnp.concatenate(oa_rows[i * rps:(i + 1) * rps], axis=0), w_ref[:D_NA],
                     preferred_element_type=F32)
             + jnp.dot(ob_ref[0, rs], w_ref[D_NA:], preferred_element_type=F32) for i, rs in enumerate(tiles)]
    for rs, mix in zip(tiles, mixes):
        xn = x_ref[0, rs] + mod[2:3] * mix
        xn_ref[0, rs] = xn
        h2 = _rms_mod(xn, n2_ref[...], mod[3:4], mod[4:5])
        h_hi, h_lo = _bf16_terms(h2, 2)
        h2_ref[0, rs] = h_hi
        l_hi = lax.dot_general(w2, h_hi, nt, preferred_element_type=F32)
        l_lo = lax.dot_general(w_hi, h_lo, nt, preferred_element_type=F32)
        logits = l_hi[:N_EXPERTS] + l_hi[N_EXPERTS:] + l_lo
        e = jnp.exp(logits - jnp.max(logits, axis=0, keepdims=True))
        aff_ref[0, :, rs] = e / jnp.sum(e, axis=0, keepdims=True)


def _bias_blocks(rpb):
    n_ro, n_co = 2 * NA_KH - 1, 2 * NA_KW - 1
    qc = np.arange(GRID_W)[:, None]
    kc = np.arange(GRID_W)[None, :]
    cs = np.clip(qc - NA_KW // 2, 0, GRID_W - NA_KW)
    col_ok = (kc >= cs) & (kc < cs + NA_KW)
    spread = ((kc - qc + NA_KW - 1)[None] == np.arange(n_co)[:, None, None]) & col_ok[None]
    blocks = jnp.dot(rpb.reshape(-1, n_co), jnp.asarray(spread.reshape(n_co, -1), F32), precision=HIGHEST)
    blocks = jnp.where(jnp.asarray(col_ok.reshape(-1)), blocks * LOG2E, NEG_INF)
    blocks = blocks.reshape(N_HEADS_NA, n_ro, GRID_W, GRID_W)
    return jnp.concatenate([blocks[:, :-1], blocks[:, 1:]], axis=-1)


def _attention_out(q, k, v, kc, vc, pb, out_norm_a, ob, x, mod, w_out_b, norm2, w_router_t, sub=512):
    b, n, d = x.shape
    rows = n // GRID_W
    tq = ROWS_PER_STEP * GRID_W
    per_b = lambda i, t: (i, 0, 0)
    tile = lambda i, t: (i, t, 0)
    full = lambda i, t: (0, 0)
    return pl.pallas_call(
        functools.partial(_attn_out_kernel, rows=rows, sub=sub),
        grid=(b, n // tq),
        in_specs=[pl.BlockSpec((1, tq, D_NA), tile),
                  pl.BlockSpec((1, n, D_NA), per_b),
                  pl.BlockSpec((1, n, D_NA), per_b),
                  pl.BlockSpec((1, CTX_LEN, D_NA), per_b),
                  pl.BlockSpec((1, CTX_LEN, D_NA), per_b),
                  pl.BlockSpec(pb.shape, lambda i, t: (0, 0, 0, 0)),
                  pl.BlockSpec((1, D_NA), full),
                  pl.BlockSpec((1, tq, D_SG), tile),
                  pl.BlockSpec((1, tq, d), tile),
                  pl.BlockSpec((1, N_MOD, d), per_b),
                  pl.BlockSpec(w_out_b.shape, full),
                  pl.BlockSpec((1, d), full),
                  pl.BlockSpec((N_EXPERTS, d), full)],
        out_specs=[pl.BlockSpec((1, tq, d), tile),
                   pl.BlockSpec((1, tq, d), tile),
                   pl.BlockSpec((1, N_EXPERTS, tq), lambda i, t: (i, 0, t))],
        out_shape=[jax.ShapeDtypeStruct((b, n, d), F32),
                   jax.ShapeDtypeStruct((b, n, d), BF16),
                   jax.ShapeDtypeStruct((b, N_EXPERTS, n), F32)],
        compiler_params=_params("parallel", "arbitrary"),
        name="attention_out_router",
    )(q, k, v, kc, vc, pb, out_norm_a, ob, x, mod, w_out_b, norm2, w_router_t)


def _prefix_count(mask_f, tri):
    rows, n = mask_f.shape
    parts = []
    carry = jnp.zeros((rows, 1), F32)
    for j in range(n // LANES):
        blk = mask_f[:, j * LANES:(j + 1) * LANES]
        parts.append(jnp.dot(blk.astype(BF16), tri, preferred_element_type=F32) + carry)
        carry = carry + jnp.sum(blk, axis=1, keepdims=True)
    return jnp.concatenate(parts, axis=1)


def _topk_kernel(aff_ref, slot_ref, eb_ref, *, cap):
    a = aff_ref[...]
    rows = a.shape[0]

    def enough(t):
        return jnp.sum(jnp.where(a >= t, 1.0, 0.0), axis=1, keepdims=True) >= cap

    tiny = jnp.full((rows, 1), float(np.finfo(np.float32).tiny), F32)
    normal = enough(tiny)
    pw = tiny
    hi = jnp.full((rows, 1), 4.0, F32)
    for bit in range(6, -1, -1):
        cand = pw * (2.0 ** (1 << bit))
        ok = enough(cand)
        pw = jnp.where(ok, cand, pw)
        hi = jnp.where(ok, hi, cand)
    lo = jnp.where(normal, pw, 0.0)
    hi = jnp.where(normal, hi, tiny)
    step = lo
    for _ in range(MANTISSA_STEPS):
        step = step * 0.5
        cand = lo + step
        ok = enough(cand)
        lo = jnp.where(ok, cand, lo)
        hi = jnp.where(ok, hi, cand)
    above = a >= hi
    tie = jnp.logical_and(a >= lo, jnp.logical_not(above))
    n_above = jnp.sum(jnp.where(above, 1.0, 0.0), axis=1, keepdims=True)
    ri = lax.broadcasted_iota(jnp.int32, (LANES, LANES), 0)
    ci = lax.broadcasted_iota(jnp.int32, (LANES, LANES), 1)
    tri = jnp.where(ri <= ci, 1.0, 0.0).astype(BF16)
    tie_rank = _prefix_count(jnp.where(tie, 1.0, 0.0), tri)
    sel = jnp.logical_or(above, jnp.logical_and(tie, tie_rank <= cap - n_above))
    sel_f = jnp.where(sel, 1.0, 0.0)
    pos = _prefix_count(sel_f, tri) - 1.0
    slot_ref[...] = jnp.where(sel, pos, -1.0).astype(jnp.int32)
    lane = lax.broadcasted_iota(jnp.int32, (rows, LANES), 1)
    cnt = jnp.zeros((rows, LANES), F32)
    for j in range(a.shape[1] // TCH):
        cnt = jnp.where(lane == j, jnp.sum(sel_f[:, j * TCH:(j + 1) * TCH], axis=1, keepdims=True), cnt)
    before = jnp.where(ri < ci, 1.0, 0.0).astype(BF16)
    eb_ref[...] = jnp.dot(cnt.astype(BF16), before, preferred_element_type=F32).astype(jnp.int32)


def _topk_slots(aff2, cap):
    rows, n = aff2.shape
    tr = rows
    return pl.pallas_call(
        functools.partial(_topk_kernel, cap=cap),
        grid=(rows // tr,),
        in_specs=[pl.BlockSpec((tr, n), lambda i: (i, 0))],
        out_specs=[pl.BlockSpec((tr, n), lambda i: (i, 0)), pl.BlockSpec((tr, LANES), lambda i: (i, 0))],
        out_shape=[jax.ShapeDtypeStruct((rows, n), jnp.int32), jax.ShapeDtypeStruct((rows, LANES), jnp.int32)],
        compiler_params=_params("parallel"),
        name="expert_topk",
    )(aff2)


def _slot_window(eb_ref, bi, e, j, n_e, n_chunks, cap):
    base = (bi * n_e + e) * (n_chunks + 1) + j
    s0 = eb_ref[base]
    s1 = eb_ref[base + 1]
    start = jnp.minimum(s0 & -SLOT_ALIGN, cap - SLOT_W)
    return s0, s1, pl.multiple_of(start, SLOT_ALIGN)


def _window_overflow(eb_ref, bi, j, n_e, n_chunks, cap):
    over = None
    for e in range(n_e):
        _, s1, start = _slot_window(eb_ref, bi, e, j, n_e, n_chunks, cap)
        o = s1 > start + SLOT_W
        over = o if over is None else jnp.logical_or(over, o)
    return over


def _gather_kernel(eb_ref, h2_ref, slot_ref, aff_ref, xs_ref, gate_ref, *, cap, jc, n_chunks):
    bi = pl.program_id(0)
    jo = pl.program_id(1)
    n_e = slot_ref.shape[1]
    wi = lax.broadcasted_iota(jnp.int32, (SLOT_W, TCH), 0)
    wcol = lax.broadcasted_iota(jnp.int32, (SLOT_W, 1), 0)

    @pl.when(jo == 0)
    def _():
        xs_ref[...] = jnp.zeros_like(xs_ref)
        gate_ref[...] = jnp.zeros_like(gate_ref)

    for jj in range(jc):
        j = jo * jc + jj
        tok = slice(jj * TCH, (jj + 1) * TCH)
        h2c = h2_ref[0, tok, :]
        for g in range(n_e // EXPERT_GROUP):
            blocks, meta = [], []
            for q in range(EXPERT_GROUP):
                e = EXPERT_GROUP * g + q
                s0, s1, start = _slot_window(eb_ref, bi, e, j, n_e, n_chunks, cap)
                hit = slot_ref[0, e:e + 1, tok] - start == wi
                blocks.append(jnp.where(hit, 1.0, 0.0).astype(BF16))
                gate = jnp.sum(jnp.where(hit, aff_ref[0, e:e + 1, tok], 0.0), axis=1, keepdims=True)
                meta.append((e, s0, s1, start, gate))
            onehot = jnp.concatenate(blocks, axis=0)
            rows = jnp.dot(onehot, h2c, preferred_element_type=F32).astype(BF16)
            for q, (e, s0, s1, start, gate) in enumerate(meta):
                own = jnp.logical_and(wcol + start >= s0, wcol + start < s1)
                win = pl.ds(start, SLOT_W)
                xs_ref[0, e, win, :] = jnp.where(own, rows[q * SLOT_W:(q + 1) * SLOT_W], xs_ref[0, e, win, :])
                gate_ref[0, e, win, :] = jnp.where(own, gate, gate_ref[0, e, win, :])

    for jj in range(jc):
        j = jo * jc + jj
        tok = slice(jj * TCH, (jj + 1) * TCH)

        @pl.when(_window_overflow(eb_ref, bi, j, n_e, n_chunks, cap))
        def _():
            h2c = h2_ref[0, tok, :]
            ci = lax.broadcasted_iota(jnp.int32, (cap, TCH), 0)
            ccol = lax.broadcasted_iota(jnp.int32, (cap, 1), 0)
            for e in range(n_e):
                s0, s1, _ = _slot_window(eb_ref, bi, e, j, n_e, n_chunks, cap)
                hit = slot_ref[0, e:e + 1, tok] == ci
                rows = jnp.dot(jnp.where(hit, 1.0, 0.0).astype(BF16), h2c, preferred_element_type=F32).astype(BF16)
                gate = jnp.sum(jnp.where(hit, aff_ref[0, e:e + 1, tok], 0.0), axis=1, keepdims=True)
                own = jnp.logical_and(ccol >= s0, ccol < s1)
                xs_ref[0, e] = jnp.where(own, rows, xs_ref[0, e])
                gate_ref[0, e] = jnp.where(own, gate, gate_ref[0, e])


def _gather_tokens(ebound, h2, slot, aff, cap, jc=4):
    b, n, d = h2.shape
    e = slot.shape[1]
    n_chunks = n // TCH
    assert n_chunks % jc == 0
    tok = lambda i, j, *_: (i, j, 0)
    per_b = lambda i, j, *_: (i, 0, 0, 0)
    return pl.pallas_call(
        functools.partial(_gather_kernel, cap=cap, jc=jc, n_chunks=n_chunks),
        grid_spec=pltpu.PrefetchScalarGridSpec(
            num_scalar_prefetch=1,
            grid=(b, n_chunks // jc),
            in_specs=[pl.BlockSpec((1, jc * TCH, d), tok),
                      pl.BlockSpec((1, e, jc * TCH), lambda i, j, *_: (i, 0, j)),
                      pl.BlockSpec((1, e, jc * TCH), lambda i, j, *_: (i, 0, j))],
            out_specs=[pl.BlockSpec((1, e, cap, d), per_b),
                       pl.BlockSpec((1, e, cap, 1), per_b)]),
        out_shape=[jax.ShapeDtypeStruct((b, e, cap, d), BF16),
                   jax.ShapeDtypeStruct((b, e, cap, 1), F32)],
        compiler_params=_params("parallel", "arbitrary"),
        name="moe_gather",
    )(ebound, h2, slot, aff)


def _expert_kernel(xs_ref, wg_ref, wu_ref, wd_ref, gate_ref, y_ref, acc_ref, *, rb, n_f):
    f = pl.program_id(1)
    nb, _, cap, d = xs_ref.shape

    def step(first, last):
        wg = wg_ref[0].astype(BF16)
        wu = wu_ref[0].astype(BF16)
        wd = wd_ref[0].astype(BF16)
        for i in range(nb // rb):
            blk = slice(i * rb, (i + 1) * rb)
            xs = xs_ref[blk, 0].reshape(rb * cap, d)
            a = jnp.dot(xs, wg, preferred_element_type=F32)
            u = jnp.dot(xs, wu, preferred_element_type=F32)
            hm = (_silu(a) * u).astype(BF16)
            part = jnp.dot(hm, wd, preferred_element_type=F32).reshape(rb, cap, d)
            total = part if first else acc_ref[blk] + part
            if last:
                y_ref[blk, 0] = (total * gate_ref[blk, 0]).astype(BF16)
            else:
                acc_ref[blk] = total

    if n_f == 1:
        step(True, True)
    else:
        pl.when(f == 0)(lambda: step(True, False))
        if n_f > 2:
            pl.when(jnp.logical_and(f > 0, f < n_f - 1))(lambda: step(False, False))
        pl.when(f == n_f - 1)(lambda: step(False, True))


def _experts(xs, w_gate, w_up, w_down, gate, fc=768, rb=4):
    b, e, cap, d = xs.shape
    dff = w_gate.shape[2]
    per_e = lambda i, f: (0, i, 0, 0)
    return pl.pallas_call(
        functools.partial(_expert_kernel, rb=rb, n_f=dff // fc),
        grid=(e, dff // fc),
        in_specs=[pl.BlockSpec((b, 1, cap, d), per_e),
                  pl.BlockSpec((1, d, fc), lambda i, f: (i, 0, f)),
                  pl.BlockSpec((1, d, fc), lambda i, f: (i, 0, f)),
                  pl.BlockSpec((1, fc, d), lambda i, f: (i, f, 0)),
                  pl.BlockSpec((b, 1, cap, 1), per_e)],
        out_specs=pl.BlockSpec((b, 1, cap, d), per_e),
        out_shape=jax.ShapeDtypeStruct((b, e, cap, d), BF16),
        scratch_shapes=[pltpu.VMEM((b, cap, d), F32)],
        compiler_params=_params("parallel", "arbitrary"),
        name="moe_experts",
    )(xs, w_gate, w_up, w_down, gate)


def _combine_kernel(eb_ref, slot_t_ref, y_ref, xn_ref, mod_ref, nf_ref, o_ref, *, cap, jc, n_chunks):
    bi = pl.program_id(0)
    jo = pl.program_id(1)
    n_e = y_ref.shape[1]
    ci = lax.broadcasted_iota(jnp.int32, (TCH, EXPERT_GROUP * SLOT_W), 1)
    toks = [slice(jj * TCH, (jj + 1) * TCH) for jj in range(jc)]
    lane_e = lax.broadcasted_iota(jnp.int32, (1, n_e), 1)
    group_off = (lane_e & (EXPERT_GROUP - 1)) * SLOT_W

    def finish(tok, moe):
        x = xn_ref[0, tok] + mod_ref[0][5:6] * moe
        o_ref[0, tok] = _rms(x, nf_ref[...])

    moes = []
    for jj, tok in enumerate(toks):
        j = jo * jc + jj
        st = slot_t_ref[0, tok, :]
        starts = [_slot_window(eb_ref, bi, e, j, n_e, n_chunks, cap)[2] for e in range(n_e)]
        start_vec = jnp.zeros((1, n_e), jnp.int32)
        for e in range(n_e):
            start_vec = jnp.where(lane_e == e, starts[e], start_vec)
        rel = st - start_vec
        tgt = jnp.where(jnp.logical_and(rel >= 0, rel < SLOT_W), rel + group_off, -1)
        s_blocks, y_blocks = [], []
        for g in range(n_e // EXPERT_GROUP):
            col = None
            for q in reversed(range(EXPERT_GROUP)):
                e = EXPERT_GROUP * g + q
                col = tgt[:, e:e + 1] if col is None else jnp.where(ci < (q + 1) * SLOT_W, tgt[:, e:e + 1], col)
                y_blocks.insert(g * EXPERT_GROUP, y_ref[0, e, pl.ds(starts[e], SLOT_W), :])
            s_blocks.append(jnp.where(col == ci, 1.0, 0.0).astype(BF16))
        scat = jnp.concatenate(s_blocks, axis=1)
        ywin = jnp.concatenate(y_blocks, axis=0)
        moes.append(jnp.dot(scat, ywin, preferred_element_type=F32))
    for tok, moe in zip(toks, moes):
        finish(tok, moe)

    for jj, tok in enumerate(toks):
        @pl.when(_window_overflow(eb_ref, bi, jo * jc + jj, n_e, n_chunks, cap))
        def _():
            st = slot_t_ref[0, tok, :]
            cf = lax.broadcasted_iota(jnp.int32, (TCH, cap), 1)
            dense = jnp.concatenate(
                [jnp.where(st[:, e:e + 1] == cf, 1.0, 0.0).astype(BF16) for e in range(n_e)], axis=1)
            finish(tok, jnp.dot(dense, y_ref[0].reshape(n_e * cap, y_ref.shape[3]),
                                preferred_element_type=F32))


def _combine(ebound, slot_t, y, x_new, mod, norm_final, cap, jc=4):
    b, n, d = x_new.shape
    e = slot_t.shape[2]
    n_chunks = n // TCH
    assert n_chunks % jc == 0
    tile = lambda i, j, *_: (i, j, 0)
    return pl.pallas_call(
        functools.partial(_combine_kernel, cap=cap, jc=jc, n_chunks=n_chunks),
        grid_spec=pltpu.PrefetchScalarGridSpec(
            num_scalar_prefetch=1,
            grid=(b, n_chunks // jc),
            in_specs=[pl.BlockSpec((1, jc * TCH, e), tile),
                      pl.BlockSpec((1, e, cap, d), lambda i, j, *_: (i, 0, 0, 0)),
                      pl.BlockSpec((1, jc * TCH, d), tile),
                      pl.BlockSpec((1, N_MOD, d), lambda i, j, *_: (i, 0, 0)),
                      pl.BlockSpec((1, d), lambda i, j, *_: (0, 0))],
            out_specs=pl.BlockSpec((1, jc * TCH, d), tile)),
        out_shape=jax.ShapeDtypeStruct((b, n, d), F32),
        compiler_params=_params("parallel", "arbitrary"),
        name="moe_combine_norm",
    )(ebound, slot_t, y, x_new, mod, norm_final)


def kernel(x, c, ctx, c_ctx, w_mod, b_mod, norm1, w_in, rpb, w_s, b_s, gmlp_norm, out_norm_a, out_norm_b,
           w_out, norm2, w_router, w_gate, w_up, w_down, norm_final):
    b, n, d = x.shape
    assert w_mod.shape[0] == 1, "single-layer stack only"
    assert n % (GRID_W * ROWS_PER_STEP) == 0 and n // GRID_W >= NA_KH
    assert n % TCH == 0 and N_EXPERTS % EXPERT_GROUP == 0 and EXPERT_GROUP & (EXPERT_GROUP - 1) == 0
    cap = EC_CAPACITY_FACTOR * n // N_EXPERTS

    pad = (-(b + 1)) % 8
    cc = jnp.concatenate([c, c_ctx[None], jnp.zeros((pad, d), F32)], axis=0)
    m = _modulation(cc, w_mod[0], b_mod[0][None])
    mod = m.reshape(-1, N_MOD, d)

    w_in_b = w_in[0].astype(BF16)
    ws2 = w_s[0].astype(BF16).reshape(N_GROUPS_SG // 2, 2 * CHUNK, CHUNK)
    bs2 = jnp.broadcast_to(b_s[0].reshape(N_GROUPS_SG // 2, 2 * CHUNK, 1), (N_GROUPS_SG // 2, 2 * CHUNK, LANES))
    q, k, v, ob = _in_proj(x, mod, norm1, w_in_b, ws2, bs2, gmlp_norm, out_norm_b)

    kc, vc = _ctx_proj(ctx.reshape(b * CTX_LEN, d), mod, b, norm1, w_in_b)
    kc = kc.reshape(b, CTX_LEN, D_NA)
    vc = vc.reshape(b, CTX_LEN, D_NA)

    x_new, h2, aff = _attention_out(q, k, v, kc, vc, _bias_blocks(rpb[0]), out_norm_a,
                                    ob, x, mod, w_out[0].astype(BF16), norm2, w_router[0].T)

    slot, ebound = _topk_slots(aff.reshape(b * N_EXPERTS, n), cap)
    slot = slot.reshape(b, N_EXPERTS, n)
    ebound = ebound[:, :n // TCH + 1].reshape(-1)

    xs, gate = _gather_tokens(ebound, h2, slot, aff, cap)
    y = _experts(xs, w_gate[0], w_up[0], w_down[0], gate)
    return _combine(ebound, jnp.swapaxes(slot, 1, 2), y, x_new, mod, norm_final[None], cap)
```

```python
import functools

import numpy as np
import jax
import jax.numpy as jnp
from jax import lax
from jax.experimental import pallas as pl
from jax.experimental.pallas import tpu as pltpu

D_MODEL = 1024
GRID_W = 64
CTX_LEN = 256
N_HEADS_NA = 8
HEAD_DIM = 64
D_NA = N_HEADS_NA * HEAD_DIM
NA_KH = 8
NA_KW = 16
D_SG = D_MODEL - D_NA
N_GROUPS_SG = 8
SG_GROUP_DIM = D_SG // N_GROUPS_SG
CHUNK = 128
N_EXPERTS = 16
EC_CAPACITY_FACTOR = 2
D_EXPERT = 1536
N_MOD = 6
EPS = 1e-6
NEG_INF = -1e30
LOG2E = float(np.log2(np.e))

LANES = 128
VMEM_LIMIT = 56 * 1024 * 1024

F32 = jnp.float32
BF16 = jnp.bfloat16
HIGHEST = lax.Precision.HIGHEST

ROWS_PER_STEP = 16

MANTISSA_STEPS = 36

TCH = 256
SLOT_W = 64
SLOT_ALIGN = 16
EXPERT_GROUP = 4


def _params(*sem):
    return pltpu.CompilerParams(dimension_semantics=sem, vmem_limit_bytes=VMEM_LIMIT)


def _rms_mod(x, g, shift, scale):
    r = lax.rsqrt(jnp.mean(x * x, axis=-1, keepdims=True) + EPS)
    return (x * r) * g * (1.0 + scale) + shift


def _rms(x, g):
    return x * lax.rsqrt(jnp.mean(x * x, axis=-1, keepdims=True) + EPS) * g


def _gelu_tanh(x):
    return 0.5 * x * (1.0 + jnp.tanh(np.sqrt(2.0 / np.pi).astype(np.float32) * (x + 0.044715 * (x * x * x))))


def _silu(x):
    return x * jax.nn.sigmoid(x)


def _bf16_terms(x, n_terms):
    terms = []
    for _ in range(n_terms):
        t = x.astype(BF16)
        terms.append(t)
        x = x - t.astype(F32)
    return terms


def _mod_kernel(c_ref, w_ref, b_ref, o_ref):
    rows = c_ref.shape[0]
    s = jnp.concatenate(_bf16_terms(_silu(c_ref[...]), 3), axis=0)
    w_hi, w_lo = _bf16_terms(w_ref[...], 2)
    hi = jnp.dot(s, w_hi, preferred_element_type=F32)
    lo = jnp.dot(s[:2 * rows], w_lo, preferred_element_type=F32)
    small = (hi[2 * rows:] + lo[rows:]) + (hi[rows:2 * rows] + lo[:rows])
    o_ref[...] = (small + hi[:rows]) + b_ref[...]


def _modulation(cc, w_mod, b_mod):
    rows, d = cc.shape
    n = w_mod.shape[1]
    tn = 1024
    return pl.pallas_call(
        _mod_kernel,
        grid=(n // tn,),
        in_specs=[pl.BlockSpec((rows, d), lambda j: (0, 0)),
                  pl.BlockSpec((d, tn), lambda j: (0, j)),
                  pl.BlockSpec((1, tn), lambda j: (0, j))],
        out_specs=pl.BlockSpec((rows, tn), lambda j: (0, j)),
        out_shape=jax.ShapeDtypeStruct((rows, n), F32),
        compiler_params=_params("arbitrary"),
        name="modulation",
    )(cc, w_mod, b_mod)


def _in_kernel(x_ref, mod_ref, n1_ref, w_ref, ws_ref, bs_ref, gn_ref, onb_ref,
               q_ref, k_ref, v_ref, ob_ref, sp_ref, *, sub):
    mod = mod_ref[0]
    tiles = [slice(r, r + sub) for r in range(0, x_ref.shape[1], sub)]
    nch = sub // CHUNK
    lane = lax.broadcasted_iota(jnp.int32, (CHUNK, LANES), 1)
    first = lane < SG_GROUP_DIM

    def project(rs):
        hb = _rms_mod(x_ref[0, rs], n1_ref[...], mod[0:1], mod[1:2]).astype(BF16)
        qkv = jnp.dot(hb, w_ref[:, :3 * D_NA], preferred_element_type=F32)
        q_ref[0, rs] = (qkv[:, :D_NA] * (HEAD_DIM ** -0.5 * LOG2E)).astype(BF16)
        k_ref[0, rs] = qkv[:, D_NA:2 * D_NA].astype(BF16)
        v_ref[0, rs] = qkv[:, 2 * D_NA:].astype(BF16)
        return jnp.dot(hb, w_ref[:, 3 * D_NA:], preferred_element_type=F32)

    def gate_mlp(rs, uz):
        u = _gelu_tanh(uz[:, :D_SG])
        z = _gelu_tanh(uz[:, D_SG:])
        mu = jnp.mean(z, axis=-1, keepdims=True)
        zc = z - mu
        var = jnp.mean(zc * zc, axis=-1, keepdims=True)
        zb = (zc * lax.rsqrt(var + EPS) * gn_ref[...]).astype(BF16)
        for p in range(N_GROUPS_SG // 2):
            zp = jnp.concatenate(
                [zb[c * CHUNK:(c + 1) * CHUNK, p * LANES:(p + 1) * LANES] for c in range(nch)], axis=1)
            r = jnp.dot(ws_ref[p], zp, preferred_element_type=F32)
            bs = bs_ref[p]
            for c in range(nch):
                top = r[:CHUNK, c * LANES:(c + 1) * LANES] + bs[:CHUNK]
                bot = r[CHUNK:, c * LANES:(c + 1) * LANES] + bs[CHUNK:]
                sp_ref[rs.start + c * CHUNK:rs.start + (c + 1) * CHUNK, p * LANES:(p + 1) * LANES] = (
                    jnp.where(first, top, bot))
        ob = u * sp_ref[rs, :]
        ob_ref[0, rs] = _rms(ob, onb_ref[...]).astype(BF16)

    uz = {0: project(tiles[0])}
    for i, rs in enumerate(tiles):
        if i + 1 < len(tiles):
            uz[i + 1] = project(tiles[i + 1])
        gate_mlp(rs, uz.pop(i))


def _in_proj(x, mod, norm1, w_in_b, ws2, bs2, gmlp_norm, out_norm_b, tm=2048, sub=512):
    b, n, d = x.shape
    d_in = w_in_b.shape[1]
    full2 = lambda i, j: (0, 0)
    full3 = lambda i, j: (0, 0, 0)
    tile = lambda i, j: (i, j, 0)
    act = jax.ShapeDtypeStruct((b, n, D_NA), BF16)
    return pl.pallas_call(
        functools.partial(_in_kernel, sub=sub),
        grid=(b, n // tm),
        in_specs=[pl.BlockSpec((1, tm, d), tile),
                  pl.BlockSpec((1, N_MOD, d), lambda i, j: (i, 0, 0)),
                  pl.BlockSpec((1, d), full2),
                  pl.BlockSpec((d, d_in), full2),
                  pl.BlockSpec(ws2.shape, full3),
                  pl.BlockSpec(bs2.shape, full3),
                  pl.BlockSpec((1, D_SG), full2),
                  pl.BlockSpec((1, D_SG), full2)],
        out_specs=[pl.BlockSpec((1, tm, D_NA), tile)] * 4,
        out_shape=[act] * 4,
        scratch_shapes=[pltpu.VMEM((tm, D_SG), F32)],
        compiler_params=_params("parallel", "arbitrary"),
        name="in_proj_gmlp",
    )(x, mod, norm1, w_in_b, ws2, bs2, gmlp_norm, out_norm_b)


def _ctx_kernel(x_ref, mod_ref, n1_ref, wk_ref, wv_ref, k_ref, v_ref):
    mod = mod_ref[0]
    hb = _rms_mod(x_ref[...], n1_ref[...], mod[0:1], mod[1:2]).astype(BF16)
    k_ref[...] = jnp.dot(hb, wk_ref[...], preferred_element_type=F32).astype(BF16)
    v_ref[...] = jnp.dot(hb, wv_ref[...], preferred_element_type=F32).astype(BF16)


def _ctx_proj(ctx2, mod, mod_row, norm1, w_in_b, tm=512):
    rows, d = ctx2.shape
    act = jax.ShapeDtypeStruct((rows, D_NA), BF16)
    return pl.pallas_call(
        _ctx_kernel,
        grid=(rows // tm,),
        in_specs=[pl.BlockSpec((tm, d), lambda i: (i, 0)),
                  pl.BlockSpec((1, N_MOD, d), lambda i: (mod_row, 0, 0)),
                  pl.BlockSpec((1, d), lambda i: (0, 0)),
                  pl.BlockSpec((d, D_NA), lambda i: (0, 1)),
                  pl.BlockSpec((d, D_NA), lambda i: (0, 2))],
        out_specs=[pl.BlockSpec((tm, D_NA), lambda i: (i, 0))] * 2,
        out_shape=[act, act],
        compiler_params=_params("arbitrary"),
        name="ctx_kv_proj",
    )(ctx2, mod, norm1, w_in_b, w_in_b)


def _attn_out_kernel(q_ref, k_ref, v_ref, kc_ref, vc_ref, pb_ref, ona_ref,
                     ob_ref, x_ref, mod_ref, w_ref, n2_ref, wr_ref, xn_ref, h2_ref, aff_ref, *, rows, sub):
    kc = kc_ref[0]
    vc = vc_ref[0]
    nk = NA_KH * GRID_W
    lane = lax.broadcasted_iota(jnp.int32, (GRID_W, LANES), 1)
    first = lane < HEAD_DIM
    nt = (((1,), (1,)), ((), ()))
    units = []
    for r in range(ROWS_PER_STEP):
        row = pl.program_id(1) * ROWS_PER_STEP + r
        rs = jnp.clip(row - NA_KH // 2, 0, rows - NA_KH)
        start = pl.multiple_of(rs * GRID_W, GRID_W)
        off = rs - row + NA_KH - 1
        units += [(r, start, off, p) for p in range(N_HEADS_NA // 2)]

    def scores(r, start, off, p):
        sl = slice(p * LANES, (p + 1) * LANES)
        qp = q_ref[0, r * GRID_W:(r + 1) * GRID_W, sl]
        q2 = jnp.concatenate([jnp.where(first, qp, jnp.zeros_like(qp)),
                              jnp.where(first, jnp.zeros_like(qp), qp)], axis=0)
        bias = jnp.concatenate(
            [jnp.concatenate([pb_ref[2 * p + j, off + 2 * c] for c in range(NA_KH // 2)], axis=1)
             for j in range(2)], axis=0)
        s_nb = lax.dot_general(q2, k_ref[0, pl.ds(start, nk), sl], nt, preferred_element_type=F32) + bias
        s_cx = lax.dot_general(q2, kc[:, sl], nt, preferred_element_type=F32)
        return s_nb, s_cx

    def softmax(s_nb, s_cx):
        m = jnp.maximum(jnp.max(s_nb, axis=1, keepdims=True), jnp.max(s_cx, axis=1, keepdims=True))
        e_nb = jnp.exp2(s_nb - m)
        e_cx = jnp.exp2(s_cx - m)
        l = jnp.sum(e_nb, axis=1, keepdims=True) + jnp.sum(e_cx, axis=1, keepdims=True)
        return e_nb.astype(BF16), e_cx.astype(BF16), l

    def values(r, start, off, p, e_nb, e_cx, l):
        sl = slice(p * LANES, (p + 1) * LANES)
        o2 = (jnp.dot(e_nb, v_ref[0, pl.ds(start, nk), sl], preferred_element_type=F32)
              + jnp.dot(e_cx, vc[:, sl], preferred_element_type=F32)) / l
        return jnp.where(first, o2[:GRID_W], o2[GRID_W:])

    s, pr, o = {}, {}, {}
    n_u = len(units)
    for step in range(n_u + 2):
        if step < n_u:
            s[step] = scores(*units[step])
        if 0 <= step - 1 < n_u:
            pr[step - 1] = softmax(*s.pop(step - 1))
        if 0 <= step - 2 < n_u:
            o[step - 2] = values(*units[step - 2], *pr.pop(step - 2))
    n_p = N_HEADS_NA // 2
    oa_rows = []
    for r in range(ROWS_PER_STEP):
        out = jnp.concatenate([o[r * n_p + p] for p in range(n_p)], axis=1)
        oa_rows.append(_rms(out, ona_ref[...]).astype(BF16))

    mod = mod_ref[0]
    w_hi, w_lo = _bf16_terms(wr_ref[...], 2)
    w2 = jnp.concatenate([w_hi, w_lo], axis=0)
    rps = sub // GRID_W
    tiles = [slice(t0, t0 + sub) for t0 in range(0, ROWS_PER_STEP * GRID_W, sub)]
    mixes = [jnp.dot(jnp.concatenate(oa_rows[i * rps:(i + 1) * rps], axis=0), w_ref[:D_NA],
                     preferred_element_type=F32)
             + jnp.dot(ob_ref[0, rs], w_ref[D_NA:], preferred_element_type=F32) for i, rs in enumerate(tiles)]
    for rs, mix in zip(tiles, mixes):
        xn = x_ref[0, rs] + mod[2:3] * mix
        xn_ref[0, rs] = xn
        h2 = _rms_mod(xn, n2_ref[...], mod[3:4], mod[4:5])
        h_hi, h_lo = _bf16_terms(h2, 2)
        h2_ref[0, rs] = h_hi
        l_hi = lax.dot_general(w2, h_hi, nt, preferred_element_type=F32)
        l_lo = lax.dot_general(w_hi, h_lo, nt, preferred_element_type=F32)
        logits = l_hi[:N_EXPERTS] + l_hi[N_EXPERTS:] + l_lo
        e = jnp.exp(logits - jnp.max(logits, axis=0, keepdims=True))
        aff_ref[0, :, rs] = e / jnp.sum(e, axis=0, keepdims=True)


def _bias_blocks(rpb):
    n_ro, n_co = 2 * NA_KH - 1, 2 * NA_KW - 1
    qc = np.arange(GRID_W)[:, None]
    kc = np.arange(GRID_W)[None, :]
    cs = np.clip(qc - NA_KW // 2, 0, GRID_W - NA_KW)
    col_ok = (kc >= cs) & (kc < cs + NA_KW)
    spread = ((kc - qc + NA_KW - 1)[None] == np.arange(n_co)[:, None, None]) & col_ok[None]
    blocks = jnp.dot(rpb.reshape(-1, n_co), jnp.asarray(spread.reshape(n_co, -1), F32), precision=HIGHEST)
    blocks = jnp.where(jnp.asarray(col_ok.reshape(-1)), blocks * LOG2E, NEG_INF)
    blocks = blocks.reshape(N_HEADS_NA, n_ro, GRID_W, GRID_W)
    return jnp.concatenate([blocks[:, :-1], blocks[:, 1:]], axis=-1)


def _attention_out(q, k, v, kc, vc, pb, out_norm_a, ob, x, mod, w_out_b, norm2, w_router_t, sub=512):
    b, n, d = x.shape
    rows = n // GRID_W
    tq = ROWS_PER_STEP * GRID_W
    per_b = lambda i, t: (i, 0, 0)
    tile = lambda i, t: (i, t, 0)
    full = lambda i, t: (0, 0)
    return pl.pallas_call(
        functools.partial(_attn_out_kernel, rows=rows, sub=sub),
        grid=(b, n // tq),
        in_specs=[pl.BlockSpec((1, tq, D_NA), tile),
                  pl.BlockSpec((1, n, D_NA), per_b),
                  pl.BlockSpec((1, n, D_NA), per_b),
                  pl.BlockSpec((1, CTX_LEN, D_NA), per_b),
                  pl.BlockSpec((1, CTX_LEN, D_NA), per_b),
                  pl.BlockSpec(pb.shape, lambda i, t: (0, 0, 0, 0)),
                  pl.BlockSpec((1, D_NA), full),
                  pl.BlockSpec((1, tq, D_SG), tile),
                  pl.BlockSpec((1, tq, d), tile),
                  pl.BlockSpec((1, N_MOD, d), per_b),
                  pl.BlockSpec(w_out_b.shape, full),
                  pl.BlockSpec((1, d), full),
                  pl.BlockSpec((N_EXPERTS, d), full)],
        out_specs=[pl.BlockSpec((1, tq, d), tile),
                   pl.BlockSpec((1, tq, d), tile),
                   pl.BlockSpec((1, N_EXPERTS, tq), lambda i, t: (i, 0, t))],
        out_shape=[jax.ShapeDtypeStruct((b, n, d), F32),
                   jax.ShapeDtypeStruct((b, n, d), BF16),
                   jax.ShapeDtypeStruct((b, N_EXPERTS, n), F32)],
        compiler_params=_params("parallel", "arbitrary"),
        name="attention_out_router",
    )(q, k, v, kc, vc, pb, out_norm_a, ob, x, mod, w_out_b, norm2, w_router_t)


def _prefix_count(mask_f, tri):
    rows, n = mask_f.shape
    parts = []
    carry = jnp.zeros((rows, 1), F32)
    for j in range(n // LANES):
        blk = mask_f[:, j * LANES:(j + 1) * LANES]
        parts.append(jnp.dot(blk.astype(BF16), tri, preferred_element_type=F32) + carry)
        carry = carry + jnp.sum(blk, axis=1, keepdims=True)
    return jnp.concatenate(parts, axis=1)


def _topk_kernel(aff_ref, slot_ref, eb_ref, *, cap):
    a = aff_ref[...]
    rows = a.shape[0]

    def enough(t):
        return jnp.sum(jnp.where(a >= t, 1.0, 0.0), axis=1, keepdims=True) >= cap

    tiny = jnp.full((rows, 1), float(np.finfo(np.float32).tiny), F32)
    normal = enough(tiny)
    pw = tiny
    hi = jnp.full((rows, 1), 4.0, F32)
    for bit in range(6, -1, -1):
        cand = pw * (2.0 ** (1 << bit))
        ok = enough(cand)
        pw = jnp.where(ok, cand, pw)
        hi = jnp.where(ok, hi, cand)
    lo = jnp.where(normal, pw, 0.0)
    hi = jnp.where(normal, hi, tiny)
    step = lo
    for _ in range(MANTISSA_STEPS):
        step = step * 0.5
        cand = lo + step
        ok = enough(cand)
        lo = jnp.where(ok, cand, lo)
        hi = jnp.where(ok, hi, cand)
    above = a >= hi
    tie = jnp.logical_and(a >= lo, jnp.logical_not(above))
    n_above = jnp.sum(jnp.where(above, 1.0, 0.0), axis=1, keepdims=True)
    ri = lax.broadcasted_iota(jnp.int32, (LANES, LANES), 0)
    ci = lax.broadcasted_iota(jnp.int32, (LANES, LANES), 1)
    tri = jnp.where(ri <= ci, 1.0, 0.0).astype(BF16)
    tie_rank = _prefix_count(jnp.where(tie, 1.0, 0.0), tri)
    sel = jnp.logical_or(above, jnp.logical_and(tie, tie_rank <= cap - n_above))
    sel_f = jnp.where(sel, 1.0, 0.0)
    pos = _prefix_count(sel_f, tri) - 1.0
    slot_ref[...] = jnp.where(sel, pos, -1.0).astype(jnp.int32)
    lane = lax.broadcasted_iota(jnp.int32, (rows, LANES), 1)
    cnt = jnp.zeros((rows, LANES), F32)
    for j in range(a.shape[1] // TCH):
        cnt = jnp.where(lane == j, jnp.sum(sel_f[:, j * TCH:(j + 1) * TCH], axis=1, keepdims=True), cnt)
    before = jnp.where(ri < ci, 1.0, 0.0).astype(BF16)
    eb_ref[...] = jnp.dot(cnt.astype(BF16), before, preferred_element_type=F32).astype(jnp.int32)


def _topk_slots(aff2, cap):
    rows, n = aff2.shape
    tr = rows
    return pl.pallas_call(
        functools.partial(_topk_kernel, cap=cap),
        grid=(rows // tr,),
        in_specs=[pl.BlockSpec((tr, n), lambda i: (i, 0))],
        out_specs=[pl.BlockSpec((tr, n), lambda i: (i, 0)), pl.BlockSpec((tr, LANES), lambda i: (i, 0))],
        out_shape=[jax.ShapeDtypeStruct((rows, n), jnp.int32), jax.ShapeDtypeStruct((rows, LANES), jnp.int32)],
        compiler_params=_params("parallel"),
        name="expert_topk",
    )(aff2)


def _slot_window(eb_ref, bi, e, j, n_e, n_chunks, cap):
    base = (bi * n_e + e) * (n_chunks + 1) + j
    s0 = eb_ref[base]
    s1 = eb_ref[base + 1]
    start = jnp.minimum(s0 & -SLOT_ALIGN, cap - SLOT_W)
    return s0, s1, pl.multiple_of(start, SLOT_ALIGN)


def _window_overflow(eb_ref, bi, j, n_e, n_chunks, cap):
    over = None
    for e in range(n_e):
        _, s1, start = _slot_window(eb_ref, bi, e, j, n_e, n_chunks, cap)
        o = s1 > start + SLOT_W
        over = o if over is None else jnp.logical_or(over, o)
    return over


def _gather_kernel(eb_ref, h2_ref, slot_ref, aff_ref, xs_ref, gate_ref, *, cap, jc, n_chunks):
    bi = pl.program_id(0)
    jo = pl.program_id(1)
    n_e = slot_ref.shape[1]
    wi = lax.broadcasted_iota(jnp.int32, (SLOT_W, TCH), 0)
    wcol = lax.broadcasted_iota(jnp.int32, (SLOT_W, 1), 0)

    @pl.when(jo == 0)
    def _():
        xs_ref[...] = jnp.zeros_like(xs_ref)
        gate_ref[...] = jnp.zeros_like(gate_ref)

    for jj in range(jc):
        j = jo * jc + jj
        tok = slice(jj * TCH, (jj + 1) * TCH)
        h2c = h2_ref[0, tok, :]
        for g in range(n_e // EXPERT_GROUP):
            blocks, meta = [], []
            for q in range(EXPERT_GROUP):
                e = EXPERT_GROUP * g + q
                s0, s1, start = _slot_window(eb_ref, bi, e, j, n_e, n_chunks, cap)
                hit = slot_ref[0, e:e + 1, tok] - start == wi
                blocks.append(jnp.where(hit, 1.0, 0.0).astype(BF16))
                gate = jnp.sum(jnp.where(hit, aff_ref[0, e:e + 1, tok], 0.0), axis=1, keepdims=True)
                meta.append((e, s0, s1, start, gate))
            onehot = jnp.concatenate(blocks, axis=0)
            rows = jnp.dot(onehot, h2c, preferred_element_type=F32).astype(BF16)
            for q, (e, s0, s1, start, gate) in enumerate(meta):
                own = jnp.logical_and(wcol + start >= s0, wcol + start < s1)
                win = pl.ds(start, SLOT_W)
                xs_ref[0, e, win, :] = jnp.where(own, rows[q * SLOT_W:(q + 1) * SLOT_W], xs_ref[0, e, win, :])
                gate_ref[0, e, win, :] = jnp.where(own, gate, gate_ref[0, e, win, :])

    for jj in range(jc):
        j = jo * jc + jj
        tok = slice(jj * TCH, (jj + 1) * TCH)

        @pl.when(_window_overflow(eb_ref, bi, j, n_e, n_chunks, cap))
        def _():
            h2c = h2_ref[0, tok, :]
            ci = lax.broadcasted_iota(jnp.int32, (cap, TCH), 0)
            ccol = lax.broadcasted_iota(jnp.int32, (cap, 1), 0)
            for e in range(n_e):
                s0, s1, _ = _slot_window(eb_ref, bi, e, j, n_e, n_chunks, cap)
                hit = slot_ref[0, e:e + 1, tok] == ci
                rows = jnp.dot(jnp.where(hit, 1.0, 0.0).astype(BF16), h2c, preferred_element_type=F32).astype(BF16)
                gate = jnp.sum(jnp.where(hit, aff_ref[0, e:e + 1, tok], 0.0), axis=1, keepdims=True)
                own = jnp.logical_and(ccol >= s0, ccol < s1)
                xs_ref[0, e] = jnp.where(own, rows, xs_ref[0, e])
                gate_ref[0, e] = jnp.where(own, gate, gate_ref[0, e])


def _gather_tokens(ebound, h2, slot, aff, cap, jc=4):
    b, n, d = h2.shape
    e = slot.shape[1]
    n_chunks = n // TCH
    assert n_chunks % jc == 0
    tok = lambda i, j, *_: (i, j, 0)
    per_b = lambda i, j, *_: (i, 0, 0, 0)
    return pl.pallas_call(
        functools.partial(_gather_kernel, cap=cap, jc=jc, n_chunks=n_chunks),
        grid_spec=pltpu.PrefetchScalarGridSpec(
            num_scalar_prefetch=1,
            grid=(b, n_chunks // jc),
            in_specs=[pl.BlockSpec((1, jc * TCH, d), tok),
                      pl.BlockSpec((1, e, jc * TCH), lambda i, j, *_: (i, 0, j)),
                      pl.BlockSpec((1, e, jc * TCH), lambda i, j, *_: (i, 0, j))],
            out_specs=[pl.BlockSpec((1, e, cap, d), per_b),
                       pl.BlockSpec((1, e, cap, 1), per_b)]),
        out_shape=[jax.ShapeDtypeStruct((b, e, cap, d), BF16),
                   jax.ShapeDtypeStruct((b, e, cap, 1), F32)],
        compiler_params=_params("parallel", "arbitrary"),
        name="moe_gather",
    )(ebound, h2, slot, aff)


def _expert_kernel(xs_ref, wg_ref, wu_ref, wd_ref, gate_ref, y_ref, acc_ref, *, rb, n_f):
    f = pl.program_id(1)
    nb, _, cap, d = xs_ref.shape

    def step(first, last):
        wg = wg_ref[0].astype(BF16)
        wu = wu_ref[0].astype(BF16)
        wd = wd_ref[0].astype(BF16)
        for i in range(nb // rb):
            blk = slice(i * rb, (i + 1) * rb)
            xs = xs_ref[blk, 0].reshape(rb * cap, d)
            a = jnp.dot(xs, wg, preferred_element_type=F32)
            u = jnp.dot(xs, wu, preferred_element_type=F32)
            hm = (_silu(a) * u).astype(BF16)
            part = jnp.dot(hm, wd, preferred_element_type=F32).reshape(rb, cap, d)
            total = part if first else acc_ref[blk] + part
            if last:
                y_ref[blk, 0] = (total * gate_ref[blk, 0]).astype(BF16)
            else:
                acc_ref[blk] = total

    if n_f == 1:
        step(True, True)
    else:
        pl.when(f == 0)(lambda: step(True, False))
        if n_f > 2:
            pl.when(jnp.logical_and(f > 0, f < n_f - 1))(lambda: step(False, False))
        pl.when(f == n_f - 1)(lambda: step(False, True))


def _experts(xs, w_gate, w_up, w_down, gate, fc=768, rb=4):
    b, e, cap, d = xs.shape
    dff = w_gate.shape[2]
    per_e = lambda i, f: (0, i, 0, 0)
    return pl.pallas_call(
        functools.partial(_expert_kernel, rb=rb, n_f=dff // fc),
        grid=(e, dff // fc),
        in_specs=[pl.BlockSpec((b, 1, cap, d), per_e),
                  pl.BlockSpec((1, d, fc), lambda i, f: (i, 0, f)),
                  pl.BlockSpec((1, d, fc), lambda i, f: (i, 0, f)),
                  pl.BlockSpec((1, fc, d), lambda i, f: (i, f, 0)),
                  pl.BlockSpec((b, 1, cap, 1), per_e)],
        out_specs=pl.BlockSpec((b, 1, cap, d), per_e),
        out_shape=jax.ShapeDtypeStruct((b, e, cap, d), BF16),
        scratch_shapes=[pltpu.VMEM((b, cap, d), F32)],
        compiler_params=_params("parallel", "arbitrary"),
        name="moe_experts",
    )(xs, w_gate, w_up, w_down, gate)


def _combine_kernel(eb_ref, slot_ref, y_ref, xn_ref, mod_ref, nf_ref, o_ref, *, cap, jc, n_chunks):
    bi = pl.program_id(0)
    jo = pl.program_id(1)
    n_e = y_ref.shape[1]
    ci = lax.broadcasted_iota(jnp.int32, (TCH, EXPERT_GROUP * SLOT_W), 1)
    toks = [slice(jj * TCH, (jj + 1) * TCH) for jj in range(jc)]
    lane_e = lax.broadcasted_iota(jnp.int32, (1, n_e), 1)
    group_off = (lane_e & (EXPERT_GROUP - 1)) * SLOT_W
    eye = jnp.where(lax.broadcasted_iota(jnp.int32, (TCH, TCH), 0) == lax.broadcasted_iota(jnp.int32, (TCH, TCH), 1),
                    1.0, 0.0).astype(BF16)

    def token_major(tok):
        tile = slot_ref[0, :, tok].astype(F32).astype(BF16)
        return lax.dot_general(eye, tile, (((1,), (1,)), ((), ())), preferred_element_type=F32).astype(jnp.int32)

    def finish(tok, moe):
        x = xn_ref[0, tok] + mod_ref[0][5:6] * moe
        o_ref[0, tok] = _rms(x, nf_ref[...])

    moes = []
    for jj, tok in enumerate(toks):
        j = jo * jc + jj
        st = token_major(tok)
        starts = [_slot_window(eb_ref, bi, e, j, n_e, n_chunks, cap)[2] for e in range(n_e)]
        start_vec = jnp.zeros((1, n_e), jnp.int32)
        for e in range(n_e):
            start_vec = jnp.where(lane_e == e, starts[e], start_vec)
        rel = st - start_vec
        tgt = jnp.where(jnp.logical_and(rel >= 0, rel < SLOT_W), rel + group_off, -1)
        s_blocks, y_blocks = [], []
        for g in range(n_e // EXPERT_GROUP):
            col = None
            for q in reversed(range(EXPERT_GROUP)):
                e = EXPERT_GROUP * g + q
                col = tgt[:, e:e + 1] if col is None else jnp.where(ci < (q + 1) * SLOT_W, tgt[:, e:e + 1], col)
                y_blocks.insert(g * EXPERT_GROUP, y_ref[0, e, pl.ds(starts[e], SLOT_W), :])
            s_blocks.append(jnp.where(col == ci, 1.0, 0.0).astype(BF16))
        scat = jnp.concatenate(s_blocks, axis=1)
        ywin = jnp.concatenate(y_blocks, axis=0)
        moes.append(jnp.dot(scat, ywin, preferred_element_type=F32))
    for tok, moe in zip(toks, moes):
        finish(tok, moe)

    for jj, tok in enumerate(toks):
        @pl.when(_window_overflow(eb_ref, bi, jo * jc + jj, n_e, n_chunks, cap))
        def _():
            st = token_major(tok)
            cf = lax.broadcasted_iota(jnp.int32, (TCH, cap), 1)
            dense = jnp.concatenate(
                [jnp.where(st[:, e:e + 1] == cf, 1.0, 0.0).astype(BF16) for e in range(n_e)], axis=1)
            finish(tok, jnp.dot(dense, y_ref[0].reshape(n_e * cap, y_ref.shape[3]),
                                preferred_element_type=F32))


def _combine(ebound, slot, y, x_new, mod, norm_final, cap, jc=4):
    b, n, d = x_new.shape
    e = slot.shape[1]
    n_chunks = n // TCH
    assert n_chunks % jc == 0 and cap <= 256
    tile = lambda i, j, *_: (i, j, 0)
    return pl.pallas_call(
        functools.partial(_combine_kernel, cap=cap, jc=jc, n_chunks=n_chunks),
        grid_spec=pltpu.PrefetchScalarGridSpec(
            num_scalar_prefetch=1,
            grid=(b, n_chunks // jc),
            in_specs=[pl.BlockSpec((1, e, jc * TCH), lambda i, j, *_: (i, 0, j)),
                      pl.BlockSpec((1, e, cap, d), lambda i, j, *_: (i, 0, 0, 0)),
                      pl.BlockSpec((1, jc * TCH, d), tile),
                      pl.BlockSpec((1, N_MOD, d), lambda i, j, *_: (i, 0, 0)),
                      pl.BlockSpec((1, d), lambda i, j, *_: (0, 0))],
            out_specs=pl.BlockSpec((1, jc * TCH, d), tile)),
        out_shape=jax.ShapeDtypeStruct((b, n, d), F32),
        compiler_params=_params("parallel", "arbitrary"),
        name="moe_combine_norm",
    )(ebound, slot, y, x_new, mod, norm_final)


def kernel(x, c, ctx, c_ctx, w_mod, b_mod, norm1, w_in, rpb, w_s, b_s, gmlp_norm, out_norm_a, out_norm_b,
           w_out, norm2, w_router, w_gate, w_up, w_down, norm_final):
    b, n, d = x.shape
    assert w_mod.shape[0] == 1, "single-layer stack only"
    assert n % (GRID_W * ROWS_PER_STEP) == 0 and n // GRID_W >= NA_KH
    assert n % TCH == 0 and N_EXPERTS % EXPERT_GROUP == 0 and EXPERT_GROUP & (EXPERT_GROUP - 1) == 0
    cap = EC_CAPACITY_FACTOR * n // N_EXPERTS

    pad = (-(b + 1)) % 8
    cc = jnp.concatenate([c, c_ctx[None], jnp.zeros((pad, d), F32)], axis=0)
    m = _modulation(cc, w_mod[0], b_mod[0][None])
    mod = m.reshape(-1, N_MOD, d)

    w_in_b = w_in[0].astype(BF16)
    ws2 = w_s[0].astype(BF16).reshape(N_GROUPS_SG // 2, 2 * CHUNK, CHUNK)
    bs2 = jnp.broadcast_to(b_s[0].reshape(N_GROUPS_SG // 2, 2 * CHUNK, 1), (N_GROUPS_SG // 2, 2 * CHUNK, LANES))
    q, k, v, ob = _in_proj(x, mod, norm1, w_in_b, ws2, bs2, gmlp_norm, out_norm_b)

    kc, vc = _ctx_proj(ctx.reshape(b * CTX_LEN, d), mod, b, norm1, w_in_b)
    kc = kc.reshape(b, CTX_LEN, D_NA)
    vc = vc.reshape(b, CTX_LEN, D_NA)

    x_new, h2, aff = _attention_out(q, k, v, kc, vc, _bias_blocks(rpb[0]), out_norm_a,
                                    ob, x, mod, w_out[0].astype(BF16), norm2, w_router[0].T)

    slot, ebound = _topk_slots(aff.reshape(b * N_EXPERTS, n), cap)
    slot = slot.reshape(b, N_EXPERTS, n)
    ebound = ebound[:, :n // TCH + 1].reshape(-1)

    xs, gate = _gather_tokens(ebound, h2, slot, aff, cap)
    y = _experts(xs, w_gate[0], w_up[0], w_down[0], gate)
    return _combine(ebound, slot, y, x_new, mod, norm_final[None], cap)
```

```python
import functools

import numpy as np
import jax
import jax.numpy as jnp
from jax import lax
from jax.experimental import pallas as pl
from jax.experimental.pallas import tpu as pltpu

D_MODEL = 1024
GRID_W = 64
CTX_LEN = 256
N_HEADS_NA = 8
HEAD_DIM = 64
D_NA = N_HEADS_NA * HEAD_DIM
NA_KH = 8
NA_KW = 16
D_SG = D_MODEL - D_NA
N_GROUPS_SG = 8
SG_GROUP_DIM = D_SG // N_GROUPS_SG
CHUNK = 128
N_EXPERTS = 16
EC_CAPACITY_FACTOR = 2
D_EXPERT = 1536
N_MOD = 6
EPS = 1e-6
NEG_INF = -1e30
LOG2E = float(np.log2(np.e))

LANES = 128
VMEM_LIMIT = 56 * 1024 * 1024

F32 = jnp.float32
BF16 = jnp.bfloat16
HIGHEST = lax.Precision.HIGHEST

ROWS_PER_STEP = 16

MANTISSA_STEPS = 36

TCH = 256
SLOT_W = 64
SLOT_ALIGN = 16
EXPERT_GROUP = 4


def _params(*sem):
    return pltpu.CompilerParams(dimension_semantics=sem, vmem_limit_bytes=VMEM_LIMIT)


def _rms_mod(x, g, shift, scale):
    r = lax.rsqrt(jnp.mean(x * x, axis=-1, keepdims=True) + EPS)
    return (x * r) * g * (1.0 + scale) + shift


def _rms(x, g):
    return x * lax.rsqrt(jnp.mean(x * x, axis=-1, keepdims=True) + EPS) * g


def _gelu_tanh(x):
    return 0.5 * x * (1.0 + jnp.tanh(np.sqrt(2.0 / np.pi).astype(np.float32) * (x + 0.044715 * (x * x * x))))


def _silu(x):
    return x * jax.nn.sigmoid(x)


def _bf16_terms(x, n_terms):
    terms = []
    for _ in range(n_terms):
        t = x.astype(BF16)
        terms.append(t)
        x = x - t.astype(F32)
    return terms


def _mod_kernel(c_ref, w_ref, b_ref, o_ref):
    rows = c_ref.shape[0]
    s = jnp.concatenate(_bf16_terms(_silu(c_ref[...]), 3), axis=0)
    w_hi, w_lo = _bf16_terms(w_ref[...], 2)
    hi = jnp.dot(s, w_hi, preferred_element_type=F32)
    lo = jnp.dot(s[:2 * rows], w_lo, preferred_element_type=F32)
    small = (hi[2 * rows:] + lo[rows:]) + (hi[rows:2 * rows] + lo[:rows])
    o_ref[...] = (small + hi[:rows]) + b_ref[...]


def _modulation(cc, w_mod, b_mod):
    rows, d = cc.shape
    n = w_mod.shape[1]
    tn = 2048
    return pl.pallas_call(
        _mod_kernel,
        grid=(n // tn,),
        in_specs=[pl.BlockSpec((rows, d), lambda j: (0, 0)),
                  pl.BlockSpec((d, tn), lambda j: (0, j)),
                  pl.BlockSpec((1, tn), lambda j: (0, j))],
        out_specs=pl.BlockSpec((rows, tn), lambda j: (0, j)),
        out_shape=jax.ShapeDtypeStruct((rows, n), F32),
        compiler_params=_params("arbitrary"),
        name="modulation",
    )(cc, w_mod, b_mod)


def _in_kernel(x_ref, mod_ref, n1_ref, w_ref, ws_ref, bs_ref, gn_ref, onb_ref,
               q_ref, k_ref, v_ref, ob_ref, sp_ref, *, sub):
    mod = mod_ref[0]
    tiles = [slice(r, r + sub) for r in range(0, x_ref.shape[1], sub)]
    nch = sub // CHUNK
    lane = lax.broadcasted_iota(jnp.int32, (CHUNK, LANES), 1)
    first = lane < SG_GROUP_DIM

    def project(rs):
        hb = _rms_mod(x_ref[0, rs], n1_ref[...], mod[0:1], mod[1:2]).astype(BF16)
        qkv = jnp.dot(hb, w_ref[:, :3 * D_NA], preferred_element_type=F32)
        q_ref[0, rs] = (qkv[:, :D_NA] * (HEAD_DIM ** -0.5 * LOG2E)).astype(BF16)
        k_ref[0, rs] = qkv[:, D_NA:2 * D_NA].astype(BF16)
        v_ref[0, rs] = qkv[:, 2 * D_NA:].astype(BF16)
        return jnp.dot(hb, w_ref[:, 3 * D_NA:], preferred_element_type=F32)

    def gate_mlp(rs, uz):
        u = _gelu_tanh(uz[:, :D_SG])
        z = _gelu_tanh(uz[:, D_SG:])
        mu = jnp.mean(z, axis=-1, keepdims=True)
        zc = z - mu
        var = jnp.mean(zc * zc, axis=-1, keepdims=True)
        zb = (zc * lax.rsqrt(var + EPS) * gn_ref[...]).astype(BF16)
        for p in range(N_GROUPS_SG // 2):
            zp = jnp.concatenate(
                [zb[c * CHUNK:(c + 1) * CHUNK, p * LANES:(p + 1) * LANES] for c in range(nch)], axis=1)
            r = jnp.dot(ws_ref[p], zp, preferred_element_type=F32)
            bs = bs_ref[p]
            for c in range(nch):
                top = r[:CHUNK, c * LANES:(c + 1) * LANES] + bs[:CHUNK]
                bot = r[CHUNK:, c * LANES:(c + 1) * LANES] + bs[CHUNK:]
                sp_ref[rs.start + c * CHUNK:rs.start + (c + 1) * CHUNK, p * LANES:(p + 1) * LANES] = (
                    jnp.where(first, top, bot))
        ob = u * sp_ref[rs, :]
        ob_ref[0, rs] = _rms(ob, onb_ref[...]).astype(BF16)

    uz = {0: project(tiles[0])}
    for i, rs in enumerate(tiles):
        if i + 1 < len(tiles):
            uz[i + 1] = project(tiles[i + 1])
        gate_mlp(rs, uz.pop(i))


def _in_proj(x, mod, norm1, w_in_b, ws2, bs2, gmlp_norm, out_norm_b, tm=2048, sub=512):
    b, n, d = x.shape
    d_in = w_in_b.shape[1]
    full2 = lambda i, j: (0, 0)
    full3 = lambda i, j: (0, 0, 0)
    tile = lambda i, j: (i, j, 0)
    act = jax.ShapeDtypeStruct((b, n, D_NA), BF16)
    return pl.pallas_call(
        functools.partial(_in_kernel, sub=sub),
        grid=(b, n // tm),
        in_specs=[pl.BlockSpec((1, tm, d), tile),
                  pl.BlockSpec((1, N_MOD, d), lambda i, j: (i, 0, 0)),
                  pl.BlockSpec((1, d), full2),
                  pl.BlockSpec((d, d_in), full2),
                  pl.BlockSpec(ws2.shape, full3),
                  pl.BlockSpec(bs2.shape, full3),
                  pl.BlockSpec((1, D_SG), full2),
                  pl.BlockSpec((1, D_SG), full2)],
        out_specs=[pl.BlockSpec((1, tm, D_NA), tile)] * 4,
        out_shape=[act] * 4,
        scratch_shapes=[pltpu.VMEM((tm, D_SG), F32)],
        compiler_params=_params("parallel", "arbitrary"),
        name="in_proj_gmlp",
    )(x, mod, norm1, w_in_b, ws2, bs2, gmlp_norm, out_norm_b)


def _ctx_kernel(x_ref, mod_ref, n1_ref, wk_ref, wv_ref, k_ref, v_ref):
    mod = mod_ref[0]
    hb = _rms_mod(x_ref[...], n1_ref[...], mod[0:1], mod[1:2]).astype(BF16)
    k_ref[...] = jnp.dot(hb, wk_ref[...], preferred_element_type=F32).astype(BF16)
    v_ref[...] = jnp.dot(hb, wv_ref[...], preferred_element_type=F32).astype(BF16)


def _ctx_proj(ctx2, mod, mod_row, norm1, w_in_b, tm=512):
    rows, d = ctx2.shape
    act = jax.ShapeDtypeStruct((rows, D_NA), BF16)
    return pl.pallas_call(
        _ctx_kernel,
        grid=(rows // tm,),
        in_specs=[pl.BlockSpec((tm, d), lambda i: (i, 0)),
                  pl.BlockSpec((1, N_MOD, d), lambda i: (mod_row, 0, 0)),
                  pl.BlockSpec((1, d), lambda i: (0, 0)),
                  pl.BlockSpec((d, D_NA), lambda i: (0, 1)),
                  pl.BlockSpec((d, D_NA), lambda i: (0, 2))],
        out_specs=[pl.BlockSpec((tm, D_NA), lambda i: (i, 0))] * 2,
        out_shape=[act, act],
        compiler_params=_params("arbitrary"),
        name="ctx_kv_proj",
    )(ctx2, mod, norm1, w_in_b, w_in_b)


def _attn_out_kernel(q_ref, k_ref, v_ref, kc_ref, vc_ref, pb_ref, ona_ref,
                     ob_ref, x_ref, mod_ref, w_ref, n2_ref, wr_ref, xn_ref, h2_ref, aff_ref, *, rows, sub):
    kc = kc_ref[0]
    vc = vc_ref[0]
    nk = NA_KH * GRID_W
    lane = lax.broadcasted_iota(jnp.int32, (GRID_W, LANES), 1)
    first = lane < HEAD_DIM
    nt = (((1,), (1,)), ((), ()))
    units = []
    for r in range(ROWS_PER_STEP):
        row = pl.program_id(1) * ROWS_PER_STEP + r
        rs = jnp.clip(row - NA_KH // 2, 0, rows - NA_KH)
        start = pl.multiple_of(rs * GRID_W, GRID_W)
        off = rs - row + NA_KH - 1
        units += [(r, start, off, p) for p in range(N_HEADS_NA // 2)]

    def scores(r, start, off, p):
        sl = slice(p * LANES, (p + 1) * LANES)
        qp = q_ref[0, r * GRID_W:(r + 1) * GRID_W, sl]
        q2 = jnp.concatenate([jnp.where(first, qp, jnp.zeros_like(qp)),
                              jnp.where(first, jnp.zeros_like(qp), qp)], axis=0)
        bias = jnp.concatenate(
            [jnp.concatenate([pb_ref[2 * p + j, off + 2 * c] for c in range(NA_KH // 2)], axis=1)
             for j in range(2)], axis=0)
        s_nb = lax.dot_general(q2, k_ref[0, pl.ds(start, nk), sl], nt, preferred_element_type=F32) + bias
        s_cx = lax.dot_general(q2, kc[:, sl], nt, preferred_element_type=F32)
        return s_nb, s_cx

    def softmax(s_nb, s_cx):
        m = jnp.maximum(jnp.max(s_nb, axis=1, keepdims=True), jnp.max(s_cx, axis=1, keepdims=True))
        e_nb = jnp.exp2(s_nb - m)
        e_cx = jnp.exp2(s_cx - m)
        l = jnp.sum(e_nb, axis=1, keepdims=True) + jnp.sum(e_cx, axis=1, keepdims=True)
        return e_nb.astype(BF16), e_cx.astype(BF16), l

    def values(r, start, off, p, e_nb, e_cx, l):
        sl = slice(p * LANES, (p + 1) * LANES)
        o2 = (jnp.dot(e_nb, v_ref[0, pl.ds(start, nk), sl], preferred_element_type=F32)
              + jnp.dot(e_cx, vc[:, sl], preferred_element_type=F32)) / l
        return jnp.where(first, o2[:GRID_W], o2[GRID_W:])

    s, pr, o = {}, {}, {}
    n_u = len(units)
    for step in range(n_u + 2):
        if step < n_u:
            s[step] = scores(*units[step])
        if 0 <= step - 1 < n_u:
            pr[step - 1] = softmax(*s.pop(step - 1))
        if 0 <= step - 2 < n_u:
            o[step - 2] = values(*units[step - 2], *pr.pop(step - 2))
    n_p = N_HEADS_NA // 2
    oa_rows = []
    for r in range(ROWS_PER_STEP):
        out = jnp.concatenate([o[r * n_p + p] for p in range(n_p)], axis=1)
        oa_rows.append(_rms(out, ona_ref[...]).astype(BF16))

    mod = mod_ref[0]
    w_hi, w_lo = _bf16_terms(wr_ref[...], 2)
    w2 = jnp.concatenate([w_hi, w_lo], axis=0)
    rps = sub // GRID_W
    tiles = [slice(t0, t0 + sub) for t0 in range(0, ROWS_PER_STEP * GRID_W, sub)]
    mixes = [jnp.dot(jnp.concatenate(oa_rows[i * rps:(i + 1) * rps], axis=0), w_ref[:D_NA],
                     preferred_element_type=F32)
             + jnp.dot(ob_ref[0, rs], w_ref[D_NA:], preferred_element_type=F32) for i, rs in enumerate(tiles)]
    for rs, mix in zip(tiles, mixes):
        xn = x_ref[0, rs] + mod[2:3] * mix
        xn_ref[0, rs] = xn
        h2 = _rms_mod(xn, n2_ref[...], mod[3:4], mod[4:5])
        h_hi, h_lo = _bf16_terms(h2, 2)
        h2_ref[0, rs] = h_hi
        l_hi = lax.dot_general(w2, h_hi, nt, preferred_element_type=F32)
        l_lo = lax.dot_general(w_hi, h_lo, nt, preferred_element_type=F32)
        logits = l_hi[:N_EXPERTS] + l_hi[N_EXPERTS:] + l_lo
        e = jnp.exp(logits - jnp.max(logits, axis=0, keepdims=True))
        aff_ref[0, :, rs] = e / jnp.sum(e, axis=0, keepdims=True)


def _bias_blocks(rpb):
    n_ro, n_co = 2 * NA_KH - 1, 2 * NA_KW - 1
    qc = np.arange(GRID_W)[:, None]
    kc = np.arange(GRID_W)[None, :]
    cs = np.clip(qc - NA_KW // 2, 0, GRID_W - NA_KW)
    col_ok = (kc >= cs) & (kc < cs + NA_KW)
    spread = ((kc - qc + NA_KW - 1)[None] == np.arange(n_co)[:, None, None]) & col_ok[None]
    blocks = jnp.dot(rpb.reshape(-1, n_co), jnp.asarray(spread.reshape(n_co, -1), F32), precision=HIGHEST)
    blocks = jnp.where(jnp.asarray(col_ok.reshape(-1)), blocks * LOG2E, NEG_INF)
    blocks = blocks.reshape(N_HEADS_NA, n_ro, GRID_W, GRID_W)
    return jnp.concatenate([blocks[:, :-1], blocks[:, 1:]], axis=-1)


def _attention_out(q, k, v, kc, vc, pb, out_norm_a, ob, x, mod, w_out_b, norm2, w_router_t, sub=512):
    b, n, d = x.shape
    rows = n // GRID_W
    tq = ROWS_PER_STEP * GRID_W
    per_b = lambda i, t: (i, 0, 0)
    tile = lambda i, t: (i, t, 0)
    full = lambda i, t: (0, 0)
    return pl.pallas_call(
        functools.partial(_attn_out_kernel, rows=rows, sub=sub),
        grid=(b, n // tq),
        in_specs=[pl.BlockSpec((1, tq, D_NA), tile),
                  pl.BlockSpec((1, n, D_NA), per_b),
                  pl.BlockSpec((1, n, D_NA), per_b),
                  pl.BlockSpec((1, CTX_LEN, D_NA), per_b),
                  pl.BlockSpec((1, CTX_LEN, D_NA), per_b),
                  pl.BlockSpec(pb.shape, lambda i, t: (0, 0, 0, 0)),
                  pl.BlockSpec((1, D_NA), full),
                  pl.BlockSpec((1, tq, D_SG), tile),
                  pl.BlockSpec((1, tq, d), tile),
                  pl.BlockSpec((1, N_MOD, d), per_b),
                  pl.BlockSpec(w_out_b.shape, full),
                  pl.BlockSpec((1, d), full),
                  pl.BlockSpec((N_EXPERTS, d), full)],
        out_specs=[pl.BlockSpec((1, tq, d), tile),
                   pl.BlockSpec((1, tq, d), tile),
                   pl.BlockSpec((1, N_EXPERTS, tq), lambda i, t: (i, 0, t))],
        out_shape=[jax.ShapeDtypeStruct((b, n, d), F32),
                   jax.ShapeDtypeStruct((b, n, d), BF16),
                   jax.ShapeDtypeStruct((b, N_EXPERTS, n), F32)],
        compiler_params=_params("parallel", "arbitrary"),
        name="attention_out_router",
    )(q, k, v, kc, vc, pb, out_norm_a, ob, x, mod, w_out_b, norm2, w_router_t)


def _prefix_count(mask_f, tri):
    rows, n = mask_f.shape
    parts = []
    carry = jnp.zeros((rows, 1), F32)
    for j in range(n // LANES):
        blk = mask_f[:, j * LANES:(j + 1) * LANES]
        parts.append(jnp.dot(blk.astype(BF16), tri, preferred_element_type=F32) + carry)
        carry = carry + jnp.sum(blk, axis=1, keepdims=True)
    return jnp.concatenate(parts, axis=1)


def _topk_kernel(aff_ref, slot_ref, eb_ref, *, cap):
    a = aff_ref[...]
    rows = a.shape[0]

    def enough(t):
        return jnp.sum(jnp.where(a >= t, 1.0, 0.0), axis=1, keepdims=True) >= cap

    tiny = jnp.full((rows, 1), float(np.finfo(np.float32).tiny), F32)
    normal = enough(tiny)
    pw = tiny
    hi = jnp.full((rows, 1), 4.0, F32)
    for bit in range(6, -1, -1):
        cand = pw * (2.0 ** (1 << bit))
        ok = enough(cand)
        pw = jnp.where(ok, cand, pw)
        hi = jnp.where(ok, hi, cand)
    lo = jnp.where(normal, pw, 0.0)
    hi = jnp.where(normal, hi, tiny)
    step = lo
    for _ in range(MANTISSA_STEPS):
        step = step * 0.5
        cand = lo + step
        ok = enough(cand)
        lo = jnp.where(ok, cand, lo)
        hi = jnp.where(ok, hi, cand)
    above = a >= hi
    tie = jnp.logical_and(a >= lo, jnp.logical_not(above))
    n_above = jnp.sum(jnp.where(above, 1.0, 0.0), axis=1, keepdims=True)
    ri = lax.broadcasted_iota(jnp.int32, (LANES, LANES), 0)
    ci = lax.broadcasted_iota(jnp.int32, (LANES, LANES), 1)
    tri = jnp.where(ri <= ci, 1.0, 0.0).astype(BF16)
    tie_rank = _prefix_count(jnp.where(tie, 1.0, 0.0), tri)
    sel = jnp.logical_or(above, jnp.logical_and(tie, tie_rank <= cap - n_above))
    sel_f = jnp.where(sel, 1.0, 0.0)
    pos = _prefix_count(sel_f, tri) - 1.0
    slot_ref[...] = jnp.where(sel, pos, -1.0).astype(jnp.int32)
    lane = lax.broadcasted_iota(jnp.int32, (rows, LANES), 1)
    cnt = jnp.zeros((rows, LANES), F32)
    for j in range(a.shape[1] // TCH):
        cnt = jnp.where(lane == j, jnp.sum(sel_f[:, j * TCH:(j + 1) * TCH], axis=1, keepdims=True), cnt)
    before = jnp.where(ri < ci, 1.0, 0.0).astype(BF16)
    eb_ref[...] = jnp.dot(cnt.astype(BF16), before, preferred_element_type=F32).astype(jnp.int32)


def _topk_slots(aff2, cap):
    rows, n = aff2.shape
    tr = rows
    return pl.pallas_call(
        functools.partial(_topk_kernel, cap=cap),
        grid=(rows // tr,),
        in_specs=[pl.BlockSpec((tr, n), lambda i: (i, 0))],
        out_specs=[pl.BlockSpec((tr, n), lambda i: (i, 0)), pl.BlockSpec((tr, LANES), lambda i: (i, 0))],
        out_shape=[jax.ShapeDtypeStruct((rows, n), jnp.int32), jax.ShapeDtypeStruct((rows, LANES), jnp.int32)],
        compiler_params=_params("parallel"),
        name="expert_topk",
    )(aff2)


def _slot_window(eb_ref, bi, e, j, n_e, n_chunks, cap):
    base = (bi * n_e + e) * (n_chunks + 1) + j
    s0 = eb_ref[base]
    s1 = eb_ref[base + 1]
    start = jnp.minimum(s0 & -SLOT_ALIGN, cap - SLOT_W)
    return s0, s1, pl.multiple_of(start, SLOT_ALIGN)


def _window_overflow(eb_ref, bi, j, n_e, n_chunks, cap):
    over = None
    for e in range(n_e):
        _, s1, start = _slot_window(eb_ref, bi, e, j, n_e, n_chunks, cap)
        o = s1 > start + SLOT_W
        over = o if over is None else jnp.logical_or(over, o)
    return over


def _gather_kernel(eb_ref, h2_ref, slot_ref, aff_ref, xs_ref, gate_ref, *, cap, jc, n_chunks):
    bi = pl.program_id(0)
    jo = pl.program_id(1)
    n_e = slot_ref.shape[1]
    wi = lax.broadcasted_iota(jnp.int32, (SLOT_W, TCH), 0)
    wcol = lax.broadcasted_iota(jnp.int32, (SLOT_W, 1), 0)

    @pl.when(jo == 0)
    def _():
        xs_ref[...] = jnp.zeros_like(xs_ref)
        gate_ref[...] = jnp.zeros_like(gate_ref)

    for jj in range(jc):
        j = jo * jc + jj
        tok = slice(jj * TCH, (jj + 1) * TCH)
        h2c = h2_ref[0, tok, :]
        for g in range(n_e // EXPERT_GROUP):
            blocks, meta = [], []
            for q in range(EXPERT_GROUP):
                e = EXPERT_GROUP * g + q
                s0, s1, start = _slot_window(eb_ref, bi, e, j, n_e, n_chunks, cap)
                hit = slot_ref[0, e:e + 1, tok] - start == wi
                blocks.append(jnp.where(hit, 1.0, 0.0).astype(BF16))
                gate = jnp.sum(jnp.where(hit, aff_ref[0, e:e + 1, tok], 0.0), axis=1, keepdims=True)
                meta.append((e, s0, s1, start, gate))
            onehot = jnp.concatenate(blocks, axis=0)
            rows = jnp.dot(onehot, h2c, preferred_element_type=F32).astype(BF16)
            for q, (e, s0, s1, start, gate) in enumerate(meta):
                own = jnp.logical_and(wcol + start >= s0, wcol + start < s1)
                win = pl.ds(start, SLOT_W)
                xs_ref[0, e, win, :] = jnp.where(own, rows[q * SLOT_W:(q + 1) * SLOT_W], xs_ref[0, e, win, :])
                gate_ref[0, e, win, :] = jnp.where(own, gate, gate_ref[0, e, win, :])

    for jj in range(jc):
        j = jo * jc + jj
        tok = slice(jj * TCH, (jj + 1) * TCH)

        @pl.when(_window_overflow(eb_ref, bi, j, n_e, n_chunks, cap))
        def _():
            h2c = h2_ref[0, tok, :]
            ci = lax.broadcasted_iota(jnp.int32, (cap, TCH), 0)
            ccol = lax.broadcasted_iota(jnp.int32, (cap, 1), 0)
            for e in range(n_e):
                s0, s1, _ = _slot_window(eb_ref, bi, e, j, n_e, n_chunks, cap)
                hit = slot_ref[0, e:e + 1, tok] == ci
                rows = jnp.dot(jnp.where(hit, 1.0, 0.0).astype(BF16), h2c, preferred_element_type=F32).astype(BF16)
                gate = jnp.sum(jnp.where(hit, aff_ref[0, e:e + 1, tok], 0.0), axis=1, keepdims=True)
                own = jnp.logical_and(ccol >= s0, ccol < s1)
                xs_ref[0, e] = jnp.where(own, rows, xs_ref[0, e])
                gate_ref[0, e] = jnp.where(own, gate, gate_ref[0, e])


def _gather_tokens(ebound, h2, slot, aff, cap, jc=8):
    b, n, d = h2.shape
    e = slot.shape[1]
    n_chunks = n // TCH
    assert n_chunks % jc == 0
    tok = lambda i, j, *_: (i, j, 0)
    per_b = lambda i, j, *_: (i, 0, 0, 0)
    return pl.pallas_call(
        functools.partial(_gather_kernel, cap=cap, jc=jc, n_chunks=n_chunks),
        grid_spec=pltpu.PrefetchScalarGridSpec(
            num_scalar_prefetch=1,
            grid=(b, n_chunks // jc),
            in_specs=[pl.BlockSpec((1, jc * TCH, d), tok),
                      pl.BlockSpec((1, e, jc * TCH), lambda i, j, *_: (i, 0, j)),
                      pl.BlockSpec((1, e, jc * TCH), lambda i, j, *_: (i, 0, j))],
            out_specs=[pl.BlockSpec((1, e, cap, d), per_b),
                       pl.BlockSpec((1, e, cap, 1), per_b)]),
        out_shape=[jax.ShapeDtypeStruct((b, e, cap, d), BF16),
                   jax.ShapeDtypeStruct((b, e, cap, 1), F32)],
        compiler_params=_params("parallel", "arbitrary"),
        name="moe_gather",
    )(ebound, h2, slot, aff)


def _expert_kernel(xs_ref, wg_ref, wu_ref, wd_ref, gate_ref, y_ref, acc_ref, *, rb, n_f):
    f = pl.program_id(1)
    nb, _, cap, d = xs_ref.shape

    def step(first, last):
        wg = wg_ref[0].astype(BF16)
        wu = wu_ref[0].astype(BF16)
        wd = wd_ref[0].astype(BF16)
        for i in range(nb // rb):
            blk = slice(i * rb, (i + 1) * rb)
            xs = xs_ref[blk, 0].reshape(rb * cap, d)
            a = jnp.dot(xs, wg, preferred_element_type=F32)
            u = jnp.dot(xs, wu, preferred_element_type=F32)
            hm = (_silu(a) * u).astype(BF16)
            part = jnp.dot(hm, wd, preferred_element_type=F32).reshape(rb, cap, d)
            total = part if first else acc_ref[blk] + part
            if last:
                y_ref[blk, 0] = (total * gate_ref[blk, 0]).astype(BF16)
            else:
                acc_ref[blk] = total

    if n_f == 1:
        step(True, True)
    else:
        pl.when(f == 0)(lambda: step(True, False))
        if n_f > 2:
            pl.when(jnp.logical_and(f > 0, f < n_f - 1))(lambda: step(False, False))
        pl.when(f == n_f - 1)(lambda: step(False, True))


def _experts(xs, w_gate, w_up, w_down, gate, fc=768, rb=4):
    b, e, cap, d = xs.shape
    dff = w_gate.shape[2]
    per_e = lambda i, f: (0, i, 0, 0)
    return pl.pallas_call(
        functools.partial(_expert_kernel, rb=rb, n_f=dff // fc),
        grid=(e, dff // fc),
        in_specs=[pl.BlockSpec((b, 1, cap, d), per_e),
                  pl.BlockSpec((1, d, fc), lambda i, f: (i, 0, f)),
                  pl.BlockSpec((1, d, fc), lambda i, f: (i, 0, f)),
                  pl.BlockSpec((1, fc, d), lambda i, f: (i, f, 0)),
                  pl.BlockSpec((b, 1, cap, 1), per_e)],
        out_specs=pl.BlockSpec((b, 1, cap, d), per_e),
        out_shape=jax.ShapeDtypeStruct((b, e, cap, d), BF16),
        scratch_shapes=[pltpu.VMEM((b, cap, d), F32)],
        compiler_params=_params("parallel", "arbitrary"),
        name="moe_experts",
    )(xs, w_gate, w_up, w_down, gate)


def _combine_kernel(eb_ref, slot_t_ref, y_ref, xn_ref, mod_ref, nf_ref, o_ref, *, cap, jc, n_chunks):
    bi = pl.program_id(0)
    jo = pl.program_id(1)
    n_e = y_ref.shape[1]
    ci = lax.broadcasted_iota(jnp.int32, (TCH, EXPERT_GROUP * SLOT_W), 1)
    toks = [slice(jj * TCH, (jj + 1) * TCH) for jj in range(jc)]
    lane_e = lax.broadcasted_iota(jnp.int32, (1, n_e), 1)
    group_off = (lane_e & (EXPERT_GROUP - 1)) * SLOT_W

    def finish(tok, moe):
        x = xn_ref[0, tok] + mod_ref[0][5:6] * moe
        o_ref[0, tok] = _rms(x, nf_ref[...])

    moes = []
    for jj, tok in enumerate(toks):
        j = jo * jc + jj
        st = slot_t_ref[0, tok, :]
        starts = [_slot_window(eb_ref, bi, e, j, n_e, n_chunks, cap)[2] for e in range(n_e)]
        start_vec = jnp.zeros((1, n_e), jnp.int32)
        for e in range(n_e):
            start_vec = jnp.where(lane_e == e, starts[e], start_vec)
        rel = st - start_vec
        tgt = jnp.where(jnp.logical_and(rel >= 0, rel < SLOT_W), rel + group_off, -1)
        s_blocks, y_blocks = [], []
        for g in range(n_e // EXPERT_GROUP):
            col = None
            for q in reversed(range(EXPERT_GROUP)):
                e = EXPERT_GROUP * g + q
                col = tgt[:, e:e + 1] if col is None else jnp.where(ci < (q + 1) * SLOT_W, tgt[:, e:e + 1], col)
                y_blocks.insert(g * EXPERT_GROUP, y_ref[0, e, pl.ds(starts[e], SLOT_W), :])
            s_blocks.append(jnp.where(col == ci, 1.0, 0.0).astype(BF16))
        scat = jnp.concatenate(s_blocks, axis=1)
        ywin = jnp.concatenate(y_blocks, axis=0)
        moes.append(jnp.dot(scat, ywin, preferred_element_type=F32))
    for tok, moe in zip(toks, moes):
        finish(tok, moe)

    for jj, tok in enumerate(toks):
        @pl.when(_window_overflow(eb_ref, bi, jo * jc + jj, n_e, n_chunks, cap))
        def _():
            st = slot_t_ref[0, tok, :]
            cf = lax.broadcasted_iota(jnp.int32, (TCH, cap), 1)
            dense = jnp.concatenate(
                [jnp.where(st[:, e:e + 1] == cf, 1.0, 0.0).astype(BF16) for e in range(n_e)], axis=1)
            finish(tok, jnp.dot(dense, y_ref[0].reshape(n_e * cap, y_ref.shape[3]),
                                preferred_element_type=F32))


def _combine(ebound, slot_t, y, x_new, mod, norm_final, cap, jc=4):
    b, n, d = x_new.shape
    e = slot_t.shape[2]
    n_chunks = n // TCH
    assert n_chunks % jc == 0
    tile = lambda i, j, *_: (i, j, 0)
    return pl.pallas_call(
        functools.partial(_combine_kernel, cap=cap, jc=jc, n_chunks=n_chunks),
        grid_spec=pltpu.PrefetchScalarGridSpec(
            num_scalar_prefetch=1,
            grid=(b, n_chunks // jc),
            in_specs=[pl.BlockSpec((1, jc * TCH, e), tile),
                      pl.BlockSpec((1, e, cap, d), lambda i, j, *_: (i, 0, 0, 0)),
                      pl.BlockSpec((1, jc * TCH, d), tile),
                      pl.BlockSpec((1, N_MOD, d), lambda i, j, *_: (i, 0, 0)),
                      pl.BlockSpec((1, d), lambda i, j, *_: (0, 0))],
            out_specs=pl.BlockSpec((1, jc * TCH, d), tile)),
        out_shape=jax.ShapeDtypeStruct((b, n, d), F32),
        compiler_params=_params("parallel", "arbitrary"),
        name="moe_combine_norm",
    )(ebound, slot_t, y, x_new, mod, norm_final)


def kernel(x, c, ctx, c_ctx, w_mod, b_mod, norm1, w_in, rpb, w_s, b_s, gmlp_norm, out_norm_a, out_norm_b,
           w_out, norm2, w_router, w_gate, w_up, w_down, norm_final):
    b, n, d = x.shape
    assert w_mod.shape[0] == 1, "single-layer stack only"
    assert n % (GRID_W * ROWS_PER_STEP) == 0 and n // GRID_W >= NA_KH
    assert n % TCH == 0 and N_EXPERTS % EXPERT_GROUP == 0 and EXPERT_GROUP & (EXPERT_GROUP - 1) == 0
    cap = EC_CAPACITY_FACTOR * n // N_EXPERTS

    pad = (-(b + 1)) % 8
    cc = jnp.concatenate([c, c_ctx[None], jnp.zeros((pad, d), F32)], axis=0)
    m = _modulation(cc, w_mod[0], b_mod[0][None])
    mod = m.reshape(-1, N_MOD, d)

    w_in_b = w_in[0].astype(BF16)
    ws2 = w_s[0].astype(BF16).reshape(N_GROUPS_SG // 2, 2 * CHUNK, CHUNK)
    bs2 = jnp.broadcast_to(b_s[0].reshape(N_GROUPS_SG // 2, 2 * CHUNK, 1), (N_GROUPS_SG // 2, 2 * CHUNK, LANES))
    q, k, v, ob = _in_proj(x, mod, norm1, w_in_b, ws2, bs2, gmlp_norm, out_norm_b)

    kc, vc = _ctx_proj(ctx.reshape(b * CTX_LEN, d), mod, b, norm1, w_in_b)
    kc = kc.reshape(b, CTX_LEN, D_NA)
    vc = vc.reshape(b, CTX_LEN, D_NA)

    x_new, h2, aff = _attention_out(q, k, v, kc, vc, _bias_blocks(rpb[0]), out_norm_a,
                                    ob, x, mod, w_out[0].astype(BF16), norm2, w_router[0].T)

    slot, ebound = _topk_slots(aff.reshape(b * N_EXPERTS, n), cap)
    slot = slot.reshape(b, N_EXPERTS, n)
    ebound = ebound[:, :n // TCH + 1].reshape(-1)

    xs, gate = _gather_tokens(ebound, h2, slot, aff, cap)
    y = _experts(xs, w_gate[0], w_up[0], w_down[0], gate)
    return _combine(ebound, jnp.swapaxes(slot, 1, 2), y, x_new, mod, norm_final[None], cap)
```

```python
import functools

import numpy as np
import jax
import jax.numpy as jnp
from jax import lax
from jax.experimental import pallas as pl
from jax.experimental.pallas import tpu as pltpu

D_MODEL = 1024
GRID_W = 64
CTX_LEN = 256
N_HEADS_NA = 8
HEAD_DIM = 64
D_NA = N_HEADS_NA * HEAD_DIM
NA_KH = 8
NA_KW = 16
D_SG = D_MODEL - D_NA
N_GROUPS_SG = 8
SG_GROUP_DIM = D_SG // N_GROUPS_SG
CHUNK = 128
N_EXPERTS = 16
EC_CAPACITY_FACTOR = 2
D_EXPERT = 1536
N_MOD = 6
EPS = 1e-6
NEG_INF = -1e30
LOG2E = float(np.log2(np.e))

LANES = 128
VMEM_LIMIT = 56 * 1024 * 1024

F32 = jnp.float32
BF16 = jnp.bfloat16
HIGHEST = lax.Precision.HIGHEST

ROWS_PER_STEP = 16

MANTISSA_STEPS = 36

TCH = 256
SLOT_W = 64
SLOT_ALIGN = 16
EXPERT_GROUP = 4


def _params(*sem):
    return pltpu.CompilerParams(dimension_semantics=sem, vmem_limit_bytes=VMEM_LIMIT)


def _rms_mod(x, g, shift, scale):
    r = lax.rsqrt(jnp.mean(x * x, axis=-1, keepdims=True) + EPS)
    return (x * r) * g * (1.0 + scale) + shift


def _rms(x, g):
    return x * lax.rsqrt(jnp.mean(x * x, axis=-1, keepdims=True) + EPS) * g


def _gelu_tanh(x):
    return 0.5 * x * (1.0 + jnp.tanh(np.sqrt(2.0 / np.pi).astype(np.float32) * (x + 0.044715 * (x * x * x))))


def _silu(x):
    return x * jax.nn.sigmoid(x)


def _bf16_terms(x, n_terms):
    terms = []
    for _ in range(n_terms):
        t = x.astype(BF16)
        terms.append(t)
        x = x - t.astype(F32)
    return terms


def _mod_kernel(c_ref, w_ref, b_ref, o_ref):
    rows = c_ref.shape[0]
    s = jnp.concatenate(_bf16_terms(_silu(c_ref[...]), 3), axis=0)
    w_hi, w_lo = _bf16_terms(w_ref[...], 2)
    hi = jnp.dot(s, w_hi, preferred_element_type=F32)
    lo = jnp.dot(s[:2 * rows], w_lo, preferred_element_type=F32)
    small = (hi[2 * rows:] + lo[rows:]) + (hi[rows:2 * rows] + lo[:rows])
    o_ref[...] = (small + hi[:rows]) + b_ref[...]


def _modulation(cc, w_mod, b_mod):
    rows, d = cc.shape
    n = w_mod.shape[1]
    tn = 1024
    return pl.pallas_call(
        _mod_kernel,
        grid=(n // tn,),
        in_specs=[pl.BlockSpec((rows, d), lambda j: (0, 0)),
                  pl.BlockSpec((d, tn), lambda j: (0, j)),
                  pl.BlockSpec((1, tn), lambda j: (0, j))],
        out_specs=pl.BlockSpec((rows, tn), lambda j: (0, j)),
        out_shape=jax.ShapeDtypeStruct((rows, n), F32),
        compiler_params=_params("arbitrary"),
        name="modulation",
    )(cc, w_mod, b_mod)


def _in_kernel(x_ref, mod_ref, n1_ref, w_ref, ws_ref, bs_ref, gn_ref, onb_ref,
               q_ref, k_ref, v_ref, ob_ref, sp_ref, *, sub):
    mod = mod_ref[0]
    tiles = [slice(r, r + sub) for r in range(0, x_ref.shape[1], sub)]
    nch = sub // CHUNK
    lane = lax.broadcasted_iota(jnp.int32, (CHUNK, LANES), 1)
    first = lane < SG_GROUP_DIM

    def project(rs):
        hb = _rms_mod(x_ref[0, rs], n1_ref[...], mod[0:1], mod[1:2]).astype(BF16)
        qkv = jnp.dot(hb, w_ref[:, :3 * D_NA], preferred_element_type=F32)
        q_ref[0, rs] = (qkv[:, :D_NA] * (HEAD_DIM ** -0.5 * LOG2E)).astype(BF16)
        k_ref[0, rs] = qkv[:, D_NA:2 * D_NA].astype(BF16)
        v_ref[0, rs] = qkv[:, 2 * D_NA:].astype(BF16)
        return jnp.dot(hb, w_ref[:, 3 * D_NA:], preferred_element_type=F32)

    def gate_mlp(rs, uz):
        u = _gelu_tanh(uz[:, :D_SG])
        z = _gelu_tanh(uz[:, D_SG:])
        mu = jnp.mean(z, axis=-1, keepdims=True)
        zc = z - mu
        var = jnp.mean(zc * zc, axis=-1, keepdims=True)
        zb = (zc * lax.rsqrt(var + EPS) * gn_ref[...]).astype(BF16)
        for p in range(N_GROUPS_SG // 2):
            zp = jnp.concatenate(
                [zb[c * CHUNK:(c + 1) * CHUNK, p * LANES:(p + 1) * LANES] for c in range(nch)], axis=1)
            r = jnp.dot(ws_ref[p], zp, preferred_element_type=F32)
            bs = bs_ref[p]
            for c in range(nch):
                top = r[:CHUNK, c * LANES:(c + 1) * LANES] + bs[:CHUNK]
                bot = r[CHUNK:, c * LANES:(c + 1) * LANES] + bs[CHUNK:]
                sp_ref[rs.start + c * CHUNK:rs.start + (c + 1) * CHUNK, p * LANES:(p + 1) * LANES] = (
                    jnp.where(first, top, bot))
        ob = u * sp_ref[rs, :]
        ob_ref[0, rs] = _rms(ob, onb_ref[...]).astype(BF16)

    uz = {0: project(tiles[0])}
    for i, rs in enumerate(tiles):
        if i + 1 < len(tiles):
            uz[i + 1] = project(tiles[i + 1])
        gate_mlp(rs, uz.pop(i))


def _in_proj(x, mod, norm1, w_in_b, ws2, bs2, gmlp_norm, out_norm_b, tm=2048, sub=512):
    b, n, d = x.shape
    d_in = w_in_b.shape[1]
    full2 = lambda i, j: (0, 0)
    full3 = lambda i, j: (0, 0, 0)
    tile = lambda i, j: (i, j, 0)
    act = jax.ShapeDtypeStruct((b, n, D_NA), BF16)
    return pl.pallas_call(
        functools.partial(_in_kernel, sub=sub),
        grid=(b, n // tm),
        in_specs=[pl.BlockSpec((1, tm, d), tile),
                  pl.BlockSpec((1, N_MOD, d), lambda i, j: (i, 0, 0)),
                  pl.BlockSpec((1, d), full2),
                  pl.BlockSpec((d, d_in), full2),
                  pl.BlockSpec(ws2.shape, full3),
                  pl.BlockSpec(bs2.shape, full3),
                  pl.BlockSpec((1, D_SG), full2),
                  pl.BlockSpec((1, D_SG), full2)],
        out_specs=[pl.BlockSpec((1, tm, D_NA), tile)] * 4,
        out_shape=[act] * 4,
        scratch_shapes=[pltpu.VMEM((tm, D_SG), F32)],
        compiler_params=_params("parallel", "arbitrary"),
        name="in_proj_gmlp",
    )(x, mod, norm1, w_in_b, ws2, bs2, gmlp_norm, out_norm_b)


def _ctx_kernel(x_ref, mod_ref, n1_ref, wk_ref, wv_ref, k_ref, v_ref):
    mod = mod_ref[0]
    hb = _rms_mod(x_ref[...], n1_ref[...], mod[0:1], mod[1:2]).astype(BF16)
    k_ref[...] = jnp.dot(hb, wk_ref[...], preferred_element_type=F32).astype(BF16)
    v_ref[...] = jnp.dot(hb, wv_ref[...], preferred_element_type=F32).astype(BF16)


def _ctx_proj(ctx2, mod, mod_row, norm1, w_in_b, tm=512):
    rows, d = ctx2.shape
    act = jax.ShapeDtypeStruct((rows, D_NA), BF16)
    return pl.pallas_call(
        _ctx_kernel,
        grid=(rows // tm,),
        in_specs=[pl.BlockSpec((tm, d), lambda i: (i, 0)),
                  pl.BlockSpec((1, N_MOD, d), lambda i: (mod_row, 0, 0)),
                  pl.BlockSpec((1, d), lambda i: (0, 0)),
                  pl.BlockSpec((d, D_NA), lambda i: (0, 1)),
                  pl.BlockSpec((d, D_NA), lambda i: (0, 2))],
        out_specs=[pl.BlockSpec((tm, D_NA), lambda i: (i, 0))] * 2,
        out_shape=[act, act],
        compiler_params=_params("arbitrary"),
        name="ctx_kv_proj",
    )(ctx2, mod, norm1, w_in_b, w_in_b)


def _attn_out_kernel(q_ref, k_ref, v_ref, kc_ref, vc_ref, pb_ref, ona_ref,
                     ob_ref, x_ref, mod_ref, w_ref, n2_ref, wr_ref, xn_ref, h2_ref, aff_ref, *, rows, sub):
    kc = kc_ref[0]
    vc = vc_ref[0]
    nk = NA_KH * GRID_W
    lane = lax.broadcasted_iota(jnp.int32, (GRID_W, LANES), 1)
    first = lane < HEAD_DIM
    nt = (((1,), (1,)), ((), ()))
    units = []
    for r in range(ROWS_PER_STEP):
        row = pl.program_id(1) * ROWS_PER_STEP + r
        rs = jnp.clip(row - NA_KH // 2, 0, rows - NA_KH)
        start = pl.multiple_of(rs * GRID_W, GRID_W)
        off = rs - row + NA_KH - 1
        units += [(r, start, off, p) for p in range(N_HEADS_NA // 2)]

    def scores(r, start, off, p):
        sl = slice(p * LANES, (p + 1) * LANES)
        qp = q_ref[0, r * GRID_W:(r + 1) * GRID_W, sl]
        q2 = jnp.concatenate([jnp.where(first, qp, jnp.zeros_like(qp)),
                              jnp.where(first, jnp.zeros_like(qp), qp)], axis=0)
        bias = jnp.concatenate(
            [jnp.concatenate([pb_ref[2 * p + j, off + 2 * c] for c in range(NA_KH // 2)], axis=1)
             for j in range(2)], axis=0)
        s_nb = lax.dot_general(q2, k_ref[0, pl.ds(start, nk), sl], nt, preferred_element_type=F32) + bias
        s_cx = lax.dot_general(q2, kc[:, sl], nt, preferred_element_type=F32)
        return s_nb, s_cx

    def softmax(s_nb, s_cx):
        m = jnp.maximum(jnp.max(s_nb, axis=1, keepdims=True), jnp.max(s_cx, axis=1, keepdims=True))
        e_nb = jnp.exp2(s_nb - m)
        e_cx = jnp.exp2(s_cx - m)
        l = jnp.sum(e_nb, axis=1, keepdims=True) + jnp.sum(e_cx, axis=1, keepdims=True)
        return e_nb.astype(BF16), e_cx.astype(BF16), l

    def values(r, start, off, p, e_nb, e_cx, l):
        sl = slice(p * LANES, (p + 1) * LANES)
        o2 = (jnp.dot(e_nb, v_ref[0, pl.ds(start, nk), sl], preferred_element_type=F32)
              + jnp.dot(e_cx, vc[:, sl], preferred_element_type=F32)) / l
        return jnp.where(first, o2[:GRID_W], o2[GRID_W:])

    s, pr, o = {}, {}, {}
    n_u = len(units)
    for step in range(n_u + 2):
        if step < n_u:
            s[step] = scores(*units[step])
        if 0 <= step - 1 < n_u:
            pr[step - 1] = softmax(*s.pop(step - 1))
        if 0 <= step - 2 < n_u:
            o[step - 2] = values(*units[step - 2], *pr.pop(step - 2))
    n_p = N_HEADS_NA // 2
    oa_rows = []
    for r in range(ROWS_PER_STEP):
        out = jnp.concatenate([o[r * n_p + p] for p in range(n_p)], axis=1)
        oa_rows.append(_rms(out, ona_ref[...]).astype(BF16))

    mod = mod_ref[0]
    w_hi, w_lo = _bf16_terms(wr_ref[...], 2)
    w2 = jnp.concatenate([w_hi, w_lo], axis=0)
    rps = sub // GRID_W
    tiles = [slice(t0, t0 + sub) for t0 in range(0, ROWS_PER_STEP * GRID_W, sub)]
    mixes = [jnp.dot(jnp.concatenate(oa_rows[i * rps:(i + 1) * rps], axis=0), w_ref[:D_NA],
                     preferred_element_type=F32)
             + jnp.dot(ob_ref[0, rs], w_ref[D_NA:], preferred_element_type=F32) for i, rs in enumerate(tiles)]
    for rs, mix in zip(tiles, mixes):
        xn = x_ref[0, rs] + mod[2:3] * mix
        xn_ref[0, rs] = xn
        h2 = _rms_mod(xn, n2_ref[...], mod[3:4], mod[4:5])
        h_hi, h_lo = _bf16_terms(h2, 2)
        h2_ref[0, rs] = h_hi
        l_hi = lax.dot_general(w2, h_hi, nt, preferred_element_type=F32)
        l_lo = lax.dot_general(w_hi, h_lo, nt, preferred_element_type=F32)
        logits = l_hi[:N_EXPERTS] + l_hi[N_EXPERTS:] + l_lo
        e = jnp.exp(logits - jnp.max(logits, axis=0, keepdims=True))
        aff_ref[0, :, rs] = e / jnp.sum(e, axis=0, keepdims=True)


def _bias_blocks(rpb):
    n_ro, n_co = 2 * NA_KH - 1, 2 * NA_KW - 1
    qc = np.arange(GRID_W)[:, None]
    kc = np.arange(GRID_W)[None, :]
    cs = np.clip(qc - NA_KW // 2, 0, GRID_W - NA_KW)
    col_ok = (kc >= cs) & (kc < cs + NA_KW)
    spread = ((kc - qc + NA_KW - 1)[None] == np.arange(n_co)[:, None, None]) & col_ok[None]
    blocks = jnp.dot(rpb.reshape(-1, n_co), jnp.asarray(spread.reshape(n_co, -1), F32), precision=HIGHEST)
    blocks = jnp.where(jnp.asarray(col_ok.reshape(-1)), blocks * LOG2E, NEG_INF)
    blocks = blocks.reshape(N_HEADS_NA, n_ro, GRID_W, GRID_W)
    return jnp.concatenate([blocks[:, :-1], blocks[:, 1:]], axis=-1)


def _attention_out(q, k, v, kc, vc, pb, out_norm_a, ob, x, mod, w_out_b, norm2, w_router_t, sub=512):
    b, n, d = x.shape
    rows = n // GRID_W
    tq = ROWS_PER_STEP * GRID_W
    per_b = lambda i, t: (i, 0, 0)
    tile = lambda i, t: (i, t, 0)
    full = lambda i, t: (0, 0)
    return pl.pallas_call(
        functools.partial(_attn_out_kernel, rows=rows, sub=sub),
        grid=(b, n // tq),
        in_specs=[pl.BlockSpec((1, tq, D_NA), tile),
                  pl.BlockSpec((1, n, D_NA), per_b),
                  pl.BlockSpec((1, n, D_NA), per_b),
                  pl.BlockSpec((1, CTX_LEN, D_NA), per_b),
                  pl.BlockSpec((1, CTX_LEN, D_NA), per_b),
                  pl.BlockSpec(pb.shape, lambda i, t: (0, 0, 0, 0)),
                  pl.BlockSpec((1, D_NA), full),
                  pl.BlockSpec((1, tq, D_SG), tile),
                  pl.BlockSpec((1, tq, d), tile),
                  pl.BlockSpec((1, N_MOD, d), per_b),
                  pl.BlockSpec(w_out_b.shape, full),
                  pl.BlockSpec((1, d), full),
                  pl.BlockSpec((N_EXPERTS, d), full)],
        out_specs=[pl.BlockSpec((1, tq, d), tile),
                   pl.BlockSpec((1, tq, d), tile),
                   pl.BlockSpec((1, N_EXPERTS, tq), lambda i, t: (i, 0, t))],
        out_shape=[jax.ShapeDtypeStruct((b, n, d), F32),
                   jax.ShapeDtypeStruct((b, n, d), BF16),
                   jax.ShapeDtypeStruct((b, N_EXPERTS, n), F32)],
        compiler_params=_params("parallel", "arbitrary"),
        name="attention_out_router",
    )(q, k, v, kc, vc, pb, out_norm_a, ob, x, mod, w_out_b, norm2, w_router_t)


def _prefix_count(mask_f, tri):
    rows, n = mask_f.shape
    parts = []
    carry = jnp.zeros((rows, 1), F32)
    for j in range(n // LANES):
        blk = mask_f[:, j * LANES:(j + 1) * LANES]
        parts.append(jnp.dot(blk.astype(BF16), tri, preferred_element_type=F32) + carry)
        carry = carry + jnp.sum(blk, axis=1, keepdims=True)
    return jnp.concatenate(parts, axis=1)


def _topk_kernel(aff_ref, slot_ref, eb_ref, *, cap):
    a = aff_ref[...]
    rows = a.shape[0]

    def enough(t):
        return jnp.sum(jnp.where(a >= t, 1.0, 0.0), axis=1, keepdims=True) >= cap

    tiny = jnp.full((rows, 1), float(np.finfo(np.float32).tiny), F32)
    normal = enough(tiny)
    pw = tiny
    hi = jnp.full((rows, 1), 4.0, F32)
    for bit in range(6, -1, -1):
        cand = pw * (2.0 ** (1 << bit))
        ok = enough(cand)
        pw = jnp.where(ok, cand, pw)
        hi = jnp.where(ok, hi, cand)
    lo = jnp.where(normal, pw, 0.0)
    hi = jnp.where(normal, hi, tiny)
    step = lo
    for _ in range(MANTISSA_STEPS):
        step = step * 0.5
        cand = lo + step
        ok = enough(cand)
        lo = jnp.where(ok, cand, lo)
        hi = jnp.where(ok, hi, cand)
    above = a >= hi
    tie = jnp.logical_and(a >= lo, jnp.logical_not(above))
    n_above = jnp.sum(jnp.where(above, 1.0, 0.0), axis=1, keepdims=True)
    ri = lax.broadcasted_iota(jnp.int32, (LANES, LANES), 0)
    ci = lax.broadcasted_iota(jnp.int32, (LANES, LANES), 1)
    tri = jnp.where(ri <= ci, 1.0, 0.0).astype(BF16)
    tie_rank = _prefix_count(jnp.where(tie, 1.0, 0.0), tri)
    sel = jnp.logical_or(above, jnp.logical_and(tie, tie_rank <= cap - n_above))
    sel_f = jnp.where(sel, 1.0, 0.0)
    pos = _prefix_count(sel_f, tri) - 1.0
    slot_ref[...] = jnp.where(sel, pos, -1.0).astype(jnp.int32)
    lane = lax.broadcasted_iota(jnp.int32, (rows, LANES), 1)
    cnt = jnp.zeros((rows, LANES), F32)
    for j in range(a.shape[1] // TCH):
        cnt = jnp.where(lane == j, jnp.sum(sel_f[:, j * TCH:(j + 1) * TCH], axis=1, keepdims=True), cnt)
    before = jnp.where(ri < ci, 1.0, 0.0).astype(BF16)
    eb_ref[...] = jnp.dot(cnt.astype(BF16), before, preferred_element_type=F32).astype(jnp.int32)


def _topk_slots(aff2, cap):
    rows, n = aff2.shape
    tr = rows
    return pl.pallas_call(
        functools.partial(_topk_kernel, cap=cap),
        grid=(rows // tr,),
        in_specs=[pl.BlockSpec((tr, n), lambda i: (i, 0))],
        out_specs=[pl.BlockSpec((tr, n), lambda i: (i, 0)), pl.BlockSpec((tr, LANES), lambda i: (i, 0))],
        out_shape=[jax.ShapeDtypeStruct((rows, n), jnp.int32), jax.ShapeDtypeStruct((rows, LANES), jnp.int32)],
        compiler_params=_params("parallel"),
        name="expert_topk",
    )(aff2)


def _slot_window(eb_ref, bi, e, j, n_e, n_chunks, cap):
    base = (bi * n_e + e) * (n_chunks + 1) + j
    s0 = eb_ref[base]
    s1 = eb_ref[base + 1]
    start = jnp.minimum(s0 & -SLOT_ALIGN, cap - SLOT_W)
    return s0, s1, pl.multiple_of(start, SLOT_ALIGN)


def _window_overflow(eb_ref, bi, j, n_e, n_chunks, cap):
    over = None
    for e in range(n_e):
        _, s1, start = _slot_window(eb_ref, bi, e, j, n_e, n_chunks, cap)
        o = s1 > start + SLOT_W
        over = o if over is None else jnp.logical_or(over, o)
    return over


def _gather_kernel(eb_ref, h2_ref, slot_ref, aff_ref, xs_ref, gate_ref, *, cap, jc, n_chunks):
    bi = pl.program_id(0)
    jo = pl.program_id(1)
    n_e = slot_ref.shape[1]
    wi = lax.broadcasted_iota(jnp.int32, (SLOT_W, TCH), 0)
    wcol = lax.broadcasted_iota(jnp.int32, (SLOT_W, 1), 0)

    @pl.when(jo == 0)
    def _():
        xs_ref[...] = jnp.zeros_like(xs_ref)
        gate_ref[...] = jnp.zeros_like(gate_ref)

    for jj in range(jc):
        j = jo * jc + jj
        tok = slice(jj * TCH, (jj + 1) * TCH)
        h2c = h2_ref[0, tok, :]
        for g in range(n_e // EXPERT_GROUP):
            blocks, meta = [], []
            for q in range(EXPERT_GROUP):
                e = EXPERT_GROUP * g + q
                s0, s1, start = _slot_window(eb_ref, bi, e, j, n_e, n_chunks, cap)
                hit = slot_ref[0, e:e + 1, tok] - start == wi
                blocks.append(jnp.where(hit, 1.0, 0.0).astype(BF16))
                gate = jnp.sum(jnp.where(hit, aff_ref[0, e:e + 1, tok], 0.0), axis=1, keepdims=True)
                meta.append((e, s0, s1, start, gate))
            onehot = jnp.concatenate(blocks, axis=0)
            rows = jnp.dot(onehot, h2c, preferred_element_type=F32).astype(BF16)
            for q, (e, s0, s1, start, gate) in enumerate(meta):
                own = jnp.logical_and(wcol + start >= s0, wcol + start < s1)
                win = pl.ds(start, SLOT_W)
                xs_ref[0, e, win, :] = jnp.where(own, rows[q * SLOT_W:(q + 1) * SLOT_W], xs_ref[0, e, win, :])
                gate_ref[0, e, win, :] = jnp.where(own, gate, gate_ref[0, e, win, :])

    for jj in range(jc):
        j = jo * jc + jj
        tok = slice(jj * TCH, (jj + 1) * TCH)

        @pl.when(_window_overflow(eb_ref, bi, j, n_e, n_chunks, cap))
        def _():
            h2c = h2_ref[0, tok, :]
            ci = lax.broadcasted_iota(jnp.int32, (cap, TCH), 0)
            ccol = lax.broadcasted_iota(jnp.int32, (cap, 1), 0)
            for e in range(n_e):
                s0, s1, _ = _slot_window(eb_ref, bi, e, j, n_e, n_chunks, cap)
                hit = slot_ref[0, e:e + 1, tok] == ci
                rows = jnp.dot(jnp.where(hit, 1.0, 0.0).astype(BF16), h2c, preferred_element_type=F32).astype(BF16)
                gate = jnp.sum(jnp.where(hit, aff_ref[0, e:e + 1, tok], 0.0), axis=1, keepdims=True)
                own = jnp.logical_and(ccol >= s0, ccol < s1)
                xs_ref[0, e] = jnp.where(own, rows, xs_ref[0, e])
                gate_ref[0, e] = jnp.where(own, gate, gate_ref[0, e])


def _gather_tokens(ebound, h2, slot, aff, cap, jc=4):
    b, n, d = h2.shape
    e = slot.shape[1]
    n_chunks = n // TCH
    assert n_chunks % jc == 0
    tok = lambda i, j, *_: (i, j, 0)
    per_b = lambda i, j, *_: (i, 0, 0, 0)
    return pl.pallas_call(
        functools.partial(_gather_kernel, cap=cap, jc=jc, n_chunks=n_chunks),
        grid_spec=pltpu.PrefetchScalarGridSpec(
            num_scalar_prefetch=1,
            grid=(b, n_chunks // jc),
            in_specs=[pl.BlockSpec((1, jc * TCH, d), tok),
                      pl.BlockSpec((1, e, jc * TCH), lambda i, j, *_: (i, 0, j)),
                      pl.BlockSpec((1, e, jc * TCH), lambda i, j, *_: (i, 0, j))],
            out_specs=[pl.BlockSpec((1, e, cap, d), per_b),
                       pl.BlockSpec((1, e, cap, 1), per_b)]),
        out_shape=[jax.ShapeDtypeStruct((b, e, cap, d), BF16),
                   jax.ShapeDtypeStruct((b, e, cap, 1), F32)],
        compiler_params=_params("parallel", "arbitrary"),
        name="moe_gather",
    )(ebound, h2, slot, aff)


def _expert_kernel(xs_ref, wg_ref, wu_ref, wd_ref, gate_ref, y_ref, acc_ref, *, rb, n_f):
    f = pl.program_id(1)
    nb, _, cap, d = xs_ref.shape

    def step(first, last):
        wg = wg_ref[0].astype(BF16)
        wu = wu_ref[0].astype(BF16)
        wd = wd_ref[0].astype(BF16)
        for i in range(nb // rb):
            blk = slice(i * rb, (i + 1) * rb)
            xs = xs_ref[blk, 0].reshape(rb * cap, d)
            a = jnp.dot(xs, wg, preferred_element_type=F32)
            u = jnp.dot(xs, wu, preferred_element_type=F32)
            hm = (_silu(a) * u).astype(BF16)
            part = jnp.dot(hm, wd, preferred_element_type=F32).reshape(rb, cap, d)
            total = part if first else acc_ref[blk] + part
            if last:
                y_ref[blk, 0] = (total * gate_ref[blk, 0]).astype(BF16)
            else:
                acc_ref[blk] = total

    if n_f == 1:
        step(True, True)
    else:
        pl.when(f == 0)(lambda: step(True, False))
        if n_f > 2:
            pl.when(jnp.logical_and(f > 0, f < n_f - 1))(lambda: step(False, False))
        pl.when(f == n_f - 1)(lambda: step(False, True))


def _experts(xs, w_gate, w_up, w_down, gate, fc=768, rb=4):
    b, e, cap, d = xs.shape
    dff = w_gate.shape[2]
    per_e = lambda i, f: (0, i, 0, 0)
    return pl.pallas_call(
        functools.partial(_expert_kernel, rb=rb, n_f=dff // fc),
        grid=(e, dff // fc),
        in_specs=[pl.BlockSpec((b, 1, cap, d), per_e),
                  pl.BlockSpec((1, d, fc), lambda i, f: (i, 0, f)),
                  pl.BlockSpec((1, d, fc), lambda i, f: (i, 0, f)),
                  pl.BlockSpec((1, fc, d), lambda i, f: (i, f, 0)),
                  pl.BlockSpec((b, 1, cap, 1), per_e)],
        out_specs=pl.BlockSpec((b, 1, cap, d), per_e),
        out_shape=jax.ShapeDtypeStruct((b, e, cap, d), BF16),
        scratch_shapes=[pltpu.VMEM((b, cap, d), F32)],
        compiler_params=_params("parallel", "arbitrary"),
        name="moe_experts",
    )(xs, w_gate, w_up, w_down, gate)


def _combine_kernel(eb_ref, slot_t_ref, y_hbm, xn_ref, mod_ref, nf_ref, o_ref, ybuf, ysem, *, cap, jc, n_chunks):
    bi = pl.program_id(0)
    jo = pl.program_id(1)
    n_e = y_hbm.shape[1]
    cur = bi & 1

    def y_copy(sample, slot):
        return pltpu.make_async_copy(y_hbm.at[sample], ybuf.at[slot], ysem.at[slot])

    @pl.when(jnp.logical_and(bi == 0, jo == 0))
    def _():
        y_copy(0, 0).start()

    @pl.when(jo == 0)
    def _():
        y_copy(bi, cur).wait()

        @pl.when(bi + 1 < pl.num_programs(0))
        def _():
            y_copy(bi + 1, 1 - cur).start()

    y_ref = ybuf.at[cur]
    ci = lax.broadcasted_iota(jnp.int32, (TCH, EXPERT_GROUP * SLOT_W), 1)
    toks = [slice(jj * TCH, (jj + 1) * TCH) for jj in range(jc)]
    lane_e = lax.broadcasted_iota(jnp.int32, (1, n_e), 1)
    group_off = (lane_e & (EXPERT_GROUP - 1)) * SLOT_W

    def finish(tok, moe):
        x = xn_ref[0, tok] + mod_ref[0][5:6] * moe
        o_ref[0, tok] = _rms(x, nf_ref[...])

    moes = []
    for jj, tok in enumerate(toks):
        j = jo * jc + jj
        st = slot_t_ref[0, tok, :]
        starts = [_slot_window(eb_ref, bi, e, j, n_e, n_chunks, cap)[2] for e in range(n_e)]
        start_vec = jnp.zeros((1, n_e), jnp.int32)
        for e in range(n_e):
            start_vec = jnp.where(lane_e == e, starts[e], start_vec)
        rel = st - start_vec
        tgt = jnp.where(jnp.logical_and(rel >= 0, rel < SLOT_W), rel + group_off, -1)
        s_blocks, y_blocks = [], []
        for g in range(n_e // EXPERT_GROUP):
            col = None
            for q in reversed(range(EXPERT_GROUP)):
                e = EXPERT_GROUP * g + q
                col = tgt[:, e:e + 1] if col is None else jnp.where(ci < (q + 1) * SLOT_W, tgt[:, e:e + 1], col)
                y_blocks.insert(g * EXPERT_GROUP, y_ref[e, pl.ds(starts[e], SLOT_W), :])
            s_blocks.append(jnp.where(col == ci, 1.0, 0.0).astype(BF16))
        scat = jnp.concatenate(s_blocks, axis=1)
        ywin = jnp.concatenate(y_blocks, axis=0)
        moes.append(jnp.dot(scat, ywin, preferred_element_type=F32))
    for tok, moe in zip(toks, moes):
        finish(tok, moe)

    for jj, tok in enumerate(toks):
        @pl.when(_window_overflow(eb_ref, bi, jo * jc + jj, n_e, n_chunks, cap))
        def _():
            st = slot_t_ref[0, tok, :]
            cf = lax.broadcasted_iota(jnp.int32, (TCH, cap), 1)
            dense = jnp.concatenate(
                [jnp.where(st[:, e:e + 1] == cf, 1.0, 0.0).astype(BF16) for e in range(n_e)], axis=1)
            finish(tok, jnp.dot(dense, y_ref[...].reshape(n_e * cap, y_hbm.shape[3]),
                                preferred_element_type=F32))


def _combine(ebound, slot_t, y, x_new, mod, norm_final, cap, jc=4):
    b, n, d = x_new.shape
    e = slot_t.shape[2]
    n_chunks = n // TCH
    assert n_chunks % jc == 0
    tile = lambda i, j, *_: (i, j, 0)
    return pl.pallas_call(
        functools.partial(_combine_kernel, cap=cap, jc=jc, n_chunks=n_chunks),
        grid_spec=pltpu.PrefetchScalarGridSpec(
            num_scalar_prefetch=1,
            grid=(b, n_chunks // jc),
            in_specs=[pl.BlockSpec((1, jc * TCH, e), tile),
                      pl.BlockSpec(memory_space=pl.ANY),
                      pl.BlockSpec((1, jc * TCH, d), tile),
                      pl.BlockSpec((1, N_MOD, d), lambda i, j, *_: (i, 0, 0)),
                      pl.BlockSpec((1, d), lambda i, j, *_: (0, 0))],
            out_specs=pl.BlockSpec((1, jc * TCH, d), tile),
            scratch_shapes=[pltpu.VMEM((2, e, cap, d), BF16), pltpu.SemaphoreType.DMA((2,))]),
        out_shape=jax.ShapeDtypeStruct((b, n, d), F32),
        compiler_params=_params("arbitrary", "arbitrary"),
        name="moe_combine_norm",
    )(ebound, slot_t, y, x_new, mod, norm_final)


def kernel(x, c, ctx, c_ctx, w_mod, b_mod, norm1, w_in, rpb, w_s, b_s, gmlp_norm, out_norm_a, out_norm_b,
           w_out, norm2, w_router, w_gate, w_up, w_down, norm_final):
    b, n, d = x.shape
    assert w_mod.shape[0] == 1, "single-layer stack only"
    assert n % (GRID_W * ROWS_PER_STEP) == 0 and n // GRID_W >= NA_KH
    assert n % TCH == 0 and N_EXPERTS % EXPERT_GROUP == 0 and EXPERT_GROUP & (EXPERT_GROUP - 1) == 0
    cap = EC_CAPACITY_FACTOR * n // N_EXPERTS

    pad = (-(b + 1)) % 8
    cc = jnp.concatenate([c, c_ctx[None], jnp.zeros((pad, d), F32)], axis=0)
    m = _modulation(cc, w_mod[0], b_mod[0][None])
    mod = m.reshape(-1, N_MOD, d)

    w_in_b = w_in[0].astype(BF16)
    ws2 = w_s[0].astype(BF16).reshape(N_GROUPS_SG // 2, 2 * CHUNK, CHUNK)
    bs2 = jnp.broadcast_to(b_s[0].reshape(N_GROUPS_SG // 2, 2 * CHUNK, 1), (N_GROUPS_SG // 2, 2 * CHUNK, LANES))
    q, k, v, ob = _in_proj(x, mod, norm1, w_in_b, ws2, bs2, gmlp_norm, out_norm_b)

    kc, vc = _ctx_proj(ctx.reshape(b * CTX_LEN, d), mod, b, norm1, w_in_b)
    kc = kc.reshape(b, CTX_LEN, D_NA)
    vc = vc.reshape(b, CTX_LEN, D_NA)

    x_new, h2, aff = _attention_out(q, k, v, kc, vc, _bias_blocks(rpb[0]), out_norm_a,
                                    ob, x, mod, w_out[0].astype(BF16), norm2, w_router[0].T)

    slot, ebound = _topk_slots(aff.reshape(b * N_EXPERTS, n), cap)
    slot = slot.reshape(b, N_EXPERTS, n)
    ebound = ebound[:, :n // TCH + 1].reshape(-1)

    xs, gate = _gather_tokens(ebound, h2, slot, aff, cap)
    y = _experts(xs, w_gate[0], w_up[0], w_down[0], gate)
    return _combine(ebound, jnp.swapaxes(slot, 1, 2), y, x_new, mod, norm_final[None], cap)
```

```python
import functools

import numpy as np
import jax
import jax.numpy as jnp
from jax import lax
from jax.experimental import pallas as pl
from jax.experimental.pallas import tpu as pltpu

D_MODEL = 1024
GRID_W = 64
CTX_LEN = 256
N_HEADS_NA = 8
HEAD_DIM = 64
D_NA = N_HEADS_NA * HEAD_DIM
NA_KH = 8
NA_KW = 16
D_SG = D_MODEL - D_NA
N_GROUPS_SG = 8
SG_GROUP_DIM = D_SG // N_GROUPS_SG
CHUNK = 128
N_EXPERTS = 16
EC_CAPACITY_FACTOR = 2
D_EXPERT = 1536
N_MOD = 6
EPS = 1e-6
NEG_INF = -1e30
LOG2E = float(np.log2(np.e))

LANES = 128
VMEM_LIMIT = 56 * 1024 * 1024

F32 = jnp.float32
BF16 = jnp.bfloat16
HIGHEST = lax.Precision.HIGHEST

ROWS_PER_STEP = 16

MANTISSA_STEPS = 36

TCH = 256
SLOT_W = 64
SLOT_ALIGN = 16
EXPERT_GROUP = 4


def _params(*sem):
    return pltpu.CompilerParams(dimension_semantics=sem, vmem_limit_bytes=VMEM_LIMIT)


def _rms_mod(x, g, shift, scale):
    r = lax.rsqrt(jnp.mean(x * x, axis=-1, keepdims=True) + EPS)
    return (x * r) * g * (1.0 + scale) + shift


def _rms(x, g):
    return x * lax.rsqrt(jnp.mean(x * x, axis=-1, keepdims=True) + EPS) * g


def _gelu_tanh(x):
    return 0.5 * x * (1.0 + jnp.tanh(np.sqrt(2.0 / np.pi).astype(np.float32) * (x + 0.044715 * (x * x * x))))


def _silu(x):
    return x * jax.nn.sigmoid(x)


def _bf16_terms(x, n_terms):
    terms = []
    for _ in range(n_terms):
        t = x.astype(BF16)
        terms.append(t)
        x = x - t.astype(F32)
    return terms


def _mod_kernel(c_ref, w_ref, b_ref, o_ref):
    rows = c_ref.shape[0]
    s = jnp.concatenate(_bf16_terms(_silu(c_ref[...]), 3), axis=0)
    w_hi, w_lo = _bf16_terms(w_ref[...], 2)
    hi = jnp.dot(s, w_hi, preferred_element_type=F32)
    lo = jnp.dot(s[:2 * rows], w_lo, preferred_element_type=F32)
    small = (hi[2 * rows:] + lo[rows:]) + (hi[rows:2 * rows] + lo[:rows])
    o_ref[...] = (small + hi[:rows]) + b_ref[...]


def _modulation(cc, w_mod, b_mod):
    rows, d = cc.shape
    n = w_mod.shape[1]
    tn = 1024
    return pl.pallas_call(
        _mod_kernel,
        grid=(n // tn,),
        in_specs=[pl.BlockSpec((rows, d), lambda j: (0, 0)),
                  pl.BlockSpec((d, tn), lambda j: (0, j)),
                  pl.BlockSpec((1, tn), lambda j: (0, j))],
        out_specs=pl.BlockSpec((rows, tn), lambda j: (0, j)),
        out_shape=jax.ShapeDtypeStruct((rows, n), F32),
        compiler_params=_params("arbitrary"),
        name="modulation",
    )(cc, w_mod, b_mod)


def _in_kernel(x_ref, mod_ref, n1_ref, w_ref, ws_ref, bs_ref, gn_ref, onb_ref,
               q_ref, k_ref, v_ref, ob_ref, sp_ref, *, sub):
    mod = mod_ref[0]
    tiles = [slice(r, r + sub) for r in range(0, x_ref.shape[1], sub)]
    nch = sub // CHUNK
    lane = lax.broadcasted_iota(jnp.int32, (CHUNK, LANES), 1)
    first = lane < SG_GROUP_DIM

    def project(rs):
        hb = _rms_mod(x_ref[0, rs], n1_ref[...], mod[0:1], mod[1:2]).astype(BF16)
        qkv = jnp.dot(hb, w_ref[:, :3 * D_NA], preferred_element_type=F32)
        q_ref[0, rs] = (qkv[:, :D_NA] * (HEAD_DIM ** -0.5 * LOG2E)).astype(BF16)
        k_ref[0, rs] = qkv[:, D_NA:2 * D_NA].astype(BF16)
        v_ref[0, rs] = qkv[:, 2 * D_NA:].astype(BF16)
        return jnp.dot(hb, w_ref[:, 3 * D_NA:], preferred_element_type=F32)

    def gate_mlp(rs, uz):
        u = _gelu_tanh(uz[:, :D_SG])
        z = _gelu_tanh(uz[:, D_SG:])
        mu = jnp.mean(z, axis=-1, keepdims=True)
        zc = z - mu
        var = jnp.mean(zc * zc, axis=-1, keepdims=True)
        zb = (zc * lax.rsqrt(var + EPS) * gn_ref[...]).astype(BF16)
        for p in range(N_GROUPS_SG // 2):
            zp = jnp.concatenate(
                [zb[c * CHUNK:(c + 1) * CHUNK, p * LANES:(p + 1) * LANES] for c in range(nch)], axis=1)
            r = jnp.dot(ws_ref[p], zp, preferred_element_type=F32)
            bs = bs_ref[p]
            for c in range(nch):
                top = r[:CHUNK, c * LANES:(c + 1) * LANES] + bs[:CHUNK]
                bot = r[CHUNK:, c * LANES:(c + 1) * LANES] + bs[CHUNK:]
                sp_ref[rs.start + c * CHUNK:rs.start + (c + 1) * CHUNK, p * LANES:(p + 1) * LANES] = (
                    jnp.where(first, top, bot))
        ob = u * sp_ref[rs, :]
        ob_ref[0, rs] = _rms(ob, onb_ref[...]).astype(BF16)

    uz = {0: project(tiles[0])}
    for i, rs in enumerate(tiles):
        if i + 1 < len(tiles):
            uz[i + 1] = project(tiles[i + 1])
        gate_mlp(rs, uz.pop(i))


def _in_proj(x, mod, norm1, w_in_b, ws2, bs2, gmlp_norm, out_norm_b, tm=2048, sub=512):
    b, n, d = x.shape
    d_in = w_in_b.shape[1]
    full2 = lambda i, j: (0, 0)
    full3 = lambda i, j: (0, 0, 0)
    tile = lambda i, j: (i, j, 0)
    act = jax.ShapeDtypeStruct((b, n, D_NA), BF16)
    return pl.pallas_call(
        functools.partial(_in_kernel, sub=sub),
        grid=(b, n // tm),
        in_specs=[pl.BlockSpec((1, tm, d), tile),
                  pl.BlockSpec((1, N_MOD, d), lambda i, j: (i, 0, 0)),
                  pl.BlockSpec((1, d), full2),
                  pl.BlockSpec((d, d_in), full2),
                  pl.BlockSpec(ws2.shape, full3),
                  pl.BlockSpec(bs2.shape, full3),
                  pl.BlockSpec((1, D_SG), full2),
                  pl.BlockSpec((1, D_SG), full2)],
        out_specs=[pl.BlockSpec((1, tm, D_NA), tile)] * 4,
        out_shape=[act] * 4,
        scratch_shapes=[pltpu.VMEM((tm, D_SG), F32)],
        compiler_params=_params("parallel", "arbitrary"),
        name="in_proj_gmlp",
    )(x, mod, norm1, w_in_b, ws2, bs2, gmlp_norm, out_norm_b)


def _ctx_kernel(x_ref, mod_ref, n1_ref, wk_ref, wv_ref, k_ref, v_ref):
    mod = mod_ref[0]
    hb = _rms_mod(x_ref[...], n1_ref[...], mod[0:1], mod[1:2]).astype(BF16)
    k_ref[...] = jnp.dot(hb, wk_ref[...], preferred_element_type=F32).astype(BF16)
    v_ref[...] = jnp.dot(hb, wv_ref[...], preferred_element_type=F32).astype(BF16)


def _ctx_proj(ctx2, mod, mod_row, norm1, w_in_b, tm=512):
    rows, d = ctx2.shape
    act = jax.ShapeDtypeStruct((rows, D_NA), BF16)
    return pl.pallas_call(
        _ctx_kernel,
        grid=(rows // tm,),
        in_specs=[pl.BlockSpec((tm, d), lambda i: (i, 0)),
                  pl.BlockSpec((1, N_MOD, d), lambda i: (mod_row, 0, 0)),
                  pl.BlockSpec((1, d), lambda i: (0, 0)),
                  pl.BlockSpec((d, D_NA), lambda i: (0, 1)),
                  pl.BlockSpec((d, D_NA), lambda i: (0, 2))],
        out_specs=[pl.BlockSpec((tm, D_NA), lambda i: (i, 0))] * 2,
        out_shape=[act, act],
        compiler_params=_params("arbitrary"),
        name="ctx_kv_proj",
    )(ctx2, mod, norm1, w_in_b, w_in_b)


def _attn_out_kernel(q_ref, k_ref, v_ref, kc_ref, vc_ref, pb_ref, ona_ref,
                     ob_ref, x_ref, mod_ref, w_ref, n2_ref, wr_ref, xn_ref, h2_ref, aff_ref, *, rows, sub):
    kc = kc_ref[0]
    vc = vc_ref[0]
    nk = NA_KH * GRID_W
    lane = lax.broadcasted_iota(jnp.int32, (GRID_W, LANES), 1)
    first = lane < HEAD_DIM
    nt = (((1,), (1,)), ((), ()))
    units = []
    for r in range(ROWS_PER_STEP):
        row = pl.program_id(1) * ROWS_PER_STEP + r
        rs = jnp.clip(row - NA_KH // 2, 0, rows - NA_KH)
        start = pl.multiple_of(rs * GRID_W, GRID_W)
        off = rs - row + NA_KH - 1
        units += [(r, start, off, p) for p in range(N_HEADS_NA // 2)]

    def scores(r, start, off, p):
        sl = slice(p * LANES, (p + 1) * LANES)
        qp = q_ref[0, r * GRID_W:(r + 1) * GRID_W, sl]
        q2 = jnp.concatenate([jnp.where(first, qp, jnp.zeros_like(qp)),
                              jnp.where(first, jnp.zeros_like(qp), qp)], axis=0)
        bias = jnp.concatenate(
            [jnp.concatenate([pb_ref[2 * p + j, off + 2 * c] for c in range(NA_KH // 2)], axis=1)
             for j in range(2)], axis=0)
        s_nb = lax.dot_general(q2, k_ref[0, pl.ds(start, nk), sl], nt, preferred_element_type=F32) + bias
        s_cx = lax.dot_general(q2, kc[:, sl], nt, preferred_element_type=F32)
        return s_nb, s_cx

    def softmax(s_nb, s_cx):
        m = jnp.maximum(jnp.max(s_nb, axis=1, keepdims=True), jnp.max(s_cx, axis=1, keepdims=True))
        e_nb = jnp.exp2(s_nb - m)
        e_cx = jnp.exp2(s_cx - m)
        l = jnp.sum(e_nb, axis=1, keepdims=True) + jnp.sum(e_cx, axis=1, keepdims=True)
        return e_nb.astype(BF16), e_cx.astype(BF16), l

    def values(r, start, off, p, e_nb, e_cx, l):
        sl = slice(p * LANES, (p + 1) * LANES)
        o2 = (jnp.dot(e_nb, v_ref[0, pl.ds(start, nk), sl], preferred_element_type=F32)
              + jnp.dot(e_cx, vc[:, sl], preferred_element_type=F32)) / l
        return jnp.where(first, o2[:GRID_W], o2[GRID_W:])

    s, pr, o = {}, {}, {}
    n_u = len(units)
    for step in range(n_u + 2):
        if step < n_u:
            s[step] = scores(*units[step])
        if 0 <= step - 1 < n_u:
            pr[step - 1] = softmax(*s.pop(step - 1))
        if 0 <= step - 2 < n_u:
            o[step - 2] = values(*units[step - 2], *pr.pop(step - 2))
    n_p = N_HEADS_NA // 2
    oa_rows = []
    for r in range(ROWS_PER_STEP):
        out = jnp.concatenate([o[r * n_p + p] for p in range(n_p)], axis=1)
        oa_rows.append(_rms(out, ona_ref[...]).astype(BF16))

    mod = mod_ref[0]
    w_hi, w_lo = _bf16_terms(wr_ref[...], 2)
    w2 = jnp.concatenate([w_hi, w_lo], axis=0)
    rps = sub // GRID_W
    tiles = [slice(t0, t0 + sub) for t0 in range(0, ROWS_PER_STEP * GRID_W, sub)]
    mixes = [jnp.dot(jnp.concatenate(oa_rows[i * rps:(i + 1) * rps], axis=0), w_ref[:D_NA],
                     preferred_element_type=F32)
             + jnp.dot(ob_ref[0, rs], w_ref[D_NA:], preferred_element_type=F32) for i, rs in enumerate(tiles)]
    for rs, mix in zip(tiles, mixes):
        xn = x_ref[0, rs] + mod[2:3] * mix
        xn_ref[0, rs] = xn
        h2 = _rms_mod(xn, n2_ref[...], mod[3:4], mod[4:5])
        h_hi, h_lo = _bf16_terms(h2, 2)
        h2_ref[0, rs] = h_hi
        l_hi = lax.dot_general(w2, h_hi, nt, preferred_element_type=F32)
        l_lo = lax.dot_general(w_hi, h_lo, nt, preferred_element_type=F32)
        logits = l_hi[:N_EXPERTS] + l_hi[N_EXPERTS:] + l_lo
        e = jnp.exp(logits - jnp.max(logits, axis=0, keepdims=True))
        aff_ref[0, :, rs] = e / jnp.sum(e, axis=0, keepdims=True)


def _bias_blocks(rpb):
    n_ro, n_co = 2 * NA_KH - 1, 2 * NA_KW - 1
    qc = np.arange(GRID_W)[:, None]
    kc = np.arange(GRID_W)[None, :]
    cs = np.clip(qc - NA_KW // 2, 0, GRID_W - NA_KW)
    col_ok = (kc >= cs) & (kc < cs + NA_KW)
    spread = ((kc - qc + NA_KW - 1)[None] == np.arange(n_co)[:, None, None]) & col_ok[None]
    blocks = jnp.dot(rpb.reshape(-1, n_co), jnp.asarray(spread.reshape(n_co, -1), F32), precision=HIGHEST)
    blocks = jnp.where(jnp.asarray(col_ok.reshape(-1)), blocks * LOG2E, NEG_INF)
    blocks = blocks.reshape(N_HEADS_NA, n_ro, GRID_W, GRID_W)
    return jnp.concatenate([blocks[:, :-1], blocks[:, 1:]], axis=-1)


def _attention_out(q, k, v, kc, vc, pb, out_norm_a, ob, x, mod, w_out_b, norm2, w_router_t, sub=512):
    b, n, d = x.shape
    rows = n // GRID_W
    tq = ROWS_PER_STEP * GRID_W
    per_b = lambda i, t: (i, 0, 0)
    tile = lambda i, t: (i, t, 0)
    full = lambda i, t: (0, 0)
    return pl.pallas_call(
        functools.partial(_attn_out_kernel, rows=rows, sub=sub),
        grid=(b, n // tq),
        in_specs=[pl.BlockSpec((1, tq, D_NA), tile),
                  pl.BlockSpec((1, n, D_NA), per_b),
                  pl.BlockSpec((1, n, D_NA), per_b),
                  pl.BlockSpec((1, CTX_LEN, D_NA), per_b),
                  pl.BlockSpec((1, CTX_LEN, D_NA), per_b),
                  pl.BlockSpec(pb.shape, lambda i, t: (0, 0, 0, 0)),
                  pl.BlockSpec((1, D_NA), full),
                  pl.BlockSpec((1, tq, D_SG), tile),
                  pl.BlockSpec((1, tq, d), tile),
                  pl.BlockSpec((1, N_MOD, d), per_b),
                  pl.BlockSpec(w_out_b.shape, full),
                  pl.BlockSpec((1, d), full),
                  pl.BlockSpec((N_EXPERTS, d), full)],
        out_specs=[pl.BlockSpec((1, tq, d), tile),
                   pl.BlockSpec((1, tq, d), tile),
                   pl.BlockSpec((1, N_EXPERTS, tq), lambda i, t: (i, 0, t))],
        out_shape=[jax.ShapeDtypeStruct((b, n, d), F32),
                   jax.ShapeDtypeStruct((b, n, d), BF16),
                   jax.ShapeDtypeStruct((b, N_EXPERTS, n), F32)],
        compiler_params=_params("parallel", "arbitrary"),
        name="attention_out_router",
    )(q, k, v, kc, vc, pb, out_norm_a, ob, x, mod, w_out_b, norm2, w_router_t)


def _prefix_count(mask_f, tri):
    rows, n = mask_f.shape
    parts = []
    carry = jnp.zeros((rows, 1), F32)
    for j in range(n // LANES):
        blk = mask_f[:, j * LANES:(j + 1) * LANES]
        parts.append(jnp.dot(blk.astype(BF16), tri, preferred_element_type=F32) + carry)
        carry = carry + jnp.sum(blk, axis=1, keepdims=True)
    return jnp.concatenate(parts, axis=1)


def _topk_kernel(aff_ref, slot_ref, eb_ref, *, cap):
    a = aff_ref[...]
    rows = a.shape[0]

    def enough(t):
        return jnp.sum(jnp.where(a >= t, 1.0, 0.0), axis=1, keepdims=True) >= cap

    tiny = jnp.full((rows, 1), float(np.finfo(np.float32).tiny), F32)
    normal = enough(tiny)
    pw = tiny
    hi = jnp.full((rows, 1), 4.0, F32)
    for bit in range(6, -1, -1):
        cand = pw * (2.0 ** (1 << bit))
        ok = enough(cand)
        pw = jnp.where(ok, cand, pw)
        hi = jnp.where(ok, hi, cand)
    lo = jnp.where(normal, pw, 0.0)
    hi = jnp.where(normal, hi, tiny)
    step = lo
    for _ in range(MANTISSA_STEPS):
        step = step * 0.5
        cand = lo + step
        ok = enough(cand)
        lo = jnp.where(ok, cand, lo)
        hi = jnp.where(ok, hi, cand)
    above = a >= hi
    tie = jnp.logical_and(a >= lo, jnp.logical_not(above))
    n_above = jnp.sum(jnp.where(above, 1.0, 0.0), axis=1, keepdims=True)
    ri = lax.broadcasted_iota(jnp.int32, (LANES, LANES), 0)
    ci = lax.broadcasted_iota(jnp.int32, (LANES, LANES), 1)
    tri = jnp.where(ri <= ci, 1.0, 0.0).astype(BF16)
    tie_rank = _prefix_count(jnp.where(tie, 1.0, 0.0), tri)
    sel = jnp.logical_or(above, jnp.logical_and(tie, tie_rank <= cap - n_above))
    sel_f = jnp.where(sel, 1.0, 0.0)
    pos = _prefix_count(sel_f, tri) - 1.0
    slot_ref[...] = jnp.where(sel, pos, -1.0).astype(jnp.int32)
    lane = lax.broadcasted_iota(jnp.int32, (rows, LANES), 1)
    cnt = jnp.zeros((rows, LANES), F32)
    for j in range(a.shape[1] // TCH):
        cnt = jnp.where(lane == j, jnp.sum(sel_f[:, j * TCH:(j + 1) * TCH], axis=1, keepdims=True), cnt)
    before = jnp.where(ri < ci, 1.0, 0.0).astype(BF16)
    eb_ref[...] = jnp.dot(cnt.astype(BF16), before, preferred_element_type=F32).astype(jnp.int32)


def _topk_slots(aff2, cap):
    rows, n = aff2.shape
    tr = rows
    return pl.pallas_call(
        functools.partial(_topk_kernel, cap=cap),
        grid=(rows // tr,),
        in_specs=[pl.BlockSpec((tr, n), lambda i: (i, 0))],
        out_specs=[pl.BlockSpec((tr, n), lambda i: (i, 0)), pl.BlockSpec((tr, LANES), lambda i: (i, 0))],
        out_shape=[jax.ShapeDtypeStruct((rows, n), jnp.int32), jax.ShapeDtypeStruct((rows, LANES), jnp.int32)],
        compiler_params=_params("parallel"),
        name="expert_topk",
    )(aff2)


def _slot_window(eb_ref, bi, e, j, n_e, n_chunks, cap):
    base = (bi * n_e + e) * (n_chunks + 1) + j
    s0 = eb_ref[base]
    s1 = eb_ref[base + 1]
    start = jnp.minimum(s0 & -SLOT_ALIGN, cap - SLOT_W)
    return s0, s1, pl.multiple_of(start, SLOT_ALIGN)


def _window_overflow(eb_ref, bi, j, n_e, n_chunks, cap):
    over = None
    for e in range(n_e):
        _, s1, start = _slot_window(eb_ref, bi, e, j, n_e, n_chunks, cap)
        o = s1 > start + SLOT_W
        over = o if over is None else jnp.logical_or(over, o)
    return over


def _gather_kernel(eb_ref, h2_ref, slot_ref, aff_ref, xs_ref, gate_ref, *, cap, jc, n_chunks):
    bi = pl.program_id(0)
    jo = pl.program_id(1)
    n_e = slot_ref.shape[1]
    wi = lax.broadcasted_iota(jnp.int32, (SLOT_W, TCH), 0)
    wcol = lax.broadcasted_iota(jnp.int32, (SLOT_W, 1), 0)

    @pl.when(jo == 0)
    def _():
        xs_ref[...] = jnp.zeros_like(xs_ref)

    for jj in range(jc):
        j = jo * jc + jj
        tok = slice(jj * TCH, (jj + 1) * TCH)
        h2c = h2_ref[0, tok, :]
        for g in range(n_e // EXPERT_GROUP):
            blocks, meta = [], []
            for q in range(EXPERT_GROUP):
                e = EXPERT_GROUP * g + q
                s0, s1, start = _slot_window(eb_ref, bi, e, j, n_e, n_chunks, cap)
                hit = slot_ref[0, e:e + 1, tok] - start == wi
                blocks.append(jnp.where(hit, 1.0, 0.0).astype(BF16))
                gate = jnp.sum(jnp.where(hit, aff_ref[0, e:e + 1, tok], 0.0), axis=1, keepdims=True)
                meta.append((e, s0, s1, start, gate))
            onehot = jnp.concatenate(blocks, axis=0)
            rows = jnp.dot(onehot, h2c, preferred_element_type=F32).astype(BF16)
            for q, (e, s0, s1, start, gate) in enumerate(meta):
                own = jnp.logical_and(wcol + start >= s0, wcol + start < s1)
                win = pl.ds(start, SLOT_W)
                xs_ref[0, e, win, :] = jnp.where(own, rows[q * SLOT_W:(q + 1) * SLOT_W], xs_ref[0, e, win, :])
                pltpu.store(gate_ref.at[0, e, win, :], gate, mask=own)

    for jj in range(jc):
        j = jo * jc + jj
        tok = slice(jj * TCH, (jj + 1) * TCH)

        @pl.when(_window_overflow(eb_ref, bi, j, n_e, n_chunks, cap))
        def _():
            h2c = h2_ref[0, tok, :]
            ci = lax.broadcasted_iota(jnp.int32, (cap, TCH), 0)
            ccol = lax.broadcasted_iota(jnp.int32, (cap, 1), 0)
            for e in range(n_e):
                s0, s1, _ = _slot_window(eb_ref, bi, e, j, n_e, n_chunks, cap)
                hit = slot_ref[0, e:e + 1, tok] == ci
                rows = jnp.dot(jnp.where(hit, 1.0, 0.0).astype(BF16), h2c, preferred_element_type=F32).astype(BF16)
                gate = jnp.sum(jnp.where(hit, aff_ref[0, e:e + 1, tok], 0.0), axis=1, keepdims=True)
                own = jnp.logical_and(ccol >= s0, ccol < s1)
                xs_ref[0, e] = jnp.where(own, rows, xs_ref[0, e])
                pltpu.store(gate_ref.at[0, e], gate, mask=own)


def _gather_tokens(ebound, h2, slot, aff, cap, jc=4):
    b, n, d = h2.shape
    e = slot.shape[1]
    n_chunks = n // TCH
    assert n_chunks % jc == 0
    tok = lambda i, j, *_: (i, j, 0)
    per_b = lambda i, j, *_: (i, 0, 0, 0)
    return pl.pallas_call(
        functools.partial(_gather_kernel, cap=cap, jc=jc, n_chunks=n_chunks),
        grid_spec=pltpu.PrefetchScalarGridSpec(
            num_scalar_prefetch=1,
            grid=(b, n_chunks // jc),
            in_specs=[pl.BlockSpec((1, jc * TCH, d), tok),
                      pl.BlockSpec((1, e, jc * TCH), lambda i, j, *_: (i, 0, j)),
                      pl.BlockSpec((1, e, jc * TCH), lambda i, j, *_: (i, 0, j))],
            out_specs=[pl.BlockSpec((1, e, cap, d), per_b),
                       pl.BlockSpec((1, e, cap, 1), per_b)]),
        out_shape=[jax.ShapeDtypeStruct((b, e, cap, d), BF16),
                   jax.ShapeDtypeStruct((b, e, cap, 1), F32)],
        compiler_params=_params("parallel", "arbitrary"),
        name="moe_gather",
    )(ebound, h2, slot, aff)


def _expert_kernel(xs_ref, wg_ref, wu_ref, wd_ref, gate_ref, y_ref, acc_ref, *, rb, n_f):
    f = pl.program_id(1)
    nb, _, cap, d = xs_ref.shape

    def step(first, last):
        wg = wg_ref[0].astype(BF16)
        wu = wu_ref[0].astype(BF16)
        wd = wd_ref[0].astype(BF16)
        for i in range(nb // rb):
            blk = slice(i * rb, (i + 1) * rb)
            xs = xs_ref[blk, 0].reshape(rb * cap, d)
            a = jnp.dot(xs, wg, preferred_element_type=F32)
            u = jnp.dot(xs, wu, preferred_element_type=F32)
            hm = (_silu(a) * u).astype(BF16)
            part = jnp.dot(hm, wd, preferred_element_type=F32).reshape(rb, cap, d)
            total = part if first else acc_ref[blk] + part
            if last:
                y_ref[blk, 0] = (total * gate_ref[blk, 0]).astype(BF16)
            else:
                acc_ref[blk] = total

    if n_f == 1:
        step(True, True)
    else:
        pl.when(f == 0)(lambda: step(True, False))
        if n_f > 2:
            pl.when(jnp.logical_and(f > 0, f < n_f - 1))(lambda: step(False, False))
        pl.when(f == n_f - 1)(lambda: step(False, True))


def _experts(xs, w_gate, w_up, w_down, gate, fc=768, rb=4):
    b, e, cap, d = xs.shape
    dff = w_gate.shape[2]
    per_e = lambda i, f: (0, i, 0, 0)
    return pl.pallas_call(
        functools.partial(_expert_kernel, rb=rb, n_f=dff // fc),
        grid=(e, dff // fc),
        in_specs=[pl.BlockSpec((b, 1, cap, d), per_e),
                  pl.BlockSpec((1, d, fc), lambda i, f: (i, 0, f)),
                  pl.BlockSpec((1, d, fc), lambda i, f: (i, 0, f)),
                  pl.BlockSpec((1, fc, d), lambda i, f: (i, f, 0)),
                  pl.BlockSpec((b, 1, cap, 1), per_e)],
        out_specs=pl.BlockSpec((b, 1, cap, d), per_e),
        out_shape=jax.ShapeDtypeStruct((b, e, cap, d), BF16),
        scratch_shapes=[pltpu.VMEM((b, cap, d), F32)],
        compiler_params=_params("parallel", "arbitrary"),
        name="moe_experts",
    )(xs, w_gate, w_up, w_down, gate)


def _combine_kernel(eb_ref, slot_t_ref, y_hbm, xn_ref, mod_ref, nf_ref, o_ref, ybuf, ysem, *, cap, jc, n_chunks):
    bi = pl.program_id(0)
    jo = pl.program_id(1)
    n_e = y_hbm.shape[1]
    cur = bi & 1

    def y_copy(sample, slot):
        return pltpu.make_async_copy(y_hbm.at[sample], ybuf.at[slot], ysem.at[slot])

    @pl.when(jnp.logical_and(bi == 0, jo == 0))
    def _():
        y_copy(0, 0).start()

    @pl.when(jo == 0)
    def _():
        y_copy(bi, cur).wait()

        @pl.when(bi + 1 < pl.num_programs(0))
        def _():
            y_copy(bi + 1, 1 - cur).start()

    y_ref = ybuf.at[cur]
    ci = lax.broadcasted_iota(jnp.int32, (TCH, EXPERT_GROUP * SLOT_W), 1)
    toks = [slice(jj * TCH, (jj + 1) * TCH) for jj in range(jc)]
    lane_e = lax.broadcasted_iota(jnp.int32, (1, n_e), 1)
    group_off = (lane_e & (EXPERT_GROUP - 1)) * SLOT_W

    def finish(tok, moe):
        x = xn_ref[0, tok] + mod_ref[0][5:6] * moe
        o_ref[0, tok] = _rms(x, nf_ref[...])

    moes = []
    for jj, tok in enumerate(toks):
        j = jo * jc + jj
        st = slot_t_ref[0, tok, :]
        starts = [_slot_window(eb_ref, bi, e, j, n_e, n_chunks, cap)[2] for e in range(n_e)]
        start_vec = jnp.zeros((1, n_e), jnp.int32)
        for e in range(n_e):
            start_vec = jnp.where(lane_e == e, starts[e], start_vec)
        rel = st - start_vec
        tgt = jnp.where(jnp.logical_and(rel >= 0, rel < SLOT_W), rel + group_off, -1)
        s_blocks, y_blocks = [], []
        for g in range(n_e // EXPERT_GROUP):
            col = None
            for q in reversed(range(EXPERT_GROUP)):
                e = EXPERT_GROUP * g + q
                col = tgt[:, e:e + 1] if col is None else jnp.where(ci < (q + 1) * SLOT_W, tgt[:, e:e + 1], col)
                y_blocks.insert(g * EXPERT_GROUP, y_ref[e, pl.ds(starts[e], SLOT_W), :])
            s_blocks.append(jnp.where(col == ci, 1.0, 0.0).astype(BF16))
        scat = jnp.concatenate(s_blocks, axis=1)
        ywin = jnp.concatenate(y_blocks, axis=0)
        moes.append(jnp.dot(scat, ywin, preferred_element_type=F32))
    for tok, moe in zip(toks, moes):
        finish(tok, moe)

    for jj, tok in enumerate(toks):
        @pl.when(_window_overflow(eb_ref, bi, jo * jc + jj, n_e, n_chunks, cap))
        def _():
            st = slot_t_ref[0, tok, :]
            cf = lax.broadcasted_iota(jnp.int32, (TCH, cap), 1)
            dense = jnp.concatenate(
                [jnp.where(st[:, e:e + 1] == cf, 1.0, 0.0).astype(BF16) for e in range(n_e)], axis=1)
            finish(tok, jnp.dot(dense, y_ref[...].reshape(n_e * cap, y_hbm.shape[3]),
                                preferred_element_type=F32))


def _combine(ebound, slot_t, y, x_new, mod, norm_final, cap, jc=4):
    b, n, d = x_new.shape
    e = slot_t.shape[2]
    n_chunks = n // TCH
    assert n_chunks % jc == 0
    tile = lambda i, j, *_: (i, j, 0)
    return pl.pallas_call(
        functools.partial(_combine_kernel, cap=cap, jc=jc, n_chunks=n_chunks),
        grid_spec=pltpu.PrefetchScalarGridSpec(
            num_scalar_prefetch=1,
            grid=(b, n_chunks // jc),
            in_specs=[pl.BlockSpec((1, jc * TCH, e), tile),
                      pl.BlockSpec(memory_space=pl.ANY),
                      pl.BlockSpec((1, jc * TCH, d), tile),
                      pl.BlockSpec((1, N_MOD, d), lambda i, j, *_: (i, 0, 0)),
                      pl.BlockSpec((1, d), lambda i, j, *_: (0, 0))],
            out_specs=pl.BlockSpec((1, jc * TCH, d), tile),
            scratch_shapes=[pltpu.VMEM((2, e, cap, d), BF16), pltpu.SemaphoreType.DMA((2,))]),
        out_shape=jax.ShapeDtypeStruct((b, n, d), F32),
        compiler_params=_params("arbitrary", "arbitrary"),
        name="moe_combine_norm",
    )(ebound, slot_t, y, x_new, mod, norm_final)


def kernel(x, c, ctx, c_ctx, w_mod, b_mod, norm1, w_in, rpb, w_s, b_s, gmlp_norm, out_norm_a, out_norm_b,
           w_out, norm2, w_router, w_gate, w_up, w_down, norm_final):
    b, n, d = x.shape
    assert w_mod.shape[0] == 1, "single-layer stack only"
    assert n % (GRID_W * ROWS_PER_STEP) == 0 and n // GRID_W >= NA_KH
    assert n % TCH == 0 and N_EXPERTS % EXPERT_GROUP == 0 and EXPERT_GROUP & (EXPERT_GROUP - 1) == 0
    cap = EC_CAPACITY_FACTOR * n // N_EXPERTS

    pad = (-(b + 1)) % 8
    cc = jnp.concatenate([c, c_ctx[None], jnp.zeros((pad, d), F32)], axis=0)
    m = _modulation(cc, w_mod[0], b_mod[0][None])
    mod = m.reshape(-1, N_MOD, d)

    w_in_b = w_in[0].astype(BF16)
    ws2 = w_s[0].astype(BF16).reshape(N_GROUPS_SG // 2, 2 * CHUNK, CHUNK)
    bs2 = jnp.broadcast_to(b_s[0].reshape(N_GROUPS_SG // 2, 2 * CHUNK, 1), (N_GROUPS_SG // 2, 2 * CHUNK, LANES))
    q, k, v, ob = _in_proj(x, mod, norm1, w_in_b, ws2, bs2, gmlp_norm, out_norm_b)

    kc, vc = _ctx_proj(ctx.reshape(b * CTX_LEN, d), mod, b, norm1, w_in_b)
    kc = kc.reshape(b, CTX_LEN, D_NA)
    vc = vc.reshape(b, CTX_LEN, D_NA)

    x_new, h2, aff = _attention_out(q, k, v, kc, vc, _bias_blocks(rpb[0]), out_norm_a,
                                    ob, x, mod, w_out[0].astype(BF16), norm2, w_router[0].T)

    slot, ebound = _topk_slots(aff.reshape(b * N_EXPERTS, n), cap)
    slot = slot.reshape(b, N_EXPERTS, n)
    ebound = ebound[:, :n // TCH + 1].reshape(-1)

    xs, gate = _gather_tokens(ebound, h2, slot, aff, cap)
    y = _experts(xs, w_gate[0], w_up[0], w_down[0], gate)
    return _combine(ebound, jnp.swapaxes(slot, 1, 2), y, x_new, mod, norm_final[None], cap)
```

```python
import functools

import numpy as np
import jax
import jax.numpy as jnp
from jax import lax
from jax.experimental import pallas as pl
from jax.experimental.pallas import tpu as pltpu

D_MODEL = 1024
GRID_W = 64
CTX_LEN = 256
N_HEADS_NA = 8
HEAD_DIM = 64
D_NA = N_HEADS_NA * HEAD_DIM
NA_KH = 8
NA_KW = 16
D_SG = D_MODEL - D_NA
N_GROUPS_SG = 8
SG_GROUP_DIM = D_SG // N_GROUPS_SG
CHUNK = 128
N_EXPERTS = 16
EC_CAPACITY_FACTOR = 2
D_EXPERT = 1536
N_MOD = 6
EPS = 1e-6
NEG_INF = -1e30
LOG2E = float(np.log2(np.e))

LANES = 128
VMEM_LIMIT = 56 * 1024 * 1024

F32 = jnp.float32
BF16 = jnp.bfloat16
HIGHEST = lax.Precision.HIGHEST

ROWS_PER_STEP = 16

MANTISSA_STEPS = 36

TCH = 256
SLOT_W = 64
SLOT_ALIGN = 16
EXPERT_GROUP = 4


def _params(*sem):
    return pltpu.CompilerParams(dimension_semantics=sem, vmem_limit_bytes=VMEM_LIMIT)


def _rms_mod(x, g, shift, scale):
    r = lax.rsqrt(jnp.mean(x * x, axis=-1, keepdims=True) + EPS)
    return (x * r) * g * (1.0 + scale) + shift


def _rms(x, g):
    return x * lax.rsqrt(jnp.mean(x * x, axis=-1, keepdims=True) + EPS) * g


def _gelu_tanh(x):
    return 0.5 * x * (1.0 + jnp.tanh(np.sqrt(2.0 / np.pi).astype(np.float32) * (x + 0.044715 * (x * x * x))))


def _silu(x):
    return x * jax.nn.sigmoid(x)


def _bf16_terms(x, n_terms):
    terms = []
    for _ in range(n_terms):
        t = x.astype(BF16)
        terms.append(t)
        x = x - t.astype(F32)
    return terms


def _mod_kernel(c_ref, w_ref, b_ref, o_ref):
    rows = c_ref.shape[0]
    s = jnp.concatenate(_bf16_terms(_silu(c_ref[...]), 3), axis=0)
    w_hi, w_lo = _bf16_terms(w_ref[...], 2)
    hi = jnp.dot(s, w_hi, preferred_element_type=F32)
    lo = jnp.dot(s[:2 * rows], w_lo, preferred_element_type=F32)
    small = (hi[2 * rows:] + lo[rows:]) + (hi[rows:2 * rows] + lo[:rows])
    o_ref[...] = (small + hi[:rows]) + b_ref[...]


def _modulation(cc, w_mod, b_mod):
    rows, d = cc.shape
    n = w_mod.shape[1]
    tn = 1024
    return pl.pallas_call(
        _mod_kernel,
        grid=(n // tn,),
        in_specs=[pl.BlockSpec((rows, d), lambda j: (0, 0)),
                  pl.BlockSpec((d, tn), lambda j: (0, j)),
                  pl.BlockSpec((1, tn), lambda j: (0, j))],
        out_specs=pl.BlockSpec((rows, tn), lambda j: (0, j)),
        out_shape=jax.ShapeDtypeStruct((rows, n), F32),
        compiler_params=_params("arbitrary"),
        name="modulation",
    )(cc, w_mod, b_mod)


def _in_kernel(x_ref, mod_ref, n1_ref, w_ref, ws_ref, bs_ref, gn_ref, onb_ref,
               q_ref, k_ref, v_ref, ob_ref, sp_ref, *, sub):
    mod = mod_ref[0]
    tiles = [slice(r, r + sub) for r in range(0, x_ref.shape[1], sub)]
    nch = sub // CHUNK
    lane = lax.broadcasted_iota(jnp.int32, (CHUNK, LANES), 1)
    first = lane < SG_GROUP_DIM

    def project(rs):
        hb = _rms_mod(x_ref[0, rs], n1_ref[...], mod[0:1], mod[1:2]).astype(BF16)
        qkv = jnp.dot(hb, w_ref[:, :3 * D_NA], preferred_element_type=F32)
        q_ref[0, rs] = (qkv[:, :D_NA] * (HEAD_DIM ** -0.5 * LOG2E)).astype(BF16)
        k_ref[0, rs] = qkv[:, D_NA:2 * D_NA].astype(BF16)
        v_ref[0, rs] = qkv[:, 2 * D_NA:].astype(BF16)
        return jnp.dot(hb, w_ref[:, 3 * D_NA:], preferred_element_type=F32)

    def gate_mlp(rs, uz):
        u = _gelu_tanh(uz[:, :D_SG])
        z = _gelu_tanh(uz[:, D_SG:])
        mu = jnp.mean(z, axis=-1, keepdims=True)
        zc = z - mu
        var = jnp.mean(zc * zc, axis=-1, keepdims=True)
        zb = (zc * lax.rsqrt(var + EPS) * gn_ref[...]).astype(BF16)
        for p in range(N_GROUPS_SG // 2):
            zp = jnp.concatenate(
                [zb[c * CHUNK:(c + 1) * CHUNK, p * LANES:(p + 1) * LANES] for c in range(nch)], axis=1)
            r = jnp.dot(ws_ref[p], zp, preferred_element_type=F32)
            bs = bs_ref[p]
            for c in range(nch):
                top = r[:CHUNK, c * LANES:(c + 1) * LANES] + bs[:CHUNK]
                bot = r[CHUNK:, c * LANES:(c + 1) * LANES] + bs[CHUNK:]
                sp_ref[rs.start + c * CHUNK:rs.start + (c + 1) * CHUNK, p * LANES:(p + 1) * LANES] = (
                    jnp.where(first, top, bot))
        ob = u * sp_ref[rs, :]
        ob_ref[0, rs] = _rms(ob, onb_ref[...]).astype(BF16)

    uz = {0: project(tiles[0])}
    for i, rs in enumerate(tiles):
        if i + 1 < len(tiles):
            uz[i + 1] = project(tiles[i + 1])
        gate_mlp(rs, uz.pop(i))


def _in_proj(x, mod, norm1, w_in_b, ws2, bs2, gmlp_norm, out_norm_b, tm=2048, sub=256):
    b, n, d = x.shape
    d_in = w_in_b.shape[1]
    full2 = lambda i, j: (0, 0)
    full3 = lambda i, j: (0, 0, 0)
    tile = lambda i, j: (i, j, 0)
    act = jax.ShapeDtypeStruct((b, n, D_NA), BF16)
    return pl.pallas_call(
        functools.partial(_in_kernel, sub=sub),
        grid=(b, n // tm),
        in_specs=[pl.BlockSpec((1, tm, d), tile),
                  pl.BlockSpec((1, N_MOD, d), lambda i, j: (i, 0, 0)),
                  pl.BlockSpec((1, d), full2),
                  pl.BlockSpec((d, d_in), full2),
                  pl.BlockSpec(ws2.shape, full3),
                  pl.BlockSpec(bs2.shape, full3),
                  pl.BlockSpec((1, D_SG), full2),
                  pl.BlockSpec((1, D_SG), full2)],
        out_specs=[pl.BlockSpec((1, tm, D_NA), tile)] * 4,
        out_shape=[act] * 4,
        scratch_shapes=[pltpu.VMEM((tm, D_SG), F32)],
        compiler_params=_params("parallel", "arbitrary"),
        name="in_proj_gmlp",
    )(x, mod, norm1, w_in_b, ws2, bs2, gmlp_norm, out_norm_b)


def _ctx_kernel(x_ref, mod_ref, n1_ref, wk_ref, wv_ref, k_ref, v_ref):
    mod = mod_ref[0]
    hb = _rms_mod(x_ref[...], n1_ref[...], mod[0:1], mod[1:2]).astype(BF16)
    k_ref[...] = jnp.dot(hb, wk_ref[...], preferred_element_type=F32).astype(BF16)
    v_ref[...] = jnp.dot(hb, wv_ref[...], preferred_element_type=F32).astype(BF16)


def _ctx_proj(ctx2, mod, mod_row, norm1, w_in_b, tm=512):
    rows, d = ctx2.shape
    act = jax.ShapeDtypeStruct((rows, D_NA), BF16)
    return pl.pallas_call(
        _ctx_kernel,
        grid=(rows // tm,),
        in_specs=[pl.BlockSpec((tm, d), lambda i: (i, 0)),
                  pl.BlockSpec((1, N_MOD, d), lambda i: (mod_row, 0, 0)),
                  pl.BlockSpec((1, d), lambda i: (0, 0)),
                  pl.BlockSpec((d, D_NA), lambda i: (0, 1)),
                  pl.BlockSpec((d, D_NA), lambda i: (0, 2))],
        out_specs=[pl.BlockSpec((tm, D_NA), lambda i: (i, 0))] * 2,
        out_shape=[act, act],
        compiler_params=_params("arbitrary"),
        name="ctx_kv_proj",
    )(ctx2, mod, norm1, w_in_b, w_in_b)


def _attn_out_kernel(q_ref, k_ref, v_ref, kc_ref, vc_ref, pb_ref, ona_ref,
                     ob_ref, x_ref, mod_ref, w_ref, n2_ref, wr_ref, xn_ref, h2_ref, aff_ref, *, rows, sub):
    kc = kc_ref[0]
    vc = vc_ref[0]
    nk = NA_KH * GRID_W
    lane = lax.broadcasted_iota(jnp.int32, (GRID_W, LANES), 1)
    first = lane < HEAD_DIM
    nt = (((1,), (1,)), ((), ()))
    units = []
    for r in range(ROWS_PER_STEP):
        row = pl.program_id(1) * ROWS_PER_STEP + r
        rs = jnp.clip(row - NA_KH // 2, 0, rows - NA_KH)
        start = pl.multiple_of(rs * GRID_W, GRID_W)
        off = rs - row + NA_KH - 1
        units += [(r, start, off, p) for p in range(N_HEADS_NA // 2)]

    def scores(r, start, off, p):
        sl = slice(p * LANES, (p + 1) * LANES)
        qp = q_ref[0, r * GRID_W:(r + 1) * GRID_W, sl]
        q2 = jnp.concatenate([jnp.where(first, qp, jnp.zeros_like(qp)),
                              jnp.where(first, jnp.zeros_like(qp), qp)], axis=0)
        bias = jnp.concatenate(
            [jnp.concatenate([pb_ref[2 * p + j, off + 2 * c] for c in range(NA_KH // 2)], axis=1)
             for j in range(2)], axis=0)
        s_nb = lax.dot_general(q2, k_ref[0, pl.ds(start, nk), sl], nt, preferred_element_type=F32) + bias
        s_cx = lax.dot_general(q2, kc[:, sl], nt, preferred_element_type=F32)
        return s_nb, s_cx

    def softmax(s_nb, s_cx):
        m = jnp.maximum(jnp.max(s_nb, axis=1, keepdims=True), jnp.max(s_cx, axis=1, keepdims=True))
        e_nb = jnp.exp2(s_nb - m)
        e_cx = jnp.exp2(s_cx - m)
        l = jnp.sum(e_nb, axis=1, keepdims=True) + jnp.sum(e_cx, axis=1, keepdims=True)
        return e_nb.astype(BF16), e_cx.astype(BF16), l

    def values(r, start, off, p, e_nb, e_cx, l):
        sl = slice(p * LANES, (p + 1) * LANES)
        o2 = (jnp.dot(e_nb, v_ref[0, pl.ds(start, nk), sl], preferred_element_type=F32)
              + jnp.dot(e_cx, vc[:, sl], preferred_element_type=F32)) / l
        return jnp.where(first, o2[:GRID_W], o2[GRID_W:])

    s, pr, o = {}, {}, {}
    n_u = len(units)
    for step in range(n_u + 2):
        if step < n_u:
            s[step] = scores(*units[step])
        if 0 <= step - 1 < n_u:
            pr[step - 1] = softmax(*s.pop(step - 1))
        if 0 <= step - 2 < n_u:
            o[step - 2] = values(*units[step - 2], *pr.pop(step - 2))
    n_p = N_HEADS_NA // 2
    oa_rows = []
    for r in range(ROWS_PER_STEP):
        out = jnp.concatenate([o[r * n_p + p] for p in range(n_p)], axis=1)
        oa_rows.append(_rms(out, ona_ref[...]).astype(BF16))

    mod = mod_ref[0]
    w_hi, w_lo = _bf16_terms(wr_ref[...], 2)
    w2 = jnp.concatenate([w_hi, w_lo], axis=0)
    rps = sub // GRID_W
    tiles = [slice(t0, t0 + sub) for t0 in range(0, ROWS_PER_STEP * GRID_W, sub)]
    mixes = [jnp.dot(jnp.concatenate(oa_rows[i * rps:(i + 1) * rps], axis=0), w_ref[:D_NA],
                     preferred_element_type=F32)
             + jnp.dot(ob_ref[0, rs], w_ref[D_NA:], preferred_element_type=F32) for i, rs in enumerate(tiles)]
    for rs, mix in zip(tiles, mixes):
        xn = x_ref[0, rs] + mod[2:3] * mix
        xn_ref[0, rs] = xn
        h2 = _rms_mod(xn, n2_ref[...], mod[3:4], mod[4:5])
        h_hi, h_lo = _bf16_terms(h2, 2)
        h2_ref[0, rs] = h_hi
        l_hi = lax.dot_general(w2, h_hi, nt, preferred_element_type=F32)
        l_lo = lax.dot_general(w_hi, h_lo, nt, preferred_element_type=F32)
        logits = l_hi[:N_EXPERTS] + l_hi[N_EXPERTS:] + l_lo
        e = jnp.exp(logits - jnp.max(logits, axis=0, keepdims=True))
        aff_ref[0, :, rs] = e / jnp.sum(e, axis=0, keepdims=True)


def _bias_blocks(rpb):
    n_ro, n_co = 2 * NA_KH - 1, 2 * NA_KW - 1
    qc = np.arange(GRID_W)[:, None]
    kc = np.arange(GRID_W)[None, :]
    cs = np.clip(qc - NA_KW // 2, 0, GRID_W - NA_KW)
    col_ok = (kc >= cs) & (kc < cs + NA_KW)
    spread = ((kc - qc + NA_KW - 1)[None] == np.arange(n_co)[:, None, None]) & col_ok[None]
    blocks = jnp.dot(rpb.reshape(-1, n_co), jnp.asarray(spread.reshape(n_co, -1), F32), precision=HIGHEST)
    blocks = jnp.where(jnp.asarray(col_ok.reshape(-1)), blocks * LOG2E, NEG_INF)
    blocks = blocks.reshape(N_HEADS_NA, n_ro, GRID_W, GRID_W)
    return jnp.concatenate([blocks[:, :-1], blocks[:, 1:]], axis=-1)


def _attention_out(q, k, v, kc, vc, pb, out_norm_a, ob, x, mod, w_out_b, norm2, w_router_t, sub=512):
    b, n, d = x.shape
    rows = n // GRID_W
    tq = ROWS_PER_STEP * GRID_W
    per_b = lambda i, t: (i, 0, 0)
    tile = lambda i, t: (i, t, 0)
    full = lambda i, t: (0, 0)
    return pl.pallas_call(
        functools.partial(_attn_out_kernel, rows=rows, sub=sub),
        grid=(b, n // tq),
        in_specs=[pl.BlockSpec((1, tq, D_NA), tile),
                  pl.BlockSpec((1, n, D_NA), per_b),
                  pl.BlockSpec((1, n, D_NA), per_b),
                  pl.BlockSpec((1, CTX_LEN, D_NA), per_b),
                  pl.BlockSpec((1, CTX_LEN, D_NA), per_b),
                  pl.BlockSpec(pb.shape, lambda i, t: (0, 0, 0, 0)),
                  pl.BlockSpec((1, D_NA), full),
                  pl.BlockSpec((1, tq, D_SG), tile),
                  pl.BlockSpec((1, tq, d), tile),
                  pl.BlockSpec((1, N_MOD, d), per_b),
                  pl.BlockSpec(w_out_b.shape, full),
                  pl.BlockSpec((1, d), full),
                  pl.BlockSpec((N_EXPERTS, d), full)],
        out_specs=[pl.BlockSpec((1, tq, d), tile),
                   pl.BlockSpec((1, tq, d), tile),
                   pl.BlockSpec((1, N_EXPERTS, tq), lambda i, t: (i, 0, t))],
        out_shape=[jax.ShapeDtypeStruct((b, n, d), F32),
                   jax.ShapeDtypeStruct((b, n, d), BF16),
                   jax.ShapeDtypeStruct((b, N_EXPERTS, n), F32)],
        compiler_params=_params("parallel", "arbitrary"),
        name="attention_out_router",
    )(q, k, v, kc, vc, pb, out_norm_a, ob, x, mod, w_out_b, norm2, w_router_t)


def _prefix_count(mask_f, tri):
    rows, n = mask_f.shape
    parts = []
    carry = jnp.zeros((rows, 1), F32)
    for j in range(n // LANES):
        blk = mask_f[:, j * LANES:(j + 1) * LANES]
        parts.append(jnp.dot(blk.astype(BF16), tri, preferred_element_type=F32) + carry)
        carry = carry + jnp.sum(blk, axis=1, keepdims=True)
    return jnp.concatenate(parts, axis=1)


def _topk_kernel(aff_ref, slot_ref, eb_ref, *, cap):
    a = aff_ref[...]
    rows = a.shape[0]

    def enough(t):
        return jnp.sum(jnp.where(a >= t, 1.0, 0.0), axis=1, keepdims=True) >= cap

    tiny = jnp.full((rows, 1), float(np.finfo(np.float32).tiny), F32)
    normal = enough(tiny)
    pw = tiny
    hi = jnp.full((rows, 1), 4.0, F32)
    for bit in range(6, -1, -1):
        cand = pw * (2.0 ** (1 << bit))
        ok = enough(cand)
        pw = jnp.where(ok, cand, pw)
        hi = jnp.where(ok, hi, cand)
    lo = jnp.where(normal, pw, 0.0)
    hi = jnp.where(normal, hi, tiny)
    step = lo
    for _ in range(MANTISSA_STEPS):
        step = step * 0.5
        cand = lo + step
        ok = enough(cand)
        lo = jnp.where(ok, cand, lo)
        hi = jnp.where(ok, hi, cand)
    above = a >= hi
    tie = jnp.logical_and(a >= lo, jnp.logical_not(above))
    n_above = jnp.sum(jnp.where(above, 1.0, 0.0), axis=1, keepdims=True)
    ri = lax.broadcasted_iota(jnp.int32, (LANES, LANES), 0)
    ci = lax.broadcasted_iota(jnp.int32, (LANES, LANES), 1)
    tri = jnp.where(ri <= ci, 1.0, 0.0).astype(BF16)
    tie_rank = _prefix_count(jnp.where(tie, 1.0, 0.0), tri)
    sel = jnp.logical_or(above, jnp.logical_and(tie, tie_rank <= cap - n_above))
    sel_f = jnp.where(sel, 1.0, 0.0)
    pos = _prefix_count(sel_f, tri) - 1.0
    slot_ref[...] = jnp.where(sel, pos, -1.0).astype(jnp.int32)
    lane = lax.broadcasted_iota(jnp.int32, (rows, LANES), 1)
    cnt = jnp.zeros((rows, LANES), F32)
    for j in range(a.shape[1] // TCH):
        cnt = jnp.where(lane == j, jnp.sum(sel_f[:, j * TCH:(j + 1) * TCH], axis=1, keepdims=True), cnt)
    before = jnp.where(ri < ci, 1.0, 0.0).astype(BF16)
    eb_ref[...] = jnp.dot(cnt.astype(BF16), before, preferred_element_type=F32).astype(jnp.int32)


def _topk_slots(aff2, cap):
    rows, n = aff2.shape
    tr = rows
    return pl.pallas_call(
        functools.partial(_topk_kernel, cap=cap),
        grid=(rows // tr,),
        in_specs=[pl.BlockSpec((tr, n), lambda i: (i, 0))],
        out_specs=[pl.BlockSpec((tr, n), lambda i: (i, 0)), pl.BlockSpec((tr, LANES), lambda i: (i, 0))],
        out_shape=[jax.ShapeDtypeStruct((rows, n), jnp.int32), jax.ShapeDtypeStruct((rows, LANES), jnp.int32)],
        compiler_params=_params("parallel"),
        name="expert_topk",
    )(aff2)


def _slot_window(eb_ref, bi, e, j, n_e, n_chunks, cap):
    base = (bi * n_e + e) * (n_chunks + 1) + j
    s0 = eb_ref[base]
    s1 = eb_ref[base + 1]
    start = jnp.minimum(s0 & -SLOT_ALIGN, cap - SLOT_W)
    return s0, s1, pl.multiple_of(start, SLOT_ALIGN)


def _window_overflow(eb_ref, bi, j, n_e, n_chunks, cap):
    over = None
    for e in range(n_e):
        _, s1, start = _slot_window(eb_ref, bi, e, j, n_e, n_chunks, cap)
        o = s1 > start + SLOT_W
        over = o if over is None else jnp.logical_or(over, o)
    return over


def _gather_kernel(eb_ref, h2_ref, slot_ref, aff_ref, xs_ref, gate_ref, *, cap, jc, n_chunks):
    bi = pl.program_id(0)
    jo = pl.program_id(1)
    n_e = slot_ref.shape[1]
    wi = lax.broadcasted_iota(jnp.int32, (SLOT_W, TCH), 0)
    wcol = lax.broadcasted_iota(jnp.int32, (SLOT_W, 1), 0)

    @pl.when(jo == 0)
    def _():
        xs_ref[...] = jnp.zeros_like(xs_ref)
        gate_ref[...] = jnp.zeros_like(gate_ref)

    for jj in range(jc):
        j = jo * jc + jj
        tok = slice(jj * TCH, (jj + 1) * TCH)
        h2c = h2_ref[0, tok, :]
        for g in range(n_e // EXPERT_GROUP):
            blocks, meta = [], []
            for q in range(EXPERT_GROUP):
                e = EXPERT_GROUP * g + q
                s0, s1, start = _slot_window(eb_ref, bi, e, j, n_e, n_chunks, cap)
                hit = slot_ref[0, e:e + 1, tok] - start == wi
                blocks.append(jnp.where(hit, 1.0, 0.0).astype(BF16))
                gate = jnp.sum(jnp.where(hit, aff_ref[0, e:e + 1, tok], 0.0), axis=1, keepdims=True)
                meta.append((e, s0, s1, start, gate))
            onehot = jnp.concatenate(blocks, axis=0)
            rows = jnp.dot(onehot, h2c, preferred_element_type=F32).astype(BF16)
            for q, (e, s0, s1, start, gate) in enumerate(meta):
                own = jnp.logical_and(wcol + start >= s0, wcol + start < s1)
                win = pl.ds(start, SLOT_W)
                xs_ref[0, e, win, :] = jnp.where(own, rows[q * SLOT_W:(q + 1) * SLOT_W], xs_ref[0, e, win, :])
                gate_ref[0, e, win, :] = jnp.where(own, gate, gate_ref[0, e, win, :])

    for jj in range(jc):
        j = jo * jc + jj
        tok = slice(jj * TCH, (jj + 1) * TCH)

        @pl.when(_window_overflow(eb_ref, bi, j, n_e, n_chunks, cap))
        def _():
            h2c = h2_ref[0, tok, :]
            ci = lax.broadcasted_iota(jnp.int32, (cap, TCH), 0)
            ccol = lax.broadcasted_iota(jnp.int32, (cap, 1), 0)
            for e in range(n_e):
                s0, s1, _ = _slot_window(eb_ref, bi, e, j, n_e, n_chunks, cap)
                hit = slot_ref[0, e:e + 1, tok] == ci
                rows = jnp.dot(jnp.where(hit, 1.0, 0.0).astype(BF16), h2c, preferred_element_type=F32).astype(BF16)
                gate = jnp.sum(jnp.where(hit, aff_ref[0, e:e + 1, tok], 0.0), axis=1, keepdims=True)
                own = jnp.logical_and(ccol >= s0, ccol < s1)
                xs_ref[0, e] = jnp.where(own, rows, xs_ref[0, e])
                gate_ref[0, e] = jnp.where(own, gate, gate_ref[0, e])


def _gather_tokens(ebound, h2, slot, aff, cap, jc=4):
    b, n, d = h2.shape
    e = slot.shape[1]
    n_chunks = n // TCH
    assert n_chunks % jc == 0
    tok = lambda i, j, *_: (i, j, 0)
    per_b = lambda i, j, *_: (i, 0, 0, 0)
    return pl.pallas_call(
        functools.partial(_gather_kernel, cap=cap, jc=jc, n_chunks=n_chunks),
        grid_spec=pltpu.PrefetchScalarGridSpec(
            num_scalar_prefetch=1,
            grid=(b, n_chunks // jc),
            in_specs=[pl.BlockSpec((1, jc * TCH, d), tok),
                      pl.BlockSpec((1, e, jc * TCH), lambda i, j, *_: (i, 0, j)),
                      pl.BlockSpec((1, e, jc * TCH), lambda i, j, *_: (i, 0, j))],
            out_specs=[pl.BlockSpec((1, e, cap, d), per_b),
                       pl.BlockSpec((1, e, cap, 1), per_b)]),
        out_shape=[jax.ShapeDtypeStruct((b, e, cap, d), BF16),
                   jax.ShapeDtypeStruct((b, e, cap, 1), F32)],
        compiler_params=_params("parallel", "arbitrary"),
        name="moe_gather",
    )(ebound, h2, slot, aff)


def _expert_kernel(xs_ref, wg_ref, wu_ref, wd_ref, gate_ref, y_ref, acc_ref, *, rb, n_f):
    f = pl.program_id(1)
    nb, _, cap, d = xs_ref.shape

    def step(first, last):
        wg = wg_ref[0].astype(BF16)
        wu = wu_ref[0].astype(BF16)
        wd = wd_ref[0].astype(BF16)
        for i in range(nb // rb):
            blk = slice(i * rb, (i + 1) * rb)
            xs = xs_ref[blk, 0].reshape(rb * cap, d)
            a = jnp.dot(xs, wg, preferred_element_type=F32)
            u = jnp.dot(xs, wu, preferred_element_type=F32)
            hm = (_silu(a) * u).astype(BF16)
            part = jnp.dot(hm, wd, preferred_element_type=F32).reshape(rb, cap, d)
            total = part if first else acc_ref[blk] + part
            if last:
                y_ref[blk, 0] = (total * gate_ref[blk, 0]).astype(BF16)
            else:
                acc_ref[blk] = total

    if n_f == 1:
        step(True, True)
    else:
        pl.when(f == 0)(lambda: step(True, False))
        if n_f > 2:
            pl.when(jnp.logical_and(f > 0, f < n_f - 1))(lambda: step(False, False))
        pl.when(f == n_f - 1)(lambda: step(False, True))


def _experts(xs, w_gate, w_up, w_down, gate, fc=768, rb=4):
    b, e, cap, d = xs.shape
    dff = w_gate.shape[2]
    per_e = lambda i, f: (0, i, 0, 0)
    return pl.pallas_call(
        functools.partial(_expert_kernel, rb=rb, n_f=dff // fc),
        grid=(e, dff // fc),
        in_specs=[pl.BlockSpec((b, 1, cap, d), per_e),
                  pl.BlockSpec((1, d, fc), lambda i, f: (i, 0, f)),
                  pl.BlockSpec((1, d, fc), lambda i, f: (i, 0, f)),
                  pl.BlockSpec((1, fc, d), lambda i, f: (i, f, 0)),
                  pl.BlockSpec((b, 1, cap, 1), per_e)],
        out_specs=pl.BlockSpec((b, 1, cap, d), per_e),
        out_shape=jax.ShapeDtypeStruct((b, e, cap, d), BF16),
        scratch_shapes=[pltpu.VMEM((b, cap, d), F32)],
        compiler_params=_params("parallel", "arbitrary"),
        name="moe_experts",
    )(xs, w_gate, w_up, w_down, gate)


def _combine_kernel(eb_ref, slot_t_ref, y_hbm, xn_ref, mod_ref, nf_ref, o_ref, ybuf, ysem, *, cap, jc, n_chunks):
    bi = pl.program_id(0)
    jo = pl.program_id(1)
    n_e = y_hbm.shape[1]
    cur = bi & 1

    def y_copy(sample, slot):
        return pltpu.make_async_copy(y_hbm.at[sample], ybuf.at[slot], ysem.at[slot])

    @pl.when(jnp.logical_and(bi == 0, jo == 0))
    def _():
        y_copy(0, 0).start()

    @pl.when(jo == 0)
    def _():
        y_copy(bi, cur).wait()

        @pl.when(bi + 1 < pl.num_programs(0))
        def _():
            y_copy(bi + 1, 1 - cur).start()

    y_ref = ybuf.at[cur]
    ci = lax.broadcasted_iota(jnp.int32, (TCH, EXPERT_GROUP * SLOT_W), 1)
    toks = [slice(jj * TCH, (jj + 1) * TCH) for jj in range(jc)]
    lane_e = lax.broadcasted_iota(jnp.int32, (1, n_e), 1)
    group_off = (lane_e & (EXPERT_GROUP - 1)) * SLOT_W

    def finish(tok, moe):
        x = xn_ref[0, tok] + mod_ref[0][5:6] * moe
        o_ref[0, tok] = _rms(x, nf_ref[...])

    moes = []
    for jj, tok in enumerate(toks):
        j = jo * jc + jj
        st = slot_t_ref[0, tok, :]
        starts = [_slot_window(eb_ref, bi, e, j, n_e, n_chunks, cap)[2] for e in range(n_e)]
        start_vec = jnp.zeros((1, n_e), jnp.int32)
        for e in range(n_e):
            start_vec = jnp.where(lane_e == e, starts[e], start_vec)
        rel = st - start_vec
        tgt = jnp.where(jnp.logical_and(rel >= 0, rel < SLOT_W), rel + group_off, -1)
        s_blocks, y_blocks = [], []
        for g in range(n_e // EXPERT_GROUP):
            col = None
            for q in reversed(range(EXPERT_GROUP)):
                e = EXPERT_GROUP * g + q
                col = tgt[:, e:e + 1] if col is None else jnp.where(ci < (q + 1) * SLOT_W, tgt[:, e:e + 1], col)
                y_blocks.insert(g * EXPERT_GROUP, y_ref[e, pl.ds(starts[e], SLOT_W), :])
            s_blocks.append(jnp.where(col == ci, 1.0, 0.0).astype(BF16))
        scat = jnp.concatenate(s_blocks, axis=1)
        ywin = jnp.concatenate(y_blocks, axis=0)
        moes.append(jnp.dot(scat, ywin, preferred_element_type=F32))
    for tok, moe in zip(toks, moes):
        finish(tok, moe)

    for jj, tok in enumerate(toks):
        @pl.when(_window_overflow(eb_ref, bi, jo * jc + jj, n_e, n_chunks, cap))
        def _():
            st = slot_t_ref[0, tok, :]
            cf = lax.broadcasted_iota(jnp.int32, (TCH, cap), 1)
            dense = jnp.concatenate(
                [jnp.where(st[:, e:e + 1] == cf, 1.0, 0.0).astype(BF16) for e in range(n_e)], axis=1)
            finish(tok, jnp.dot(dense, y_ref[...].reshape(n_e * cap, y_hbm.shape[3]),
                                preferred_element_type=F32))


def _combine(ebound, slot_t, y, x_new, mod, norm_final, cap, jc=4):
    b, n, d = x_new.shape
    e = slot_t.shape[2]
    n_chunks = n // TCH
    assert n_chunks % jc == 0
    tile = lambda i, j, *_: (i, j, 0)
    return pl.pallas_call(
        functools.partial(_combine_kernel, cap=cap, jc=jc, n_chunks=n_chunks),
        grid_spec=pltpu.PrefetchScalarGridSpec(
            num_scalar_prefetch=1,
            grid=(b, n_chunks // jc),
            in_specs=[pl.BlockSpec((1, jc * TCH, e), tile),
                      pl.BlockSpec(memory_space=pl.ANY),
                      pl.BlockSpec((1, jc * TCH, d), tile),
                      pl.BlockSpec((1, N_MOD, d), lambda i, j, *_: (i, 0, 0)),
                      pl.BlockSpec((1, d), lambda i, j, *_: (0, 0))],
            out_specs=pl.BlockSpec((1, jc * TCH, d), tile),
            scratch_shapes=[pltpu.VMEM((2, e, cap, d), BF16), pltpu.SemaphoreType.DMA((2,))]),
        out_shape=jax.ShapeDtypeStruct((b, n, d), F32),
        compiler_params=_params("arbitrary", "arbitrary"),
        name="moe_combine_norm",
    )(ebound, slot_t, y, x_new, mod, norm_final)


def kernel(x, c, ctx, c_ctx, w_mod, b_mod, norm1, w_in, rpb, w_s, b_s, gmlp_norm, out_norm_a, out_norm_b,
           w_out, norm2, w_router, w_gate, w_up, w_down, norm_final):
    b, n, d = x.shape
    assert w_mod.shape[0] == 1, "single-layer stack only"
    assert n % (GRID_W * ROWS_PER_STEP) == 0 and n // GRID_W >= NA_KH
    assert n % TCH == 0 and N_EXPERTS % EXPERT_GROUP == 0 and EXPERT_GROUP & (EXPERT_GROUP - 1) == 0
    cap = EC_CAPACITY_FACTOR * n // N_EXPERTS

    pad = (-(b + 1)) % 8
    cc = jnp.concatenate([c, c_ctx[None], jnp.zeros((pad, d), F32)], axis=0)
    m = _modulation(cc, w_mod[0], b_mod[0][None])
    mod = m.reshape(-1, N_MOD, d)

    w_in_b = w_in[0].astype(BF16)
    ws2 = w_s[0].astype(BF16).reshape(N_GROUPS_SG // 2, 2 * CHUNK, CHUNK)
    bs2 = jnp.broadcast_to(b_s[0].reshape(N_GROUPS_SG // 2, 2 * CHUNK, 1), (N_GROUPS_SG // 2, 2 * CHUNK, LANES))
    q, k, v, ob = _in_proj(x, mod, norm1, w_in_b, ws2, bs2, gmlp_norm, out_norm_b)

    kc, vc = _ctx_proj(ctx.reshape(b * CTX_LEN, d), mod, b, norm1, w_in_b)
    kc = kc.reshape(b, CTX_LEN, D_NA)
    vc = vc.reshape(b, CTX_LEN, D_NA)

    x_new, h2, aff = _attention_out(q, k, v, kc, vc, _bias_blocks(rpb[0]), out_norm_a,
                                    ob, x, mod, w_out[0].astype(BF16), norm2, w_router[0].T)

    slot, ebound = _topk_slots(aff.reshape(b * N_EXPERTS, n), cap)
    slot = slot.reshape(b, N_EXPERTS, n)
    ebound = ebound[:, :n // TCH + 1].reshape(-1)

    xs, gate = _gather_tokens(ebound, h2, slot, aff, cap)
    y = _experts(xs, w_gate[0], w_up[0], w_down[0], gate)
    return _combine(ebound, jnp.swapaxes(slot, 1, 2), y, x_new, mod, norm_final[None], cap)
```
